```python
import jax, jax.numpy as jnp
from jax import lax
import numpy as np

D_MODEL = 1024
BATCH = 4
SEQ = 4096
DEPTH = 4
DEC_BATCH = 32
DEC_SEQ = 1
PAST_LEN = 8192
PAGE_SIZE = 128

N_MIXERS = 4
EPS = 1e-6
PLE_DIM = 256
D_FF = ((8 * D_MODEL + 767) // 768) * 256
ML_HEADS = 8
ML_DQK = D_MODEL // (2 * ML_HEADS)
ML_DV = D_MODEL // ML_HEADS
ML_CHUNK = 64
ML_IN = 2 * ML_HEADS * ML_DQK + 2 * D_MODEL + 2 * ML_HEADS
POOL_WINDOWS = (2, 4, 8, 16)
POOL_GROUPS = len(POOL_WINDOWS)
POOL_GW = D_MODEL // POOL_GROUPS
POOL_STATE = max(POOL_WINDOWS) - 1
SB_HEADS = 16
SB_DH = D_MODEL // SB_HEADS
SB_QBLOCK = 128
SB_LOGIT_BIAS = -5.0
CONV_WIDTH = 31
CONV_STATE = CONV_WIDTH - 1
N_ML_LAYERS = len(range(0, DEPTH, N_MIXERS))
N_PL_LAYERS = len(range(1, DEPTH, N_MIXERS))
N_SB_LAYERS = len(range(2, DEPTH, N_MIXERS))
N_CV_LAYERS = len(range(3, DEPTH, N_MIXERS))

kernel_name = 'hybrid_mlstm_pool_stickbreak_conformer_step'


def _rmsnorm(x, g):
    xf = x.astype(jnp.float32)
    y = xf * lax.rsqrt(jnp.mean(xf * xf, axis=-1, keepdims=True) + EPS)
    return (y * g.astype(jnp.float32)).astype(x.dtype)


def _layernorm(x, g, b):
    xf = x.astype(jnp.float32)
    xc = xf - jnp.mean(xf, axis=-1, keepdims=True)
    y = xc * lax.rsqrt(jnp.mean(xc * xc, axis=-1, keepdims=True) + EPS)
    return (y * g.astype(jnp.float32) + b.astype(jnp.float32)).astype(x.dtype)


def _mlstm_chunk(carry, inp):
    c, n, m = carry
    q, k, v, log_i, log_f = inp
    L = q.shape[1]
    b = jnp.cumsum(log_f, axis=1)
    causal = jnp.tril(jnp.ones((L, L), bool))
    dlog = b[:, :, None, :] - b[:, None, :, :] + log_i[:, None, :, :]
    dlog = jnp.where(causal[None, :, :, None], dlog, -jnp.inf)
    inter = b + m[:, None, :]
    m_t = jnp.maximum(inter, jnp.max(dlog, axis=2))
    w_intra = jnp.exp(dlog - m_t[:, :, None, :])
    w_inter = jnp.exp(inter - m_t)
    s_mat = w_intra * jnp.einsum('bthd,bshd->btsh', q, k)
    num = w_inter[..., None] * jnp.einsum('bthd,bhde->bthe', q, c) + jnp.einsum('btsh,bshe->bthe', s_mat, v)
    den = w_inter * jnp.einsum('bthd,bhd->bth', q, n) + jnp.sum(s_mat, axis=2)
    h = num / jnp.maximum(jnp.abs(den), jnp.exp(-m_t))[..., None]
    g_end = b[:, -1, :]
    dl_end = g_end[:, None, :] - b + log_i
    m_new = jnp.maximum(g_end + m, jnp.max(dl_end, axis=1))
    a_prev = jnp.exp(g_end + m - m_new)
    w_end = jnp.exp(dl_end - m_new[:, None, :])
    c_new = a_prev[..., None, None] * c + jnp.einsum('bsh,bshd,bshe->bhde', w_end, k, v)
    n_new = a_prev[..., None] * n + jnp.einsum('bsh,bshd->bhd', w_end, k)
    return (c_new, n_new, m_new), h


def _mlstm_mixer(u, c0, n0, m0, w_in, b_gate, g_out, w_out):
    B, L, _ = u.shape
    H, dk, dv = ML_HEADS, ML_DQK, ML_DV
    f32 = jnp.float32
    z = u @ w_in
    o0 = 2 * H * dk
    q = z[..., :H * dk].reshape(B, L, H, dk).astype(f32)
    k = z[..., H * dk:o0].reshape(B, L, H, dk).astype(f32) * (dk ** -0.5)
    v = z[..., o0:o0 + D_MODEL].reshape(B, L, H, dv).astype(f32)
    o_gate = jax.nn.sigmoid(z[..., o0 + D_MODEL:o0 + 2 * D_MODEL])
    gates = z[..., o0 + 2 * D_MODEL:].astype(f32) + b_gate.astype(f32)
    log_i = gates[..., :H]
    log_f = jax.nn.log_sigmoid(gates[..., H:])
    chunk = ML_CHUNK if L % ML_CHUNK == 0 else L
    nc = L // chunk

    def to_chunks(a):
        return jnp.moveaxis(a.reshape((B, nc, chunk) + a.shape[2:]), 1, 0)

    init = (c0.astype(f32), n0.astype(f32), m0.astype(f32))
    (c1, n1, m1), h = lax.scan(_mlstm_chunk, init, (to_chunks(q), to_chunks(k), to_chunks(v), to_chunks(log_i), to_chunks(log_f)))
    h = jnp.moveaxis(h, 0, 1).reshape(B, L, H, dv)
    h = _rmsnorm(h, g_out).reshape(B, L, D_MODEL).astype(u.dtype)
    return (o_gate * h) @ w_out, (c1, n1, m1)


def _pool_mixer(u, prefix, start_pos, w_pool, scale):
    B, L, _ = u.shape
    P = POOL_STATE
    ext = jnp.concatenate([prefix.astype(u.dtype), u], axis=1)
    ef = ext.astype(jnp.float32)
    cs = jnp.concatenate([jnp.zeros((B, 1, D_MODEL), jnp.float32), jnp.cumsum(ef, axis=1)], axis=1)
    pos = start_pos + jnp.arange(L)
    means = []
    for g, w in enumerate(POOL_WINDOWS):
        cols = slice(g * POOL_GW, (g + 1) * POOL_GW)
        wsum = cs[:, P + 1:P + 1 + L, cols] - cs[:, P + 1 - w:P + 1 - w + L, cols]
        cnt = jnp.minimum(pos + 1, w).astype(jnp.float32)
        means.append(wsum / cnt[None, :, None])
    pooled = (jnp.concatenate(means, axis=-1) - ef[:, P:]).astype(u.dtype)
    mixed = jnp.einsum('blgc,gce->blge', pooled.reshape(B, L, POOL_GROUPS, POOL_GW), w_pool)
    return mixed.reshape(B, L, D_MODEL) * scale, ext[:, L:]


def _sb_qkv(u, w_qkv, g_q, g_k):
    B, L, _ = u.shape
    qkv = (u @ w_qkv).reshape(B, L, 3, SB_HEADS, SB_DH)
    return _rmsnorm(qkv[:, :, 0], g_q), _rmsnorm(qkv[:, :, 1], g_k), qkv[:, :, 2]


def _sb_block(q, qpos, k, v, b_logit):
    z = jnp.einsum('bthd,bshd->bhts', q, k) * (SB_DH ** -0.5) + b_logit[None, :, None, None]
    valid = (jnp.arange(k.shape[1])[None, :] < qpos[:, None])[None, None]
    log_beta = jax.nn.log_sigmoid(z)
    log_rest = jnp.where(valid, jax.nn.log_sigmoid(-z), 0.0)
    between = lax.cumsum(log_rest, axis=3, reverse=True) - log_rest
    a = jnp.where(valid, jnp.exp(log_beta + between), 0.0)
    return jnp.einsum('bhts,bshd->bthd', a, v)


def _sb_attend(q, start_pos, k, v, b_logit):
    B, L, H, d = q.shape
    f32 = jnp.float32
    qf, kf, vf, bf = q.astype(f32), k.astype(f32), v.astype(f32), b_logit.astype(f32)
    blk = SB_QBLOCK if L % SB_QBLOCK == 0 else L
    nb = L // blk
    qb = jnp.moveaxis(qf.reshape(B, nb, blk, H, d), 1, 0)
    pos = (start_pos + jnp.arange(L)).reshape(nb, blk)
    out = lax.map(lambda a: _sb_block(a[0], a[1], kf, vf, bf), (qb, pos))
    return jnp.moveaxis(out, 0, 1).reshape(B, L, H, d)


def _conv_mixer(u, prefix, w_glu, b_glu, w_dw, b_dw, g_ln, b_ln, w_pw, b_pw):
    L = u.shape[1]
    a = u @ w_glu + b_glu
    glu = a[..., :D_MODEL] * jax.nn.sigmoid(a[..., D_MODEL:])
    ext = jnp.concatenate([prefix.astype(glu.dtype), glu], axis=1)
    c = lax.conv_general_dilated(ext, w_dw[:, None, :].astype(ext.dtype), window_strides=(1,), padding='VALID',
                                 dimension_numbers=('NWC', 'WIO', 'NWC'), feature_group_count=D_MODEL) + b_dw
    c = jax.nn.silu(_layernorm(c, g_ln, b_ln))
    return c @ w_pw + b_pw, ext[:, L:]


def _swiglu(u, w_gu, w_down):
    a = u @ w_gu
    return (jax.nn.silu(a[..., :D_FF]) * a[..., D_FF:]) @ w_down


def _ple(h, p_i, w_ple, g_norm, w_gate):
    gate = jax.nn.sigmoid(_rmsnorm(h, g_norm) @ w_gate)
    return gate * (p_i.astype(h.dtype) @ w_ple)


def _trunk(x, p, start_pos, ml_state, pool_prefix, sb_past, conv_prefix, W):
    B, L, _ = x.shape
    h = x
    ml_new, pool_new, k_new, v_new, conv_new = [], [], [], [], []
    for i in range(DEPTH):
        kind, j = i % N_MIXERS, i // N_MIXERS
        u = _rmsnorm(h, W['norm_mix'][i])
        if kind == 0:
            y, st = _mlstm_mixer(u, ml_state[0][j], ml_state[1][j], ml_state[2][j], W['ml_w_in'][j],
                                 W['ml_b_gate'][j], W['ml_g_out'][j], W['ml_w_out'][j])
            ml_new.append(st)
        elif kind == 1:
            y, st = _pool_mixer(u, pool_prefix[j], start_pos, W['pl_w'][j], W['pl_scale'][j])
            pool_new.append(st)
        elif kind == 2:
            q, k, v = _sb_qkv(u, W['sb_w_qkv'][j], W['sb_g_q'][j], W['sb_g_k'][j])
            if sb_past is None:
                kk, vv = k, v
            else:
                kk = jnp.concatenate([sb_past[0][j].astype(k.dtype), k], axis=1)
                vv = jnp.concatenate([sb_past[1][j].astype(v.dtype), v], axis=1)
            att = _sb_attend(q, start_pos, kk, vv, W['sb_b_logit'][j]).astype(u.dtype)
            y = att.reshape(B, L, D_MODEL) @ W['sb_w_out'][j]
            k_new.append(k)
            v_new.append(v)
        else:
            y, st = _conv_mixer(u, conv_prefix[j], W['cv_w_glu'][j], W['cv_b_glu'][j], W['cv_w_dw'][j], W['cv_b_dw'][j],
                                W['cv_g_ln'][j], W['cv_b_ln'][j], W['cv_w_pw'][j], W['cv_b_pw'][j])
            conv_new.append(st)
        h = h + y
        h = h + _swiglu(_rmsnorm(h, W['norm_ffn'][i]), W['w_ffn_gu'][i], W['w_ffn_down'][i])
        h = h + _ple(h, p[i], W['w_ple'][i], W['norm_ple'][i], W['w_ple_gate'][i])
    new_state = (jnp.stack([s[0] for s in ml_new]), jnp.stack([s[1] for s in ml_new]), jnp.stack([s[2] for s in ml_new]),
                 jnp.stack(pool_new), jnp.stack(k_new), jnp.stack(v_new), jnp.stack(conv_new))
    return h, new_state


def setup_inputs(seed: int = 0) -> dict:
    key = jax.random.key(seed)
    ks = iter(jax.random.split(key, 64))
    D = D_MODEL

    def nrm(shape, scale=1.0):
        return jax.random.normal(next(ks), shape, jnp.float32) * scale

    def gain(shape):
        return 1.0 + nrm(shape, 0.02)

    n_pages = PAST_LEN // PAGE_SIZE
    n_used = DEC_BATCH * n_pages
    n_pool = n_used + (n_used + 3) // 4
    page_table = jax.random.permutation(next(ks), n_pool)[:n_used].reshape(DEC_BATCH, n_pages).astype(jnp.int32)
    return {
        'x_prompt': nrm((BATCH, SEQ, D)),
        'x_sample': nrm((DEC_BATCH, DEC_SEQ, D)),
        'state_mlstm_C': nrm((N_ML_LAYERS, DEC_BATCH, ML_HEADS, ML_DQK, ML_DV), 0.1),
        'state_mlstm_n': nrm((N_ML_LAYERS, DEC_BATCH, ML_HEADS, ML_DQK), 0.5),
        'state_mlstm_m': nrm((N_ML_LAYERS, DEC_BATCH, ML_HEADS), 1.0),
        'state_pool': nrm((N_PL_LAYERS, DEC_BATCH, POOL_STATE, D)),
        'cache_sb_k': nrm((N_SB_LAYERS, n_pool, PAGE_SIZE, SB_HEADS, SB_DH)),
        'cache_sb_v': nrm((N_SB_LAYERS, n_pool, PAGE_SIZE, SB_HEADS, SB_DH)),
        'state_conv': nrm((N_CV_LAYERS, DEC_BATCH, CONV_STATE, D), 0.5),
        'page_table': page_table,
        'p_prompt': nrm((DEPTH, BATCH, SEQ, PLE_DIM)),
        'p_sample': nrm((DEPTH, DEC_BATCH, DEC_SEQ, PLE_DIM)),
        'norm_mix': gain((DEPTH, D)),
        'norm_ffn': gain((DEPTH, D)),
        'w_ffn_gu': nrm((DEPTH, D, 2 * D_FF), D ** -0.5),
        'w_ffn_down': nrm((DEPTH, D_FF, D), D_FF ** -0.5),
        'w_ple': nrm((DEPTH, PLE_DIM, D), PLE_DIM ** -0.5),
        'norm_ple': gain((DEPTH, D)),
        'w_ple_gate': nrm((DEPTH, D, D), D ** -0.5),
        'ml_w_in': nrm((N_ML_LAYERS, D, ML_IN), D ** -0.5),
        'ml_b_gate': jnp.concatenate([nrm((N_ML_LAYERS, ML_HEADS), 0.1), 3.0 + nrm((N_ML_LAYERS, ML_HEADS), 0.5)], axis=-1),
        'ml_g_out': gain((N_ML_LAYERS, ML_HEADS, ML_DV)),
        'ml_w_out': nrm((N_ML_LAYERS, D, D), D ** -0.5),
        'pl_w': nrm((N_PL_LAYERS, POOL_GROUPS, POOL_GW, POOL_GW), POOL_GW ** -0.5),
        'pl_scale': 0.5 + nrm((N_PL_LAYERS, D), 0.05),
        'sb_w_qkv': nrm((N_SB_LAYERS, D, 3 * D), D ** -0.5),
        'sb_g_q': gain((N_SB_LAYERS, SB_DH)),
        'sb_g_k': gain((N_SB_LAYERS, SB_DH)),
        'sb_b_logit': SB_LOGIT_BIAS + nrm((N_SB_LAYERS, SB_HEADS), 0.3),
        'sb_w_out': nrm((N_SB_LAYERS, D, D), D ** -0.5),
        'cv_w_glu': nrm((N_CV_LAYERS, D, 2 * D), D ** -0.5),
        'cv_b_glu': nrm((N_CV_LAYERS, 2 * D), 0.02),
        'cv_w_dw': nrm((N_CV_LAYERS, CONV_WIDTH, D), CONV_WIDTH ** -0.5),
        'cv_b_dw': nrm((N_CV_LAYERS, D), 0.02),
        'cv_g_ln': gain((N_CV_LAYERS, D)),
        'cv_b_ln': nrm((N_CV_LAYERS, D), 0.02),
        'cv_w_pw': nrm((N_CV_LAYERS, D, D), D ** -0.5),
        'cv_b_pw': nrm((N_CV_LAYERS, D), 0.02),
    }


def reference(x_prompt, x_sample, state_mlstm_C, state_mlstm_n, state_mlstm_m, state_pool, cache_sb_k, cache_sb_v,
              state_conv, page_table, p_prompt, p_sample, norm_mix, norm_ffn, w_ffn_gu, w_ffn_down, w_ple, norm_ple,
              w_ple_gate, ml_w_in, ml_b_gate, ml_g_out, ml_w_out, pl_w, pl_scale, sb_w_qkv, sb_g_q, sb_g_k, sb_b_logit,
              sb_w_out, cv_w_glu, cv_b_glu, cv_w_dw, cv_b_dw, cv_g_ln, cv_b_ln, cv_w_pw, cv_b_pw):
    W = dict(norm_mix=norm_mix, norm_ffn=norm_ffn, w_ffn_gu=w_ffn_gu, w_ffn_down=w_ffn_down, w_ple=w_ple,
             norm_ple=norm_ple, w_ple_gate=w_ple_gate, ml_w_in=ml_w_in, ml_b_gate=ml_b_gate, ml_g_out=ml_g_out,
             ml_w_out=ml_w_out, pl_w=pl_w, pl_scale=pl_scale, sb_w_qkv=sb_w_qkv, sb_g_q=sb_g_q, sb_g_k=sb_g_k,
             sb_b_logit=sb_b_logit, sb_w_out=sb_w_out, cv_w_glu=cv_w_glu, cv_b_glu=cv_b_glu, cv_w_dw=cv_w_dw,
             cv_b_dw=cv_b_dw, cv_g_ln=cv_g_ln, cv_b_ln=cv_b_ln, cv_w_pw=cv_w_pw, cv_b_pw=cv_b_pw)
    bp = x_prompt.shape[0]
    f32 = jnp.float32
    ml0 = (jnp.zeros((N_ML_LAYERS, bp, ML_HEADS, ML_DQK, ML_DV), f32),
           jnp.zeros((N_ML_LAYERS, bp, ML_HEADS, ML_DQK), f32),
           jnp.zeros((N_ML_LAYERS, bp, ML_HEADS), f32))
    pool0 = jnp.zeros((N_PL_LAYERS, bp, POOL_STATE, D_MODEL), x_prompt.dtype)
    conv0 = jnp.zeros((N_CV_LAYERS, bp, CONV_STATE, D_MODEL), x_prompt.dtype)
    y_prompt, (pC, pn, pm, ppool, pk, pv, pconv) = _trunk(x_prompt, p_prompt, 0, ml0, pool0, None, conv0, W)
    bd = x_sample.shape[0]
    n_past = page_table.shape[1] * PAGE_SIZE
    k_past = [cache_sb_k[j][page_table].reshape(bd, n_past, SB_HEADS, SB_DH) for j in range(N_SB_LAYERS)]
    v_past = [cache_sb_v[j][page_table].reshape(bd, n_past, SB_HEADS, SB_DH) for j in range(N_SB_LAYERS)]
    y_sample, (sC, sn, sm, spool, sk, sv, sconv) = _trunk(
        x_sample, p_sample, n_past, (state_mlstm_C, state_mlstm_n, state_mlstm_m), state_pool,
        (k_past, v_past), state_conv, W)
    return (y_prompt, y_sample, pC, pn, pm, ppool, pk, pv, pconv, sC, sn, sm, spool, sk, sv, sconv)
```

```python
import functools

import jax
import jax.numpy as jnp
from jax import lax
from jax.experimental import pallas as pl
from jax.experimental.pallas import tpu as pltpu

F32 = jnp.float32
BF16 = jnp.bfloat16
HIGHEST = lax.Precision.HIGHEST

EPS = 1e-6
D = 1024
PLE_DIM = 256
D_FF = 2816
ML_HEADS, ML_DQK, ML_DV = 8, 64, 128
POOL_WINDOWS = (2, 4, 8, 16)
POOL_GW = D // len(POOL_WINDOWS)
POOL_STATE = 15
SB_HEADS, SB_DH = 16, 64
CONV_WIDTH = 31
CONV_STATE = CONV_WIDTH - 1
PAGE_SIZE = 128

V7X_VMEM_LIMIT = 56 * 1024 * 1024
LANES = 128
SUBLANES = 8

FF_CHUNK = 256
ML_CHUNK = 256
SB_BLOCK = 256
POOL_HALO = 16
CONV_HALO = 32


def _row_tile(n):
    return 512 if n % 512 == 0 else n


def _params(*sem):
    return pltpu.CompilerParams(dimension_semantics=sem, vmem_limit_bytes=V7X_VMEM_LIMIT)


def _full(shape):
    return pl.BlockSpec(shape, lambda *_: (0,) * len(shape), pipeline_mode=pl.Buffered(1))


def _rows(tm, width):
    return pl.BlockSpec((tm, width), lambda i: (i, 0))


def _rms(x, g):
    return x * lax.rsqrt(jnp.mean(x * x, axis=-1, keepdims=True) + EPS) * g


def _bdot(a, b):
    return jnp.dot(a.astype(BF16), b.astype(BF16), preferred_element_type=F32)


def _bdot_nt(a, b):
    return lax.dot_general(a.astype(BF16), b.astype(BF16), (((1,), (1,)), ((), ())),
                           preferred_element_type=F32)


def _sigmoid(x):
    return 1.0 / (1.0 + jnp.exp(-x))


def _log_sigmoid(x):
    return jnp.minimum(x, 0.0) - jnp.log(1.0 + jnp.exp(-jnp.abs(x)))


def _ffn_ple_kernel(h_ref, p_ref, gf_ref, wgu_ref, wd_ref, gp_ref, wgate_ref, wple_ref, o_ref):
    h = h_ref[...]
    u = _rms(h, gf_ref[...]).astype(BF16)
    acc = jnp.zeros(h.shape, F32)
    for c in range(D_FF // FF_CHUNK):
        lo = c * FF_CHUNK
        gate = jnp.dot(u, wgu_ref[:, lo:lo + FF_CHUNK], preferred_element_type=F32)
        up = jnp.dot(u, wgu_ref[:, D_FF + lo:D_FF + lo + FF_CHUNK], preferred_element_type=F32)
        act = (gate * _sigmoid(gate) * up).astype(BF16)
        acc = acc + jnp.dot(act, wd_ref[lo:lo + FF_CHUNK, :], preferred_element_type=F32)
    h1 = h + acc
    u2 = _rms(h1, gp_ref[...])
    gate = _sigmoid(_bdot(u2, wgate_ref[...]))
    o_ref[...] = h1 + gate * _bdot(p_ref[...], wple_ref[...])


def _ffn_ple(h, p, g_ffn, w_gu, w_down, g_ple, w_gate, w_ple):
    n = h.shape[0]
    tm = _row_tile(n)
    return pl.pallas_call(
        _ffn_ple_kernel,
        out_shape=jax.ShapeDtypeStruct((n, D), F32),
        grid=(n // tm,),
        in_specs=[_rows(tm, D), _rows(tm, PLE_DIM), _full((1, D)), _full((D, 2 * D_FF)), _full((D_FF, D)),
                  _full((1, D)), _full((D, D)), _full((PLE_DIM, D))],
        out_specs=_rows(tm, D),
        compiler_params=_params("parallel"),
        name="ffn_ple",
    )(h, p, g_ffn, w_gu, w_down, g_ple, w_gate, w_ple)


def _out_proj_kernel(h_ref, y_ref, w_ref, o_ref):
    o_ref[...] = h_ref[...] + _bdot(y_ref[...], w_ref[...])


def _out_proj(h, y, w):
    n = h.shape[0]
    tm = _row_tile(n)
    return pl.pallas_call(
        _out_proj_kernel,
        out_shape=jax.ShapeDtypeStruct((n, D), F32),
        grid=(n // tm,),
        in_specs=[_rows(tm, D), _rows(tm, D), _full((D, D))],
        out_specs=_rows(tm, D),
        compiler_params=_params("parallel"),
        name="out_proj",
    )(h, y, w)


def _ml_in_kernel(h_ref, g_ref, wq_ref, wk_ref, wv_ref, wo_ref, wg_ref, bg_ref,
                  q_ref, k_ref, v_ref, og_ref, gates_ref):
    uf = _rms(h_ref[...], g_ref[...])
    u = uf.astype(BF16)
    q_ref[...] = jnp.dot(u, wq_ref[...], preferred_element_type=F32)
    k_ref[...] = jnp.dot(u, wk_ref[...], preferred_element_type=F32) * (ML_DQK ** -0.5)
    v_ref[...] = jnp.dot(u, wv_ref[...], preferred_element_type=F32)
    og_ref[...] = _sigmoid(jnp.dot(u, wo_ref[...], preferred_element_type=F32))
    gates = jnp.dot(uf, wg_ref[...], preferred_element_type=F32, precision=HIGHEST) + bg_ref[...]
    is_forget = lax.broadcasted_iota(jnp.int32, gates.shape, 1) >= ML_HEADS
    gates_ref[...] = jnp.where(is_forget, _log_sigmoid(gates), gates)


def _ml_in(h, g, wq, wk, wv, wo, wg, bg):
    n = h.shape[0]
    tm = _row_tile(n)
    hq = ML_HEADS * ML_DQK
    return pl.pallas_call(
        _ml_in_kernel,
        out_shape=[jax.ShapeDtypeStruct((n, hq), F32), jax.ShapeDtypeStruct((n, hq), F32),
                   jax.ShapeDtypeStruct((n, D), F32), jax.ShapeDtypeStruct((n, D), F32),
                   jax.ShapeDtypeStruct((n, 2 * ML_HEADS), F32)],
        grid=(n // tm,),
        in_specs=[_rows(tm, D), _full((1, D)), _full((D, hq)), _full((D, hq)), _full((D, D)), _full((D, D)),
                  _full((D, 2 * ML_HEADS)), _full((1, 2 * ML_HEADS))],
        out_specs=[_rows(tm, hq), _rows(tm, hq), _rows(tm, D), _rows(tm, D), _rows(tm, 2 * ML_HEADS)],
        compiler_params=_params("parallel"),
        name="ml_in",
    )(h, g, wq, wk, wv, wo, wg, bg)


def _head_norm_gate(hh, g, og):
    return og * (hh * lax.rsqrt(jnp.mean(hh * hh, axis=-1, keepdims=True) + EPS) * g)


def _ml_chunk_kernel(q_ref, k_ref, v_ref, og_ref, gc_ref, gr_ref, gout_ref, y_ref, c_ref, n_ref, m_ref):
    @pl.when(pl.program_id(1) == 0)
    def _():
        c_ref[...] = jnp.zeros(c_ref.shape, F32)
        n_ref[...] = jnp.zeros(n_ref.shape, F32)
        m_ref[...] = jnp.zeros(m_ref.shape, F32)

    lc = q_ref.shape[1]
    row = lax.broadcasted_iota(jnp.int32, (lc, lc), 0)
    col = lax.broadcasted_iota(jnp.int32, (lc, lc), 1)
    causal = col <= row
    gates_c = gc_ref[0]
    gates_r = gr_ref[0]
    b_c = jnp.dot(causal.astype(F32), gates_c[:, ML_HEADS:], preferred_element_type=F32, precision=HIGHEST)
    b_r = jnp.dot(gates_r[ML_HEADS:, :], (row <= col).astype(F32), preferred_element_type=F32,
                  precision=HIGHEST)
    for hd in range(ML_HEADS):
        qh = q_ref[0, :, hd * ML_DQK:(hd + 1) * ML_DQK]
        kh = k_ref[0, :, hd * ML_DQK:(hd + 1) * ML_DQK]
        vh = v_ref[0, :, hd * ML_DV:(hd + 1) * ML_DV]
        bc = b_c[:, hd:hd + 1]
        br = b_r[hd:hd + 1, :]
        li_c = gates_c[:, hd:hd + 1]
        li_r = gates_r[hd:hd + 1, :]
        m_prev = m_ref[0, hd:hd + 1, 0:1]
        c_prev = c_ref[0, hd]
        n_prev = n_ref[0, hd:hd + 1, :]

        dlog = jnp.where(causal, bc - br + li_r, -jnp.inf)
        inter = bc + m_prev
        m_t = jnp.maximum(inter, jnp.max(dlog, axis=1, keepdims=True))
        w_intra = jnp.exp(dlog - m_t)
        w_inter = jnp.exp(inter - m_t)
        s_mat = w_intra * _bdot_nt(qh, kh)
        num = w_inter * _bdot(qh, c_prev) + _bdot(s_mat, vh)
        den = w_inter * jnp.sum(qh * n_prev, axis=1, keepdims=True) + jnp.sum(s_mat, axis=1, keepdims=True)
        hh = num / jnp.maximum(jnp.abs(den), jnp.exp(-m_t))
        y_ref[0, :, hd * ML_DV:(hd + 1) * ML_DV] = _head_norm_gate(
            hh, gout_ref[:, hd * ML_DV:(hd + 1) * ML_DV], og_ref[0, :, hd * ML_DV:(hd + 1) * ML_DV])

        g_end = br[:, lc - 1:lc]
        dl_end = g_end - bc + li_c
        m_new = jnp.maximum(g_end + m_prev, jnp.max(dl_end, axis=0, keepdims=True))
        a_prev = jnp.exp(g_end + m_prev - m_new)
        kw = jnp.exp(dl_end - m_new) * kh
        c_ref[0, hd] = a_prev * c_prev + lax.dot_general(
            kw.astype(BF16), vh.astype(BF16), (((0,), (0,)), ((), ())), preferred_element_type=F32)
        n_ref[0, hd:hd + 1, :] = a_prev * n_prev + jnp.sum(kw, axis=0, keepdims=True)
        m_ref[0, hd:hd + 1, :] = jnp.broadcast_to(m_new, (1, LANES))


def _ml_chunk(q, k, v, og, gates, g_out):
    b, l, _ = q.shape
    lc = ML_CHUNK if l % ML_CHUNK == 0 else l
    gates_t = jnp.swapaxes(gates, 1, 2)
    hq = ML_HEADS * ML_DQK
    blk = lambda w: pl.BlockSpec((1, lc, w), lambda i, j: (i, j, 0))
    return pl.pallas_call(
        _ml_chunk_kernel,
        out_shape=[jax.ShapeDtypeStruct((b, l, D), F32),
                   jax.ShapeDtypeStruct((b, ML_HEADS, ML_DQK, ML_DV), F32),
                   jax.ShapeDtypeStruct((b, ML_HEADS, ML_DQK), F32),
                   jax.ShapeDtypeStruct((b, ML_HEADS, LANES), F32)],
        grid=(b, l // lc),
        in_specs=[blk(hq), blk(hq), blk(D), blk(D), blk(2 * ML_HEADS),
                  pl.BlockSpec((1, 2 * ML_HEADS, lc), lambda i, j: (i, 0, j)),
                  pl.BlockSpec((1, D), lambda i, j: (0, 0))],
        out_specs=[blk(D),
                   pl.BlockSpec((1, ML_HEADS, ML_DQK, ML_DV), lambda i, j: (i, 0, 0, 0)),
                   pl.BlockSpec((1, ML_HEADS, ML_DQK), lambda i, j: (i, 0, 0)),
                   pl.BlockSpec((1, ML_HEADS, LANES), lambda i, j: (i, 0, 0))],
        compiler_params=_params("parallel", "arbitrary"),
        name="ml_chunk",
    )(q, k, v, og, gates, gates_t, g_out)


def _ml_step_kernel(qt_ref, kt_ref, v_ref, og_ref, gates_ref, c_ref, nt_ref, m_ref, gout_ref,
                    y_ref, c_out, nt_out, m_out):
    lane = lax.broadcasted_iota(jnp.int32, (1, ML_HEADS), 1)
    m_row = jnp.zeros((1, ML_HEADS), F32)
    for hd in range(ML_HEADS):
        qc = qt_ref[0, :, hd:hd + 1]
        kc = kt_ref[0, :, hd:hd + 1]
        nc = nt_ref[0, :, hd:hd + 1]
        vh = v_ref[0, :, hd * ML_DV:(hd + 1) * ML_DV]
        c_prev = c_ref[0, hd]
        li = gates_ref[0, :, hd:hd + 1]
        lf = gates_ref[0, :, ML_HEADS + hd:ML_HEADS + hd + 1]
        m_prev = m_ref[0, :, hd:hd + 1]
        inter = lf + m_prev
        m_t = jnp.maximum(inter, li)
        w_intra = jnp.exp(li - m_t)
        w_inter = jnp.exp(inter - m_t)
        s = w_intra * jnp.sum(qc * kc, axis=0, keepdims=True)
        num = w_inter * jnp.sum(qc * c_prev, axis=0, keepdims=True) + s * vh
        den = w_inter * jnp.sum(qc * nc, axis=0, keepdims=True) + s
        hh = num / jnp.maximum(jnp.abs(den), jnp.exp(-m_t))
        y_ref[0, :, hd * ML_DV:(hd + 1) * ML_DV] = _head_norm_gate(
            hh, gout_ref[:, hd * ML_DV:(hd + 1) * ML_DV], og_ref[0, :, hd * ML_DV:(hd + 1) * ML_DV])
        c_out[0, hd] = w_inter * c_prev + w_intra * (kc * vh)
        nt_out[0, :, hd:hd + 1] = w_inter * nc + w_intra * kc
        m_row = jnp.where(lane == hd, m_t, m_row)
    m_out[0] = m_row


def _ml_step(q, k, v, og, gates, c0, n0, m0, g_out):
    b = q.shape[0]
    to_t = lambda a: jnp.swapaxes(a.reshape(b, ML_HEADS, ML_DQK), 1, 2)
    t_spec = pl.BlockSpec((1, ML_DQK, ML_HEADS), lambda i: (i, 0, 0))
    row = lambda w: pl.BlockSpec((1, 1, w), lambda i: (i, 0, 0))
    c_spec = pl.BlockSpec((1, ML_HEADS, ML_DQK, ML_DV), lambda i: (i, 0, 0, 0))
    y, c1, nt1, m1 = pl.pallas_call(
        _ml_step_kernel,
        out_shape=[jax.ShapeDtypeStruct((b, 1, D), F32),
                   jax.ShapeDtypeStruct((b, ML_HEADS, ML_DQK, ML_DV), F32),
                   jax.ShapeDtypeStruct((b, ML_DQK, ML_HEADS), F32),
                   jax.ShapeDtypeStruct((b, 1, ML_HEADS), F32)],
        grid=(b,),
        in_specs=[t_spec, t_spec, row(D), row(D), row(2 * ML_HEADS), c_spec, t_spec, row(ML_HEADS),
                  pl.BlockSpec((1, D), lambda i: (0, 0))],
        out_specs=[row(D), c_spec, t_spec, row(ML_HEADS)],
        compiler_params=_params("parallel"),
        name="ml_step",
    )(to_t(q), to_t(k), v.reshape(b, 1, D), og.reshape(b, 1, D), gates.reshape(b, 1, 2 * ML_HEADS),
      c0, jnp.swapaxes(n0, 1, 2), m0.reshape(b, 1, ML_HEADS), g_out)
    return y.reshape(b, D), c1, jnp.swapaxes(nt1, 1, 2), m1.reshape(b, ML_HEADS)


def _pool_project(h, pooled, w_ref, scale_ref, o_ref):
    for g in range(len(POOL_WINDOWS)):
        cols = slice(g * POOL_GW, (g + 1) * POOL_GW)
        o_ref[:, cols] = h[:, cols] + _bdot(pooled[g], w_ref[g]) * scale_ref[:, cols]


def _pool_kernel(h_ref, g_ref, w_ref, scale_ref, o_ref, tail_ref, ext_ref):
    tm = h_ref.shape[1]
    step = pl.program_id(1)

    @pl.when(step == 0)
    def _():
        ext_ref[0:POOL_HALO, :] = jnp.zeros((POOL_HALO, D), F32)

    @pl.when(step > 0)
    def _():
        ext_ref[0:POOL_HALO, :] = ext_ref[tm:tm + POOL_HALO, :]

    h = h_ref[0]
    u = _rms(h, g_ref[...])
    ext_ref[POOL_HALO:POOL_HALO + tm, :] = u
    tail_ref[0] = u[tm - POOL_HALO:, :]
    pos = step * tm + lax.broadcasted_iota(jnp.int32, (tm, 1), 0)
    pooled = []
    for g, w in enumerate(POOL_WINDOWS):
        cols = slice(g * POOL_GW, (g + 1) * POOL_GW)
        wsum = u[:, cols]
        for back in range(1, w):
            wsum = wsum + ext_ref[POOL_HALO - back:POOL_HALO - back + tm, cols]
        cnt = jnp.minimum(pos + 1, w).astype(F32)
        pooled.append(wsum / cnt - u[:, cols])
    _pool_project(h, pooled, w_ref, scale_ref, o_ref.at[0])


def _pool_prompt(h, g, w, scale):
    b, l, _ = h.shape
    tm = _row_tile(l)
    return pl.pallas_call(
        _pool_kernel,
        out_shape=[jax.ShapeDtypeStruct((b, l, D), F32), jax.ShapeDtypeStruct((b, POOL_HALO, D), F32)],
        grid=(b, l // tm),
        in_specs=[pl.BlockSpec((1, tm, D), lambda i, j: (i, j, 0)),
                  pl.BlockSpec((1, D), lambda i, j: (0, 0)),
                  pl.BlockSpec((len(POOL_WINDOWS), POOL_GW, POOL_GW), lambda i, j: (0, 0, 0)),
                  pl.BlockSpec((1, D), lambda i, j: (0, 0))],
        out_specs=[pl.BlockSpec((1, tm, D), lambda i, j: (i, j, 0)),
                   pl.BlockSpec((1, POOL_HALO, D), lambda i, j: (i, 0, 0))],
        scratch_shapes=[pltpu.VMEM((POOL_HALO + tm, D), F32)],
        compiler_params=_params("parallel", "arbitrary"),
        name="pool",
    )(h, g, w, scale)


def _pool_step_kernel(h_ref, pref_ref, g_ref, w_ref, scale_ref, o_ref, state_ref, *, start_pos):
    h = h_ref[...]
    u = _rms(h, g_ref[...])
    pooled = []
    for g, w in enumerate(POOL_WINDOWS):
        cols = slice(g * POOL_GW, (g + 1) * POOL_GW)
        wsum = u[:, cols]
        for back in range(1, w):
            wsum = wsum + pref_ref[POOL_STATE - back, :, cols]
        pooled.append(wsum / float(min(start_pos + 1, w)) - u[:, cols])
    _pool_project(h, pooled, w_ref, scale_ref, o_ref)
    state_ref[0:POOL_STATE - 1] = pref_ref[1:POOL_STATE]
    state_ref[POOL_STATE - 1] = u


def _pool_step(h, prefix, g, w, scale, start_pos):
    b = h.shape[0]
    out, state_t = pl.pallas_call(
        functools.partial(_pool_step_kernel, start_pos=start_pos),
        out_shape=[jax.ShapeDtypeStruct((b, D), F32), jax.ShapeDtypeStruct((POOL_STATE, b, D), F32)],
        compiler_params=pltpu.CompilerParams(vmem_limit_bytes=V7X_VMEM_LIMIT),
        name="pool_step",
    )(h, jnp.swapaxes(prefix, 0, 1), g, w, scale)
    return out, jnp.swapaxes(state_t, 0, 1)


def _pair_norm(x, g):
    first = lax.broadcasted_iota(jnp.int32, x.shape, 1) < SB_DH
    sq = x * x
    s_first = jnp.sum(jnp.where(first, sq, 0.0), axis=1, keepdims=True)
    s_second = jnp.sum(jnp.where(first, 0.0, sq), axis=1, keepdims=True)
    inv = jnp.where(first, lax.rsqrt(s_first * (1.0 / SB_DH) + EPS), lax.rsqrt(s_second * (1.0 / SB_DH) + EPS))
    return x * inv * g


def _sb_qkv_kernel(h_ref, g_ref, wq_ref, wk_ref, wv_ref, gq_ref, gk_ref, q_ref, k_ref, v_ref):
    u = _rms(h_ref[...], g_ref[...]).astype(BF16)
    v_ref[...] = jnp.dot(u, wv_ref[...], preferred_element_type=F32)
    for w_ref, gain_ref, o_ref in ((wq_ref, gq_ref, q_ref), (wk_ref, gk_ref, k_ref)):
        x = jnp.dot(u, w_ref[...], preferred_element_type=F32)
        for p in range(D // LANES):
            cols = slice(p * LANES, (p + 1) * LANES)
            o_ref[:, cols] = _pair_norm(x[:, cols], gain_ref[...])


def _sb_qkv(h, g, wq, wk, wv, gq, gk):
    n = h.shape[0]
    tm = _row_tile(n)
    return pl.pallas_call(
        _sb_qkv_kernel,
        out_shape=[jax.ShapeDtypeStruct((n, D), F32)] * 3,
        grid=(n // tm,),
        in_specs=[_rows(tm, D), _full((1, D)), _full((D, D)), _full((D, D)), _full((D, D)),
                  _full((1, LANES)), _full((1, LANES))],
        out_specs=[_rows(tm, D)] * 3,
        compiler_params=_params("parallel"),
        name="sb_qkv",
    )(h, g, wq, wk, wv, gq, gk)


def _sb_weights(z, valid):
    softplus = jnp.maximum(z, 0.0) + jnp.log(1.0 + jnp.exp(-jnp.abs(z)))
    log_rest = -softplus
    if valid is not None:
        log_rest = jnp.where(valid, log_rest, 0.0)
    return z - softplus, log_rest


def _sb_block(qh, kb, vb, bias, tri, carry, valid):
    z = _bdot_nt(qh, kb) + bias
    log_beta, log_rest = _sb_weights(z, valid)
    hi = log_rest.astype(BF16)
    lo = (log_rest - hi.astype(F32)).astype(BF16)
    inner = jnp.dot(hi, tri, preferred_element_type=F32) + jnp.dot(lo, tri, preferred_element_type=F32)
    a = jnp.exp(log_beta + inner + carry)
    if valid is not None:
        a = jnp.where(valid, a, 0.0)
    out = _bdot(a, vb)
    return out, carry + inner[:, 0:1] + log_rest[:, 0:1]


def _sb_attn_kernel(bias_ref, q_ref, k_ref, v_ref, tri_ref, o_ref):
    tq = q_ref.shape[1]
    pair = pl.program_id(1)
    qi = pl.program_id(2)
    row = lax.broadcasted_iota(jnp.int32, (tq, tq), 0)
    col = lax.broadcasted_iota(jnp.int32, (tq, tq), 1)
    strict = col < row
    tri = tri_ref[...]
    for hh in range(2):
        cols = slice(hh * SB_DH, (hh + 1) * SB_DH)
        bias = bias_ref[2 * pair + hh]
        qh = (q_ref[0, :, cols] * (SB_DH ** -0.5)).astype(BF16)
        start = pl.multiple_of(qi * tq, tq)
        acc, carry = _sb_block(qh, k_ref[0, pl.ds(start, tq), cols], v_ref[0, pl.ds(start, tq), cols],
                               bias, tri, jnp.zeros((tq, 1), F32), strict)

        def body(jj, state):
            acc, carry = state
            off = pl.multiple_of((qi - 1 - jj) * tq, tq)
            out, carry = _sb_block(qh, k_ref[0, pl.ds(off, tq), cols], v_ref[0, pl.ds(off, tq), cols],
                                   bias, tri, carry, None)
            return acc + out, carry

        acc, _ = lax.fori_loop(0, qi, body, (acc, carry))
        o_ref[0, :, cols] = acc


def _later_key_matrix(n):
    r = jnp.arange(n)
    return (r[:, None] > r[None, :]).astype(BF16)


def _sb_attn_prompt(q, k, v, b_logit):
    b, l, _ = q.shape
    tq = SB_BLOCK if l % SB_BLOCK == 0 else l
    qspec = pl.BlockSpec((1, tq, LANES), lambda i, p, j: (i, j, p))
    kvspec = pl.BlockSpec((1, l, LANES), lambda i, p, j: (i, 0, p))
    return pl.pallas_call(
        _sb_attn_kernel,
        out_shape=jax.ShapeDtypeStruct((b, l, D), F32),
        grid=(b, SB_HEADS // 2, l // tq),
        in_specs=[pl.BlockSpec(memory_space=pltpu.SMEM), qspec, kvspec, kvspec,
                  pl.BlockSpec((tq, tq), lambda i, p, j: (0, 0))],
        out_specs=qspec,
        compiler_params=_params("parallel", "parallel", "arbitrary"),
        name="sb_attn",
    )(b_logit, q, k, v, _later_key_matrix(tq))


def _sb_paged_kernel(pt_ref, q_ref, bias_ref, k_ref, v_ref, tri_ref, o_ref, qbd_ref, acc_ref, carry_ref):
    step = pl.program_id(1)
    head = lax.broadcasted_iota(jnp.int32, (SB_HEADS, D), 0)
    own = lax.broadcasted_iota(jnp.int32, (SB_HEADS, D), 1) // SB_DH == head

    @pl.when(step == 0)
    def _():
        qbd_ref[...] = jnp.where(own, q_ref[0] * (SB_DH ** -0.5), 0.0).astype(BF16)
        acc_ref[...] = jnp.zeros(acc_ref.shape, F32)
        carry_ref[...] = jnp.zeros(carry_ref.shape, F32)

    z = _bdot_nt(qbd_ref[...], k_ref[0]) + bias_ref[...]
    log_beta, log_rest = _sb_weights(z, None)
    inner = jnp.dot(log_rest, tri_ref[...], preferred_element_type=F32, precision=HIGHEST)
    carry = carry_ref[:, 0:1]
    a = jnp.exp(log_beta + inner + carry)
    acc_ref[...] += _bdot(a, v_ref[0])
    carry_ref[...] = jnp.broadcast_to(carry + jnp.sum(log_rest, axis=1, keepdims=True), carry_ref.shape)

    @pl.when(step == pl.num_programs(1) - 1)
    def _():
        o_ref[0] = jnp.sum(jnp.where(own, acc_ref[...], 0.0), axis=0, keepdims=True)


def _sb_attn_paged(q, cache_k, cache_v, page_table, b_logit):
    b = q.shape[0]
    n_pages = page_table.shape[1]
    n_pool = cache_k.shape[0]
    page = lambda i, j, pt: (pt[i * n_pages + n_pages - 1 - j], 0, 0)
    out = pl.pallas_call(
        _sb_paged_kernel,
        out_shape=jax.ShapeDtypeStruct((b, 1, D), F32),
        grid_spec=pltpu.PrefetchScalarGridSpec(
            num_scalar_prefetch=1,
            grid=(b, n_pages),
            in_specs=[pl.BlockSpec((1, 1, D), lambda i, j, pt: (i, 0, 0)),
                      pl.BlockSpec((SB_HEADS, 1), lambda i, j, pt: (0, 0)),
                      pl.BlockSpec((1, PAGE_SIZE, D), page),
                      pl.BlockSpec((1, PAGE_SIZE, D), page),
                      pl.BlockSpec((PAGE_SIZE, PAGE_SIZE), lambda i, j, pt: (0, 0))],
            out_specs=pl.BlockSpec((1, 1, D), lambda i, j, pt: (i, 0, 0)),
            scratch_shapes=[pltpu.VMEM((SB_HEADS, D), BF16), pltpu.VMEM((SB_HEADS, D), F32),
                            pltpu.VMEM((SB_HEADS, LANES), F32)]),
        compiler_params=_params("parallel", "arbitrary"),
        name="sb_paged",
    )(page_table.reshape(-1), q.reshape(b, 1, D), b_logit.reshape(SB_HEADS, 1),
      cache_k.reshape(n_pool, PAGE_SIZE, D), cache_v.reshape(n_pool, PAGE_SIZE, D),
      _later_key_matrix(PAGE_SIZE).astype(F32))
    return out.reshape(b, D)


def _glu(h, g, w_ref, b_ref):
    a = _bdot(_rms(h, g), w_ref[...]) + b_ref[...]
    return a[:, :D] * _sigmoid(a[:, D:])


def _conv_tail(h, c, gln_ref, bln_ref, wpw_ref, bpw_ref):
    cc = c - jnp.mean(c, axis=-1, keepdims=True)
    y = cc * lax.rsqrt(jnp.mean(cc * cc, axis=-1, keepdims=True) + EPS) * gln_ref[...] + bln_ref[...]
    y = y * _sigmoid(y)
    return h + _bdot(y, wpw_ref[...]) + bpw_ref[...]


def _conv_kernel(h_ref, g_ref, wglu_ref, bglu_ref, wdw_ref, bdw_ref, gln_ref, bln_ref, wpw_ref, bpw_ref,
                 o_ref, tail_ref, ext_ref):
    tm = h_ref.shape[1]
    step = pl.program_id(1)

    @pl.when(step == 0)
    def _():
        ext_ref[0:CONV_HALO, :] = jnp.zeros((CONV_HALO, D), F32)

    @pl.when(step > 0)
    def _():
        ext_ref[0:CONV_HALO, :] = ext_ref[tm:tm + CONV_HALO, :]

    h = h_ref[0]
    glu = _glu(h, g_ref[...], wglu_ref, bglu_ref)
    ext_ref[CONV_HALO:CONV_HALO + tm, :] = glu
    tail_ref[0] = glu[tm - CONV_HALO:, :]
    c = jnp.broadcast_to(bdw_ref[...], (tm, D))
    for j in range(CONV_WIDTH):
        lo = CONV_HALO - CONV_STATE + j
        c = c + ext_ref[lo:lo + tm, :] * wdw_ref[j:j + 1, :]
    o_ref[0] = _conv_tail(h, c, gln_ref, bln_ref, wpw_ref, bpw_ref)


def _conv_prompt(h, g, w_glu, b_glu, w_dw, b_dw, g_ln, b_ln, w_pw, b_pw):
    b, l, _ = h.shape
    tm = _row_tile(l)
    const = lambda shape: pl.BlockSpec(shape, lambda i, j: (0,) * len(shape), pipeline_mode=pl.Buffered(1))
    return pl.pallas_call(
        _conv_kernel,
        out_shape=[jax.ShapeDtypeStruct((b, l, D), F32), jax.ShapeDtypeStruct((b, CONV_HALO, D), F32)],
        grid=(b, l // tm),
        in_specs=[pl.BlockSpec((1, tm, D), lambda i, j: (i, j, 0)),
                  const((1, D)), const((D, 2 * D)), const((1, 2 * D)), const((CONV_WIDTH, D)), const((1, D)),
                  const((1, D)), const((1, D)), const((D, D)), const((1, D))],
        out_specs=[pl.BlockSpec((1, tm, D), lambda i, j: (i, j, 0)),
                   pl.BlockSpec((1, CONV_HALO, D), lambda i, j: (i, 0, 0))],
        scratch_shapes=[pltpu.VMEM((CONV_HALO + tm, D), F32)],
        compiler_params=_params("parallel", "arbitrary"),
        name="conv",
    )(h, g, w_glu, b_glu, w_dw, b_dw, g_ln, b_ln, w_pw, b_pw)


def _conv_step_kernel(h_ref, pref_ref, g_ref, wglu_ref, bglu_ref, wdw_ref, bdw_ref, gln_ref, bln_ref,
                      wpw_ref, bpw_ref, o_ref, state_ref):
    h = h_ref[...]
    glu = _glu(h, g_ref[...], wglu_ref, bglu_ref)
    c = bdw_ref[...] + glu * wdw_ref[CONV_STATE:CONV_WIDTH, :]
    for j in range(CONV_STATE):
        c = c + pref_ref[j] * wdw_ref[j:j + 1, :]
    o_ref[...] = _conv_tail(h, c, gln_ref, bln_ref, wpw_ref, bpw_ref)
    state_ref[0:CONV_STATE - 1] = pref_ref[1:CONV_STATE]
    state_ref[CONV_STATE - 1] = glu


def _conv_step(h, prefix, g, w_glu, b_glu, w_dw, b_dw, g_ln, b_ln, w_pw, b_pw):
    b = h.shape[0]
    out, state_t = pl.pallas_call(
        _conv_step_kernel,
        out_shape=[jax.ShapeDtypeStruct((b, D), F32), jax.ShapeDtypeStruct((CONV_STATE, b, D), F32)],
        compiler_params=pltpu.CompilerParams(vmem_limit_bytes=V7X_VMEM_LIMIT),
        name="conv_step",
    )(h, jnp.swapaxes(prefix, 0, 1), g, w_glu, b_glu, w_dw, b_dw, g_ln, b_ln, w_pw, b_pw)
    return out, jnp.swapaxes(state_t, 0, 1)


def _prepare_weights(w):
    hq = ML_HEADS * ML_DQK
    row = lambda a: a.reshape(a.shape[0], 1, -1)
    ml_w_in = w['ml_w_in']
    sb_w = w['sb_w_qkv'].astype(BF16)
    tile_pair = lambda g: jnp.tile(g, (1, 2)).reshape(g.shape[0], 1, LANES)
    return dict(
        norm_mix=row(w['norm_mix']), norm_ffn=row(w['norm_ffn']), norm_ple=row(w['norm_ple']),
        w_ffn_gu=w['w_ffn_gu'].astype(BF16), w_ffn_down=w['w_ffn_down'].astype(BF16),
        w_ple=w['w_ple'].astype(BF16), w_ple_gate=w['w_ple_gate'].astype(BF16),
        ml_wq=ml_w_in[:, :, :hq].astype(BF16), ml_wk=ml_w_in[:, :, hq:2 * hq].astype(BF16),
        ml_wv=ml_w_in[:, :, 2 * hq:2 * hq + D].astype(BF16),
        ml_wo=ml_w_in[:, :, 2 * hq + D:2 * hq + 2 * D].astype(BF16),
        ml_wg=ml_w_in[:, :, 2 * hq + 2 * D:], ml_b_gate=row(w['ml_b_gate']),
        ml_g_out=w['ml_g_out'].reshape(-1, 1, D), ml_w_out=w['ml_w_out'].astype(BF16),
        pl_w=w['pl_w'].astype(BF16), pl_scale=row(w['pl_scale']),
        sb_wq=sb_w[:, :, :D], sb_wk=sb_w[:, :, D:2 * D], sb_wv=sb_w[:, :, 2 * D:],
        sb_g_q=tile_pair(w['sb_g_q']), sb_g_k=tile_pair(w['sb_g_k']), sb_b_logit=w['sb_b_logit'],
        sb_w_out=w['sb_w_out'].astype(BF16),
        cv_w_glu=w['cv_w_glu'].astype(BF16), cv_b_glu=row(w['cv_b_glu']), cv_w_dw=w['cv_w_dw'],
        cv_b_dw=row(w['cv_b_dw']), cv_g_ln=row(w['cv_g_ln']), cv_b_ln=row(w['cv_b_ln']),
        cv_w_pw=w['cv_w_pw'].astype(BF16), cv_b_pw=row(w['cv_b_pw']),
    )


def _trunk(x, p, w, state):
    b, l, _ = x.shape
    n = b * l
    h = x.reshape(n, D)
    depth = p.shape[0]
    new = {}
    for i in range(depth):
        kind, j = i % 4, i // 4
        g_mix = w['norm_mix'][i]
        if kind == 0:
            q, k, v, og, gates = _ml_in(h, g_mix, w['ml_wq'][j], w['ml_wk'][j], w['ml_wv'][j], w['ml_wo'][j],
                                        w['ml_wg'][j], w['ml_b_gate'][j])
            if state is None:
                y, c1, n1, m1 = _ml_chunk(q.reshape(b, l, -1), k.reshape(b, l, -1), v.reshape(b, l, D),
                                          og.reshape(b, l, D), gates.reshape(b, l, -1), w['ml_g_out'][j])
                m1 = m1[:, :, 0]
            else:
                y, c1, n1, m1 = _ml_step(q, k, v, og, gates, state['ml_c'][j], state['ml_n'][j],
                                         state['ml_m'][j], w['ml_g_out'][j])
            new.setdefault('ml_c', []).append(c1)
            new.setdefault('ml_n', []).append(n1)
            new.setdefault('ml_m', []).append(m1)
            h = _out_proj(h, y.reshape(n, D), w['ml_w_out'][j])
        elif kind == 1:
            if state is None:
                h3, tail = _pool_prompt(h.reshape(b, l, D), g_mix, w['pl_w'][j], w['pl_scale'][j])
                h, st = h3.reshape(n, D), tail[:, POOL_HALO - POOL_STATE:]
            else:
                h, st = _pool_step(h, state['pool'][j], g_mix, w['pl_w'][j], w['pl_scale'][j], state['start_pos'])
            new.setdefault('pool', []).append(st)
        elif kind == 2:
            q, k, v = _sb_qkv(h, g_mix, w['sb_wq'][j], w['sb_wk'][j], w['sb_wv'][j], w['sb_g_q'][j], w['sb_g_k'][j])
            if state is None:
                att = _sb_attn_prompt(q.reshape(b, l, D), k.reshape(b, l, D), v.reshape(b, l, D),
                                      w['sb_b_logit'][j]).reshape(n, D)
            else:
                att = _sb_attn_paged(q, state['sb_k'][j], state['sb_v'][j], state['page_table'], w['sb_b_logit'][j])
            new.setdefault('sb_k', []).append(k.reshape(b, l, SB_HEADS, SB_DH))
            new.setdefault('sb_v', []).append(v.reshape(b, l, SB_HEADS, SB_DH))
            h = _out_proj(h, att, w['sb_w_out'][j])
        else:
            cv = (g_mix, w['cv_w_glu'][j], w['cv_b_glu'][j], w['cv_w_dw'][j], w['cv_b_dw'][j], w['cv_g_ln'][j],
                  w['cv_b_ln'][j], w['cv_w_pw'][j], w['cv_b_pw'][j])
            if state is None:
                h3, tail = _conv_prompt(h.reshape(b, l, D), *cv)
                h, st = h3.reshape(n, D), tail[:, CONV_HALO - CONV_STATE:]
            else:
                h, st = _conv_step(h, state['conv'][j], *cv)
            new.setdefault('conv', []).append(st)
        h = _ffn_ple(h, p[i].reshape(n, PLE_DIM), w['norm_ffn'][i], w['w_ffn_gu'][i], w['w_ffn_down'][i],
                     w['norm_ple'][i], w['w_ple_gate'][i], w['w_ple'][i])
    stacked = tuple(jnp.stack(new[name]) for name in ('ml_c', 'ml_n', 'ml_m', 'pool', 'sb_k', 'sb_v', 'conv'))
    return (h.reshape(b, l, D),) + stacked


def kernel(x_prompt, x_sample, state_mlstm_C, state_mlstm_n, state_mlstm_m, state_pool, cache_sb_k, cache_sb_v,
           state_conv, page_table, p_prompt, p_sample, norm_mix, norm_ffn, w_ffn_gu, w_ffn_down, w_ple, norm_ple,
           w_ple_gate, ml_w_in, ml_b_gate, ml_g_out, ml_w_out, pl_w, pl_scale, sb_w_qkv, sb_g_q, sb_g_k, sb_b_logit,
           sb_w_out, cv_w_glu, cv_b_glu, cv_w_dw, cv_b_dw, cv_g_ln, cv_b_ln, cv_w_pw, cv_b_pw):
    w = _prepare_weights(dict(
        norm_mix=norm_mix, norm_ffn=norm_ffn, w_ffn_gu=w_ffn_gu, w_ffn_down=w_ffn_down, w_ple=w_ple,
        norm_ple=norm_ple, w_ple_gate=w_ple_gate, ml_w_in=ml_w_in, ml_b_gate=ml_b_gate, ml_g_out=ml_g_out,
        ml_w_out=ml_w_out, pl_w=pl_w, pl_scale=pl_scale, sb_w_qkv=sb_w_qkv, sb_g_q=sb_g_q, sb_g_k=sb_g_k,
        sb_b_logit=sb_b_logit, sb_w_out=sb_w_out, cv_w_glu=cv_w_glu, cv_b_glu=cv_b_glu, cv_w_dw=cv_w_dw,
        cv_b_dw=cv_b_dw, cv_g_ln=cv_g_ln, cv_b_ln=cv_b_ln, cv_w_pw=cv_w_pw, cv_b_pw=cv_b_pw))
    prompt = _trunk(x_prompt, p_prompt, w, None)
    sample_state = dict(ml_c=state_mlstm_C, ml_n=state_mlstm_n, ml_m=state_mlstm_m, pool=state_pool,
                        sb_k=cache_sb_k, sb_v=cache_sb_v, page_table=page_table, conv=state_conv,
                        start_pos=page_table.shape[1] * PAGE_SIZE)
    sample = _trunk(x_sample, p_sample, w, sample_state)
    return (prompt[0], sample[0]) + prompt[1:] + sample[1:]
```

```python
import functools

import jax
import jax.numpy as jnp
from jax import lax
from jax.experimental import pallas as pl
from jax.experimental.pallas import tpu as pltpu

F32 = jnp.float32
BF16 = jnp.bfloat16
HIGHEST = lax.Precision.HIGHEST

EPS = 1e-6
LOG2E = 1.4426950408889634
D = 1024
PLE_DIM = 256
D_FF = 2816
ML_HEADS, ML_DQK, ML_DV = 8, 64, 128
POOL_WINDOWS = (2, 4, 8, 16)
POOL_GW = D // len(POOL_WINDOWS)
POOL_STATE = 15
SB_HEADS, SB_DH = 16, 64
CONV_WIDTH = 31
CONV_STATE = CONV_WIDTH - 1
PAGE_SIZE = 128

V7X_VMEM_LIMIT = 56 * 1024 * 1024
LANES = 128
SUBLANES = 8

FF_CHUNK = 256
ML_CHUNK = 256
SB_BLOCK = 256
PAGES_PER_STEP = 4
POOL_HALO = 16
CONV_HALO = 32


def _row_tile(n):
    return 512 if n % 512 == 0 else n


def _params(*sem):
    return pltpu.CompilerParams(dimension_semantics=sem, vmem_limit_bytes=V7X_VMEM_LIMIT)


def _full(shape):
    return pl.BlockSpec(shape, lambda *_: (0,) * len(shape), pipeline_mode=pl.Buffered(1))


def _rows(tm, width):
    return pl.BlockSpec((tm, width), lambda i: (i, 0))


def _rms(x, g):
    return x * lax.rsqrt(jnp.mean(x * x, axis=-1, keepdims=True) + EPS) * g


def _bdot(a, b):
    return jnp.dot(a.astype(BF16), b.astype(BF16), preferred_element_type=F32)


def _bdot_nt(a, b):
    return lax.dot_general(a.astype(BF16), b.astype(BF16), (((1,), (1,)), ((), ())),
                           preferred_element_type=F32)


def _sigmoid(x):
    return 1.0 / (1.0 + jnp.exp(-x))


def _log_sigmoid(x):
    return jnp.minimum(x, 0.0) - jnp.log(1.0 + jnp.exp(-jnp.abs(x)))


def _ffn_ple_kernel(h_ref, p_ref, gf_ref, wgu_ref, wd_ref, gp_ref, wgate_ref, wple_ref, o_ref):
    h = h_ref[...]
    u = _rms(h, gf_ref[...]).astype(BF16)
    acc = jnp.zeros(h.shape, F32)
    for c in range(D_FF // FF_CHUNK):
        lo = c * FF_CHUNK
        gate = jnp.dot(u, wgu_ref[:, lo:lo + FF_CHUNK], preferred_element_type=F32)
        up = jnp.dot(u, wgu_ref[:, D_FF + lo:D_FF + lo + FF_CHUNK], preferred_element_type=F32)
        act = (gate * _sigmoid(gate) * up).astype(BF16)
        acc = acc + jnp.dot(act, wd_ref[lo:lo + FF_CHUNK, :], preferred_element_type=F32)
    h1 = h + acc
    u2 = _rms(h1, gp_ref[...])
    gate = _sigmoid(_bdot(u2, wgate_ref[...]))
    o_ref[...] = h1 + gate * _bdot(p_ref[...], wple_ref[...])


def _ffn_ple(h, p, g_ffn, w_gu, w_down, g_ple, w_gate, w_ple):
    n = h.shape[0]
    tm = _row_tile(n)
    return pl.pallas_call(
        _ffn_ple_kernel,
        out_shape=jax.ShapeDtypeStruct((n, D), F32),
        grid=(n // tm,),
        in_specs=[_rows(tm, D), _rows(tm, PLE_DIM), _full((1, D)), _full((D, 2 * D_FF)), _full((D_FF, D)),
                  _full((1, D)), _full((D, D)), _full((PLE_DIM, D))],
        out_specs=_rows(tm, D),
        compiler_params=_params("parallel"),
        name="ffn_ple",
    )(h, p, g_ffn, w_gu, w_down, g_ple, w_gate, w_ple)


def _out_proj_kernel(h_ref, y_ref, w_ref, o_ref):
    o_ref[...] = h_ref[...] + _bdot(y_ref[...], w_ref[...])


def _out_proj(h, y, w):
    n = h.shape[0]
    tm = _row_tile(n)
    return pl.pallas_call(
        _out_proj_kernel,
        out_shape=jax.ShapeDtypeStruct((n, D), F32),
        grid=(n // tm,),
        in_specs=[_rows(tm, D), _rows(tm, D), _full((D, D))],
        out_specs=_rows(tm, D),
        compiler_params=_params("parallel"),
        name="out_proj",
    )(h, y, w)


def _ml_in_kernel(h_ref, g_ref, wq_ref, wk_ref, wv_ref, wo_ref, wg_ref, bg_ref,
                  q_ref, k_ref, v_ref, og_ref, gates_ref):
    uf = _rms(h_ref[...], g_ref[...])
    u = uf.astype(BF16)
    q_ref[...] = jnp.dot(u, wq_ref[...], preferred_element_type=F32)
    k_ref[...] = jnp.dot(u, wk_ref[...], preferred_element_type=F32) * (ML_DQK ** -0.5)
    v_ref[...] = jnp.dot(u, wv_ref[...], preferred_element_type=F32)
    og_ref[...] = _sigmoid(jnp.dot(u, wo_ref[...], preferred_element_type=F32))
    gates = jnp.dot(uf, wg_ref[...], preferred_element_type=F32, precision=HIGHEST) + bg_ref[...]
    is_forget = lax.broadcasted_iota(jnp.int32, gates.shape, 1) >= ML_HEADS
    gates_ref[...] = jnp.where(is_forget, _log_sigmoid(gates), gates)


def _ml_in(h, g, wq, wk, wv, wo, wg, bg):
    n = h.shape[0]
    tm = _row_tile(n)
    hq = ML_HEADS * ML_DQK
    return pl.pallas_call(
        _ml_in_kernel,
        out_shape=[jax.ShapeDtypeStruct((n, hq), F32), jax.ShapeDtypeStruct((n, hq), F32),
                   jax.ShapeDtypeStruct((n, D), F32), jax.ShapeDtypeStruct((n, D), F32),
                   jax.ShapeDtypeStruct((n, 2 * ML_HEADS), F32)],
        grid=(n // tm,),
        in_specs=[_rows(tm, D), _full((1, D)), _full((D, hq)), _full((D, hq)), _full((D, D)), _full((D, D)),
                  _full((D, 2 * ML_HEADS)), _full((1, 2 * ML_HEADS))],
        out_specs=[_rows(tm, hq), _rows(tm, hq), _rows(tm, D), _rows(tm, D), _rows(tm, 2 * ML_HEADS)],
        compiler_params=_params("parallel"),
        name="ml_in",
    )(h, g, wq, wk, wv, wo, wg, bg)


def _head_norm_gate(hh, g, og):
    return og * (hh * lax.rsqrt(jnp.mean(hh * hh, axis=-1, keepdims=True) + EPS) * g)


def _ml_chunk_kernel(q_ref, k_ref, v_ref, og_ref, gc_ref, gr_ref, gout_ref, y_ref, c_ref, n_ref, m_ref):
    @pl.when(pl.program_id(1) == 0)
    def _():
        c_ref[...] = jnp.zeros(c_ref.shape, F32)
        n_ref[...] = jnp.zeros(n_ref.shape, F32)
        m_ref[...] = jnp.zeros(m_ref.shape, F32)

    lc = q_ref.shape[1]
    row = lax.broadcasted_iota(jnp.int32, (lc, lc), 0)
    col = lax.broadcasted_iota(jnp.int32, (lc, lc), 1)
    causal = col <= row
    gates_c = gc_ref[0]
    gates_r = gr_ref[0]
    b_c = jnp.dot(causal.astype(F32), gates_c[:, ML_HEADS:], preferred_element_type=F32, precision=HIGHEST)
    b_r = jnp.dot(gates_r[ML_HEADS:, :], (row <= col).astype(F32), preferred_element_type=F32,
                  precision=HIGHEST)
    for hd in range(ML_HEADS):
        qh = q_ref[0, :, hd * ML_DQK:(hd + 1) * ML_DQK]
        kh = k_ref[0, :, hd * ML_DQK:(hd + 1) * ML_DQK]
        vh = v_ref[0, :, hd * ML_DV:(hd + 1) * ML_DV]
        bc = b_c[:, hd:hd + 1]
        br = b_r[hd:hd + 1, :]
        li_c = gates_c[:, hd:hd + 1]
        li_r = gates_r[hd:hd + 1, :]
        m_prev = m_ref[0, hd:hd + 1, 0:1]
        c_prev = c_ref[0, hd]
        n_prev = n_ref[0, hd:hd + 1, :]

        dlog = jnp.where(causal, bc - br + li_r, -jnp.inf)
        inter = bc + m_prev
        m_t = jnp.maximum(inter, jnp.max(dlog, axis=1, keepdims=True))
        w_intra = jnp.exp(dlog - m_t)
        w_inter = jnp.exp(inter - m_t)
        s_mat = w_intra * _bdot_nt(qh, kh)
        num = w_inter * _bdot(qh, c_prev) + _bdot(s_mat, vh)
        den = w_inter * jnp.sum(qh * n_prev, axis=1, keepdims=True) + jnp.sum(s_mat, axis=1, keepdims=True)
        hh = num / jnp.maximum(jnp.abs(den), jnp.exp(-m_t))
        y_ref[0, :, hd * ML_DV:(hd + 1) * ML_DV] = _head_norm_gate(
            hh, gout_ref[:, hd * ML_DV:(hd + 1) * ML_DV], og_ref[0, :, hd * ML_DV:(hd + 1) * ML_DV])

        g_end = br[:, lc - 1:lc]
        dl_end = g_end - bc + li_c
        m_new = jnp.maximum(g_end + m_prev, jnp.max(dl_end, axis=0, keepdims=True))
        a_prev = jnp.exp(g_end + m_prev - m_new)
        kw = jnp.exp(dl_end - m_new) * kh
        c_ref[0, hd] = a_prev * c_prev + lax.dot_general(
            kw.astype(BF16), vh.astype(BF16), (((0,), (0,)), ((), ())), preferred_element_type=F32)
        n_ref[0, hd:hd + 1, :] = a_prev * n_prev + jnp.sum(kw, axis=0, keepdims=True)
        m_ref[0, hd:hd + 1, :] = jnp.broadcast_to(m_new, (1, LANES))


def _ml_chunk(q, k, v, og, gates, g_out):
    b, l, _ = q.shape
    lc = ML_CHUNK if l % ML_CHUNK == 0 else l
    gates_t = jnp.swapaxes(gates, 1, 2)
    hq = ML_HEADS * ML_DQK
    blk = lambda w: pl.BlockSpec((1, lc, w), lambda i, j: (i, j, 0))
    return pl.pallas_call(
        _ml_chunk_kernel,
        out_shape=[jax.ShapeDtypeStruct((b, l, D), F32),
                   jax.ShapeDtypeStruct((b, ML_HEADS, ML_DQK, ML_DV), F32),
                   jax.ShapeDtypeStruct((b, ML_HEADS, ML_DQK), F32),
                   jax.ShapeDtypeStruct((b, ML_HEADS, LANES), F32)],
        grid=(b, l // lc),
        in_specs=[blk(hq), blk(hq), blk(D), blk(D), blk(2 * ML_HEADS),
                  pl.BlockSpec((1, 2 * ML_HEADS, lc), lambda i, j: (i, 0, j)),
                  pl.BlockSpec((1, D), lambda i, j: (0, 0))],
        out_specs=[blk(D),
                   pl.BlockSpec((1, ML_HEADS, ML_DQK, ML_DV), lambda i, j: (i, 0, 0, 0)),
                   pl.BlockSpec((1, ML_HEADS, ML_DQK), lambda i, j: (i, 0, 0)),
                   pl.BlockSpec((1, ML_HEADS, LANES), lambda i, j: (i, 0, 0))],
        compiler_params=_params("parallel", "arbitrary"),
        name="ml_chunk",
    )(q, k, v, og, gates, gates_t, g_out)


def _ml_step_kernel(qt_ref, kt_ref, v_ref, og_ref, gates_ref, c_ref, nt_ref, m_ref, gout_ref,
                    y_ref, c_out, nt_out, m_out):
    lane = lax.broadcasted_iota(jnp.int32, (1, ML_HEADS), 1)
    m_row = jnp.zeros((1, ML_HEADS), F32)
    for hd in range(ML_HEADS):
        qc = qt_ref[0, :, hd:hd + 1]
        kc = kt_ref[0, :, hd:hd + 1]
        nc = nt_ref[0, :, hd:hd + 1]
        vh = v_ref[0, :, hd * ML_DV:(hd + 1) * ML_DV]
        c_prev = c_ref[0, hd]
        li = gates_ref[0, :, hd:hd + 1]
        lf = gates_ref[0, :, ML_HEADS + hd:ML_HEADS + hd + 1]
        m_prev = m_ref[0, :, hd:hd + 1]
        inter = lf + m_prev
        m_t = jnp.maximum(inter, li)
        w_intra = jnp.exp(li - m_t)
        w_inter = jnp.exp(inter - m_t)
        s = w_intra * jnp.sum(qc * kc, axis=0, keepdims=True)
        num = w_inter * jnp.sum(qc * c_prev, axis=0, keepdims=True) + s * vh
        den = w_inter * jnp.sum(qc * nc, axis=0, keepdims=True) + s
        hh = num / jnp.maximum(jnp.abs(den), jnp.exp(-m_t))
        y_ref[0, :, hd * ML_DV:(hd + 1) * ML_DV] = _head_norm_gate(
            hh, gout_ref[:, hd * ML_DV:(hd + 1) * ML_DV], og_ref[0, :, hd * ML_DV:(hd + 1) * ML_DV])
        c_out[0, hd] = w_inter * c_prev + w_intra * (kc * vh)
        nt_out[0, :, hd:hd + 1] = w_inter * nc + w_intra * kc
        m_row = jnp.where(lane == hd, m_t, m_row)
    m_out[0] = m_row


def _ml_step(q, k, v, og, gates, c0, n0, m0, g_out):
    b = q.shape[0]
    to_t = lambda a: jnp.swapaxes(a.reshape(b, ML_HEADS, ML_DQK), 1, 2)
    t_spec = pl.BlockSpec((1, ML_DQK, ML_HEADS), lambda i: (i, 0, 0))
    row = lambda w: pl.BlockSpec((1, 1, w), lambda i: (i, 0, 0))
    c_spec = pl.BlockSpec((1, ML_HEADS, ML_DQK, ML_DV), lambda i: (i, 0, 0, 0))
    y, c1, nt1, m1 = pl.pallas_call(
        _ml_step_kernel,
        out_shape=[jax.ShapeDtypeStruct((b, 1, D), F32),
                   jax.ShapeDtypeStruct((b, ML_HEADS, ML_DQK, ML_DV), F32),
                   jax.ShapeDtypeStruct((b, ML_DQK, ML_HEADS), F32),
                   jax.ShapeDtypeStruct((b, 1, ML_HEADS), F32)],
        grid=(b,),
        in_specs=[t_spec, t_spec, row(D), row(D), row(2 * ML_HEADS), c_spec, t_spec, row(ML_HEADS),
                  pl.BlockSpec((1, D), lambda i: (0, 0))],
        out_specs=[row(D), c_spec, t_spec, row(ML_HEADS)],
        compiler_params=_params("parallel"),
        name="ml_step",
    )(to_t(q), to_t(k), v.reshape(b, 1, D), og.reshape(b, 1, D), gates.reshape(b, 1, 2 * ML_HEADS),
      c0, jnp.swapaxes(n0, 1, 2), m0.reshape(b, 1, ML_HEADS), g_out)
    return y.reshape(b, D), c1, jnp.swapaxes(nt1, 1, 2), m1.reshape(b, ML_HEADS)


def _pool_project(h, pooled, w_ref, scale_ref, o_ref):
    for g in range(len(POOL_WINDOWS)):
        cols = slice(g * POOL_GW, (g + 1) * POOL_GW)
        o_ref[:, cols] = h[:, cols] + _bdot(pooled[g], w_ref[g]) * scale_ref[:, cols]


def _pool_kernel(h_ref, g_ref, w_ref, scale_ref, o_ref, tail_ref, ext_ref):
    tm = h_ref.shape[1]
    step = pl.program_id(1)

    @pl.when(step == 0)
    def _():
        ext_ref[0:POOL_HALO, :] = jnp.zeros((POOL_HALO, D), F32)

    @pl.when(step > 0)
    def _():
        ext_ref[0:POOL_HALO, :] = ext_ref[tm:tm + POOL_HALO, :]

    h = h_ref[0]
    u = _rms(h, g_ref[...])
    ext_ref[POOL_HALO:POOL_HALO + tm, :] = u
    tail_ref[0] = u[tm - POOL_HALO:, :]
    pos = step * tm + lax.broadcasted_iota(jnp.int32, (tm, 1), 0)
    pooled = []
    for g, w in enumerate(POOL_WINDOWS):
        cols = slice(g * POOL_GW, (g + 1) * POOL_GW)
        wsum = u[:, cols]
        for back in range(1, w):
            wsum = wsum + ext_ref[POOL_HALO - back:POOL_HALO - back + tm, cols]
        cnt = jnp.minimum(pos + 1, w).astype(F32)
        pooled.append(wsum / cnt - u[:, cols])
    _pool_project(h, pooled, w_ref, scale_ref, o_ref.at[0])


def _pool_prompt(h, g, w, scale):
    b, l, _ = h.shape
    tm = _row_tile(l)
    return pl.pallas_call(
        _pool_kernel,
        out_shape=[jax.ShapeDtypeStruct((b, l, D), F32), jax.ShapeDtypeStruct((b, POOL_HALO, D), F32)],
        grid=(b, l // tm),
        in_specs=[pl.BlockSpec((1, tm, D), lambda i, j: (i, j, 0)),
                  pl.BlockSpec((1, D), lambda i, j: (0, 0)),
                  pl.BlockSpec((len(POOL_WINDOWS), POOL_GW, POOL_GW), lambda i, j: (0, 0, 0)),
                  pl.BlockSpec((1, D), lambda i, j: (0, 0))],
        out_specs=[pl.BlockSpec((1, tm, D), lambda i, j: (i, j, 0)),
                   pl.BlockSpec((1, POOL_HALO, D), lambda i, j: (i, 0, 0))],
        scratch_shapes=[pltpu.VMEM((POOL_HALO + tm, D), F32)],
        compiler_params=_params("parallel", "arbitrary"),
        name="pool",
    )(h, g, w, scale)


def _pool_step_kernel(h_ref, pref_ref, g_ref, w_ref, scale_ref, o_ref, state_ref, *, start_pos):
    h = h_ref[...]
    u = _rms(h, g_ref[...])
    pooled = []
    for g, w in enumerate(POOL_WINDOWS):
        cols = slice(g * POOL_GW, (g + 1) * POOL_GW)
        wsum = u[:, cols]
        for back in range(1, w):
            wsum = wsum + pref_ref[POOL_STATE - back, :, cols]
        pooled.append(wsum / float(min(start_pos + 1, w)) - u[:, cols])
    _pool_project(h, pooled, w_ref, scale_ref, o_ref)
    state_ref[0:POOL_STATE - 1] = pref_ref[1:POOL_STATE]
    state_ref[POOL_STATE - 1] = u


def _pool_step(h, prefix, g, w, scale, start_pos):
    b = h.shape[0]
    out, state_t = pl.pallas_call(
        functools.partial(_pool_step_kernel, start_pos=start_pos),
        out_shape=[jax.ShapeDtypeStruct((b, D), F32), jax.ShapeDtypeStruct((POOL_STATE, b, D), F32)],
        compiler_params=pltpu.CompilerParams(vmem_limit_bytes=V7X_VMEM_LIMIT),
        name="pool_step",
    )(h, jnp.swapaxes(prefix, 0, 1), g, w, scale)
    return out, jnp.swapaxes(state_t, 0, 1)


def _pair_norm(x, g):
    first = lax.broadcasted_iota(jnp.int32, x.shape, 1) < SB_DH
    sq = x * x
    s_first = jnp.sum(jnp.where(first, sq, 0.0), axis=1, keepdims=True)
    s_second = jnp.sum(jnp.where(first, 0.0, sq), axis=1, keepdims=True)
    inv = jnp.where(first, lax.rsqrt(s_first * (1.0 / SB_DH) + EPS), lax.rsqrt(s_second * (1.0 / SB_DH) + EPS))
    return x * inv * g


def _sb_qkv_kernel(h_ref, g_ref, wq_ref, wk_ref, wv_ref, gq_ref, gk_ref, q_ref, k_ref, v_ref):
    u = _rms(h_ref[...], g_ref[...]).astype(BF16)
    v_ref[...] = jnp.dot(u, wv_ref[...], preferred_element_type=F32)
    for w_ref, gain_ref, o_ref in ((wq_ref, gq_ref, q_ref), (wk_ref, gk_ref, k_ref)):
        x = jnp.dot(u, w_ref[...], preferred_element_type=F32)
        for p in range(D // LANES):
            cols = slice(p * LANES, (p + 1) * LANES)
            o_ref[:, cols] = _pair_norm(x[:, cols], gain_ref[...])


def _sb_qkv(h, g, wq, wk, wv, gq, gk):
    n = h.shape[0]
    tm = _row_tile(n)
    return pl.pallas_call(
        _sb_qkv_kernel,
        out_shape=[jax.ShapeDtypeStruct((n, D), F32)] * 3,
        grid=(n // tm,),
        in_specs=[_rows(tm, D), _full((1, D)), _full((D, D)), _full((D, D)), _full((D, D)),
                  _full((1, LANES)), _full((1, LANES))],
        out_specs=[_rows(tm, D)] * 3,
        compiler_params=_params("parallel"),
        name="sb_qkv",
    )(h, g, wq, wk, wv, gq, gk)


def _sb_weights(z, valid):
    softplus = jnp.maximum(z, 0.0) + jnp.log(1.0 + jnp.exp(-jnp.abs(z)))
    log_rest = -softplus
    if valid is not None:
        log_rest = jnp.where(valid, log_rest, 0.0)
    return z - softplus, log_rest


def _neg_abs(x):
    bits = lax.bitcast_convert_type(x, jnp.uint32) | jnp.uint32(0x80000000)
    return lax.bitcast_convert_type(bits, F32)


def _sb_pair_block(q2, k_ref, v_ref, start, ntri2, carry_ref, valid):
    tk = ntri2.shape[1]
    kb = k_ref[0, pl.ds(start, tk), :]
    vb = v_ref[0, pl.ds(start, tk), :]
    lane = lax.broadcasted_iota(jnp.int32, kb.shape, 1)
    splits, log_betas, firsts = [], [], []
    for hh in range(2):
        own = (lane < SB_DH) if hh == 0 else (lane >= SB_DH)
        bias_lanes = (lane >= SB_DH) & (lane < SB_DH + 2) if hh == 0 else (lane < 2)
        kh = jnp.where(own, kb, jnp.where(bias_lanes, 1.0, 0.0)).astype(BF16)
        z = lax.dot_general(q2[hh], kh, (((1,), (1,)), ((), ())), preferred_element_type=F32)
        softplus = jnp.maximum(z, 0.0) + jnp.log(1.0 + jnp.exp2(_neg_abs(z))) * LOG2E
        log_betas.append(z - softplus)
        if valid is not None:
            softplus = jnp.where(valid, softplus, 0.0)
        hi = softplus.astype(BF16)
        lo = (softplus - hi.astype(F32)).astype(BF16)
        splits.append(jnp.concatenate([hi, lo], axis=1))
        firsts.append(softplus[:, 0:1])
    inner = jnp.dot(jnp.concatenate(splits, axis=0), ntri2, preferred_element_type=F32)
    tq = inner.shape[0] // 2
    out = None
    for hh in range(2):
        own = (lane < SB_DH) if hh == 0 else (lane >= SB_DH)
        inner_h = inner[hh * tq:(hh + 1) * tq]
        carry = carry_ref[hh]
        a = jnp.exp2(log_betas[hh] + inner_h + carry)
        if valid is not None:
            a = jnp.where(valid, a, 0.0)
        carry_ref[hh] = carry + inner_h[:, 0:1] - firsts[hh]
        part = jnp.dot(a.astype(BF16), jnp.where(own, vb, 0.0).astype(BF16), preferred_element_type=F32)
        out = part if out is None else out + part
    return out


def _sb_attn_kernel(bias_ref, q_ref, k_ref, v_ref, ntri2_ref, o_ref, carry_ref):
    tq = q_ref.shape[1]
    pair = pl.program_id(1)
    qi = pl.program_id(2)
    row = lax.broadcasted_iota(jnp.int32, (tq, tq), 0)
    col = lax.broadcasted_iota(jnp.int32, (tq, tq), 1)
    strict = col < row
    ntri2 = ntri2_ref[...]
    q = q_ref[0] * (SB_DH ** -0.5 * LOG2E)
    lane = lax.broadcasted_iota(jnp.int32, q.shape, 1)
    q2 = []
    for hh in range(2):
        bias_hi = bias_ref[0, 2 * pair + hh]
        bias_lo = bias_ref[1, 2 * pair + hh]
        own = (lane < SB_DH) if hh == 0 else (lane >= SB_DH)
        base = SB_DH if hh == 0 else 0
        extra = jnp.where(lane == base, bias_hi, jnp.where(lane == base + 1, bias_lo, 0.0))
        q2.append(jnp.where(own, q, extra).astype(BF16))
    carry_ref[...] = jnp.zeros(carry_ref.shape, F32)
    o_ref[0] = _sb_pair_block(q2, k_ref, v_ref, pl.multiple_of(qi * tq, tq), ntri2, carry_ref, strict)

    def body(jj, _):
        off = pl.multiple_of((qi - 1 - jj) * tq, tq)
        o_ref[0] += _sb_pair_block(q2, k_ref, v_ref, off, ntri2, carry_ref, None)
        return 0

    lax.fori_loop(0, qi, body, 0)


def _later_key_matrix(n):
    r = jnp.arange(n)
    return (r[:, None] > r[None, :]).astype(F32)


def _sb_attn_prompt(q, k, v, b_logit):
    b, l, _ = q.shape
    tq = SB_BLOCK if l % SB_BLOCK == 0 else l
    qspec = pl.BlockSpec((1, tq, LANES), lambda i, p, j: (i, j, p))
    kvspec = pl.BlockSpec((1, l, LANES), lambda i, p, j: (i, 0, p))
    ntri = -_later_key_matrix(tq).astype(BF16)
    bias2 = b_logit * LOG2E
    bias_hi = bias2.astype(BF16).astype(F32)
    bias_split = jnp.stack([bias_hi, bias2 - bias_hi])
    return pl.pallas_call(
        _sb_attn_kernel,
        out_shape=jax.ShapeDtypeStruct((b, l, D), F32),
        grid=(b, SB_HEADS // 2, l // tq),
        in_specs=[pl.BlockSpec(memory_space=pltpu.SMEM), qspec, kvspec, kvspec,
                  pl.BlockSpec((2 * tq, tq), lambda i, p, j: (0, 0))],
        out_specs=qspec,
        scratch_shapes=[pltpu.VMEM((2, tq, 1), F32)],
        compiler_params=_params("parallel", "parallel", "arbitrary"),
        name="sb_attn",
    )(bias_split, q, k, v, jnp.concatenate([ntri, ntri], axis=0))


def _sb_paged_kernel(pt_ref, q_ref, bias_ref, *refs):
    k_refs, v_refs = refs[:PAGES_PER_STEP], refs[PAGES_PER_STEP:2 * PAGES_PER_STEP]
    tri_ref, o_ref, acc_ref, carry_ref = refs[2 * PAGES_PER_STEP:]
    step = pl.program_id(1)

    @pl.when(step == 0)
    def _():
        acc_ref[...] = jnp.zeros(acc_ref.shape, F32)
        carry_ref[...] = jnp.zeros(carry_ref.shape, F32)

    qs = q_ref[0] * (SB_DH ** -0.5)
    lane = lax.broadcasted_iota(jnp.int32, (SB_HEADS, PAGE_SIZE), 1)
    ways = 4
    ones = jnp.ones((SB_DH, LANES), BF16)
    for k_ref, v_ref in zip(k_refs, v_refs):
        prod = (k_ref[0] * qs).reshape(PAGE_SIZE * SB_HEADS, SB_DH).astype(BF16)
        z_rows = jnp.dot(prod, ones, preferred_element_type=F32)
        z_parts = [jnp.zeros((SB_HEADS, PAGE_SIZE), F32) for _ in range(ways)]
        for s in range(PAGE_SIZE):
            z_parts[s % ways] = jnp.where(lane == s, z_rows[s * SB_HEADS:(s + 1) * SB_HEADS], z_parts[s % ways])
        z = (z_parts[0] + z_parts[1]) + (z_parts[2] + z_parts[3]) + bias_ref[...]
        log_beta, log_rest = _sb_weights(z, None)
        hi = log_rest.astype(BF16)
        rem = log_rest - hi.astype(F32)
        mid = rem.astype(BF16)
        lo = (rem - mid.astype(F32)).astype(BF16)
        terms = jnp.dot(jnp.concatenate([hi, mid, lo], axis=0), tri_ref[...], preferred_element_type=F32)
        inner = terms[:SB_HEADS] + terms[SB_HEADS:2 * SB_HEADS] + terms[2 * SB_HEADS:]
        carry = carry_ref[:, 0:1]
        a = jnp.exp(log_beta + inner + carry)
        carry_ref[...] = jnp.broadcast_to(carry + jnp.sum(log_rest, axis=1, keepdims=True), carry_ref.shape)
        acc_parts = [jnp.zeros((SB_HEADS, SB_DH), F32) for _ in range(ways)]
        for s in range(PAGE_SIZE):
            a_col = jnp.sum(jnp.where(lane == s, a, 0.0), axis=1, keepdims=True)
            acc_parts[s % ways] = acc_parts[s % ways] + a_col * v_ref[0, s]
        acc_ref[...] += (acc_parts[0] + acc_parts[1]) + (acc_parts[2] + acc_parts[3])

    @pl.when(step == pl.num_programs(1) - 1)
    def _():
        o_ref[0] = acc_ref[...]


def _sb_attn_paged(q, cache_k, cache_v, page_table, b_logit):
    b = q.shape[0]
    n_pages = page_table.shape[1]
    steps = n_pages // PAGES_PER_STEP

    def page_spec(t):
        return pl.BlockSpec((1, PAGE_SIZE, SB_HEADS, SB_DH),
                            lambda i, j, pt: (pt[i * n_pages + n_pages - 1 - (j * PAGES_PER_STEP + t)], 0, 0, 0))

    pages = [page_spec(t) for t in range(PAGES_PER_STEP)]
    head_block = pl.BlockSpec((1, SB_HEADS, SB_DH), lambda i, j, pt: (i, 0, 0))
    out = pl.pallas_call(
        _sb_paged_kernel,
        out_shape=jax.ShapeDtypeStruct((b, SB_HEADS, SB_DH), F32),
        grid_spec=pltpu.PrefetchScalarGridSpec(
            num_scalar_prefetch=1,
            grid=(b, steps),
            in_specs=[head_block, pl.BlockSpec((SB_HEADS, 1), lambda i, j, pt: (0, 0))] + pages + pages
                     + [pl.BlockSpec((PAGE_SIZE, PAGE_SIZE), lambda i, j, pt: (0, 0))],
            out_specs=head_block,
            scratch_shapes=[pltpu.VMEM((SB_HEADS, SB_DH), F32), pltpu.VMEM((SB_HEADS, LANES), F32)]),
        compiler_params=_params("parallel", "arbitrary"),
        name="sb_paged",
    )(page_table.reshape(-1), q.reshape(b, SB_HEADS, SB_DH), b_logit.reshape(SB_HEADS, 1),
      *([cache_k] * PAGES_PER_STEP), *([cache_v] * PAGES_PER_STEP), _later_key_matrix(PAGE_SIZE).astype(BF16))
    return out.reshape(b, D)


def _glu(h, g, w_ref, b_ref):
    a = _bdot(_rms(h, g), w_ref[...]) + b_ref[...]
    return a[:, :D] * _sigmoid(a[:, D:])


def _conv_tail(h, c, gln_ref, bln_ref, wpw_ref, bpw_ref):
    cc = c - jnp.mean(c, axis=-1, keepdims=True)
    y = cc * lax.rsqrt(jnp.mean(cc * cc, axis=-1, keepdims=True) + EPS) * gln_ref[...] + bln_ref[...]
    y = y * _sigmoid(y)
    return h + _bdot(y, wpw_ref[...]) + bpw_ref[...]


def _conv_kernel(h_ref, g_ref, wglu_ref, bglu_ref, wdw_ref, bdw_ref, gln_ref, bln_ref, wpw_ref, bpw_ref,
                 o_ref, tail_ref, ext_ref):
    tm = h_ref.shape[1]
    step = pl.program_id(1)

    @pl.when(step == 0)
    def _():
        ext_ref[0:CONV_HALO, :] = jnp.zeros((CONV_HALO, D), F32)

    @pl.when(step > 0)
    def _():
        ext_ref[0:CONV_HALO, :] = ext_ref[tm:tm + CONV_HALO, :]

    h = h_ref[0]
    glu = _glu(h, g_ref[...], wglu_ref, bglu_ref)
    ext_ref[CONV_HALO:CONV_HALO + tm, :] = glu
    tail_ref[0] = glu[tm - CONV_HALO:, :]
    c = jnp.broadcast_to(bdw_ref[...], (tm, D))
    for j in range(CONV_WIDTH):
        lo = CONV_HALO - CONV_STATE + j
        c = c + ext_ref[lo:lo + tm, :] * wdw_ref[j:j + 1, :]
    o_ref[0] = _conv_tail(h, c, gln_ref, bln_ref, wpw_ref, bpw_ref)


def _conv_prompt(h, g, w_glu, b_glu, w_dw, b_dw, g_ln, b_ln, w_pw, b_pw):
    b, l, _ = h.shape
    tm = _row_tile(l)
    const = lambda shape: pl.BlockSpec(shape, lambda i, j: (0,) * len(shape), pipeline_mode=pl.Buffered(1))
    return pl.pallas_call(
        _conv_kernel,
        out_shape=[jax.ShapeDtypeStruct((b, l, D), F32), jax.ShapeDtypeStruct((b, CONV_HALO, D), F32)],
        grid=(b, l // tm),
        in_specs=[pl.BlockSpec((1, tm, D), lambda i, j: (i, j, 0)),
                  const((1, D)), const((D, 2 * D)), const((1, 2 * D)), const((CONV_WIDTH, D)), const((1, D)),
                  const((1, D)), const((1, D)), const((D, D)), const((1, D))],
        out_specs=[pl.BlockSpec((1, tm, D), lambda i, j: (i, j, 0)),
                   pl.BlockSpec((1, CONV_HALO, D), lambda i, j: (i, 0, 0))],
        scratch_shapes=[pltpu.VMEM((CONV_HALO + tm, D), F32)],
        compiler_params=_params("parallel", "arbitrary"),
        name="conv",
    )(h, g, w_glu, b_glu, w_dw, b_dw, g_ln, b_ln, w_pw, b_pw)


def _conv_step_kernel(h_ref, pref_ref, g_ref, wglu_ref, bglu_ref, wdw_ref, bdw_ref, gln_ref, bln_ref,
                      wpw_ref, bpw_ref, o_ref, state_ref):
    h = h_ref[...]
    glu = _glu(h, g_ref[...], wglu_ref, bglu_ref)
    c = bdw_ref[...] + glu * wdw_ref[CONV_STATE:CONV_WIDTH, :]
    for j in range(CONV_STATE):
        c = c + pref_ref[j] * wdw_ref[j:j + 1, :]
    o_ref[...] = _conv_tail(h, c, gln_ref, bln_ref, wpw_ref, bpw_ref)
    state_ref[0:CONV_STATE - 1] = pref_ref[1:CONV_STATE]
    state_ref[CONV_STATE - 1] = glu


def _conv_step(h, prefix, g, w_glu, b_glu, w_dw, b_dw, g_ln, b_ln, w_pw, b_pw):
    b = h.shape[0]
    out, state_t = pl.pallas_call(
        _conv_step_kernel,
        out_shape=[jax.ShapeDtypeStruct((b, D), F32), jax.ShapeDtypeStruct((CONV_STATE, b, D), F32)],
        compiler_params=pltpu.CompilerParams(vmem_limit_bytes=V7X_VMEM_LIMIT),
        name="conv_step",
    )(h, jnp.swapaxes(prefix, 0, 1), g, w_glu, b_glu, w_dw, b_dw, g_ln, b_ln, w_pw, b_pw)
    return out, jnp.swapaxes(state_t, 0, 1)


def _prepare_weights(w):
    hq = ML_HEADS * ML_DQK
    row = lambda a: a.reshape(a.shape[0], 1, -1)
    ml_w_in = w['ml_w_in']
    sb_w = w['sb_w_qkv'].astype(BF16)
    tile_pair = lambda g: jnp.tile(g, (1, 2)).reshape(g.shape[0], 1, LANES)
    return dict(
        norm_mix=row(w['norm_mix']), norm_ffn=row(w['norm_ffn']), norm_ple=row(w['norm_ple']),
        w_ffn_gu=w['w_ffn_gu'].astype(BF16), w_ffn_down=w['w_ffn_down'].astype(BF16),
        w_ple=w['w_ple'].astype(BF16), w_ple_gate=w['w_ple_gate'].astype(BF16),
        ml_wq=ml_w_in[:, :, :hq].astype(BF16), ml_wk=ml_w_in[:, :, hq:2 * hq].astype(BF16),
        ml_wv=ml_w_in[:, :, 2 * hq:2 * hq + D].astype(BF16),
        ml_wo=ml_w_in[:, :, 2 * hq + D:2 * hq + 2 * D].astype(BF16),
        ml_wg=ml_w_in[:, :, 2 * hq + 2 * D:], ml_b_gate=row(w['ml_b_gate']),
        ml_g_out=w['ml_g_out'].reshape(-1, 1, D), ml_w_out=w['ml_w_out'].astype(BF16),
        pl_w=w['pl_w'].astype(BF16), pl_scale=row(w['pl_scale']),
        sb_wq=sb_w[:, :, :D], sb_wk=sb_w[:, :, D:2 * D], sb_wv=sb_w[:, :, 2 * D:],
        sb_g_q=tile_pair(w['sb_g_q']), sb_g_k=tile_pair(w['sb_g_k']), sb_b_logit=w['sb_b_logit'],
        sb_w_out=w['sb_w_out'].astype(BF16),
        cv_w_glu=w['cv_w_glu'].astype(BF16), cv_b_glu=row(w['cv_b_glu']), cv_w_dw=w['cv_w_dw'],
        cv_b_dw=row(w['cv_b_dw']), cv_g_ln=row(w['cv_g_ln']), cv_b_ln=row(w['cv_b_ln']),
        cv_w_pw=w['cv_w_pw'].astype(BF16), cv_b_pw=row(w['cv_b_pw']),
    )


def _trunk(x, p, w, state):
    b, l, _ = x.shape
    n = b * l
    h = x.reshape(n, D)
    depth = p.shape[0]
    new = {}
    for i in range(depth):
        kind, j = i % 4, i // 4
        g_mix = w['norm_mix'][i]
        if kind == 0:
            q, k, v, og, gates = _ml_in(h, g_mix, w['ml_wq'][j], w['ml_wk'][j], w['ml_wv'][j], w['ml_wo'][j],
                                        w['ml_wg'][j], w['ml_b_gate'][j])
            if state is None:
                y, c1, n1, m1 = _ml_chunk(q.reshape(b, l, -1), k.reshape(b, l, -1), v.reshape(b, l, D),
                                          og.reshape(b, l, D), gates.reshape(b, l, -1), w['ml_g_out'][j])
                m1 = m1[:, :, 0]
            else:
                y, c1, n1, m1 = _ml_step(q, k, v, og, gates, state['ml_c'][j], state['ml_n'][j],
                                         state['ml_m'][j], w['ml_g_out'][j])
            new.setdefault('ml_c', []).append(c1)
            new.setdefault('ml_n', []).append(n1)
            new.setdefault('ml_m', []).append(m1)
            h = _out_proj(h, y.reshape(n, D), w['ml_w_out'][j])
        elif kind == 1:
            if state is None:
                h3, tail = _pool_prompt(h.reshape(b, l, D), g_mix, w['pl_w'][j], w['pl_scale'][j])
                h, st = h3.reshape(n, D), tail[:, POOL_HALO - POOL_STATE:]
            else:
                h, st = _pool_step(h, state['pool'][j], g_mix, w['pl_w'][j], w['pl_scale'][j], state['start_pos'])
            new.setdefault('pool', []).append(st)
        elif kind == 2:
            q, k, v = _sb_qkv(h, g_mix, w['sb_wq'][j], w['sb_wk'][j], w['sb_wv'][j], w['sb_g_q'][j], w['sb_g_k'][j])
            if state is None:
                att = _sb_attn_prompt(q.reshape(b, l, D), k.reshape(b, l, D), v.reshape(b, l, D),
                                      w['sb_b_logit'][j]).reshape(n, D)
            else:
                att = _sb_attn_paged(q, state['sb_k'][j], state['sb_v'][j], state['page_table'], w['sb_b_logit'][j])
            new.setdefault('sb_k', []).append(k.reshape(b, l, SB_HEADS, SB_DH))
            new.setdefault('sb_v', []).append(v.reshape(b, l, SB_HEADS, SB_DH))
            h = _out_proj(h, att, w['sb_w_out'][j])
        else:
            cv = (g_mix, w['cv_w_glu'][j], w['cv_b_glu'][j], w['cv_w_dw'][j], w['cv_b_dw'][j], w['cv_g_ln'][j],
                  w['cv_b_ln'][j], w['cv_w_pw'][j], w['cv_b_pw'][j])
            if state is None:
                h3, tail = _conv_prompt(h.reshape(b, l, D), *cv)
                h, st = h3.reshape(n, D), tail[:, CONV_HALO - CONV_STATE:]
            else:
                h, st = _conv_step(h, state['conv'][j], *cv)
            new.setdefault('conv', []).append(st)
        h = _ffn_ple(h, p[i].reshape(n, PLE_DIM), w['norm_ffn'][i], w['w_ffn_gu'][i], w['w_ffn_down'][i],
                     w['norm_ple'][i], w['w_ple_gate'][i], w['w_ple'][i])
    stacked = tuple(jnp.stack(new[name]) for name in ('ml_c', 'ml_n', 'ml_m', 'pool', 'sb_k', 'sb_v', 'conv'))
    return (h.reshape(b, l, D),) + stacked


def kernel(x_prompt, x_sample, state_mlstm_C, state_mlstm_n, state_mlstm_m, state_pool, cache_sb_k, cache_sb_v,
           state_conv, page_table, p_prompt, p_sample, norm_mix, norm_ffn, w_ffn_gu, w_ffn_down, w_ple, norm_ple,
           w_ple_gate, ml_w_in, ml_b_gate, ml_g_out, ml_w_out, pl_w, pl_scale, sb_w_qkv, sb_g_q, sb_g_k, sb_b_logit,
           sb_w_out, cv_w_glu, cv_b_glu, cv_w_dw, cv_b_dw, cv_g_ln, cv_b_ln, cv_w_pw, cv_b_pw):
    w = _prepare_weights(dict(
        norm_mix=norm_mix, norm_ffn=norm_ffn, w_ffn_gu=w_ffn_gu, w_ffn_down=w_ffn_down, w_ple=w_ple,
        norm_ple=norm_ple, w_ple_gate=w_ple_gate, ml_w_in=ml_w_in, ml_b_gate=ml_b_gate, ml_g_out=ml_g_out,
        ml_w_out=ml_w_out, pl_w=pl_w, pl_scale=pl_scale, sb_w_qkv=sb_w_qkv, sb_g_q=sb_g_q, sb_g_k=sb_g_k,
        sb_b_logit=sb_b_logit, sb_w_out=sb_w_out, cv_w_glu=cv_w_glu, cv_b_glu=cv_b_glu, cv_w_dw=cv_w_dw,
        cv_b_dw=cv_b_dw, cv_g_ln=cv_g_ln, cv_b_ln=cv_b_ln, cv_w_pw=cv_w_pw, cv_b_pw=cv_b_pw))
    prompt = _trunk(x_prompt, p_prompt, w, None)
    sample_state = dict(ml_c=state_mlstm_C, ml_n=state_mlstm_n, ml_m=state_mlstm_m, pool=state_pool,
                        sb_k=cache_sb_k, sb_v=cache_sb_v, page_table=page_table, conv=state_conv,
                        start_pos=page_table.shape[1] * PAGE_SIZE)
    sample = _trunk(x_sample, p_sample, w, sample_state)
    return (prompt[0], sample[0]) + prompt[1:] + sample[1:]
```

```python
import functools

import jax
import jax.numpy as jnp
from jax import lax
from jax.experimental import pallas as pl
from jax.experimental.pallas import tpu as pltpu

F32 = jnp.float32
BF16 = jnp.bfloat16
HIGHEST = lax.Precision.HIGHEST

EPS = 1e-6
LOG2E = 1.4426950408889634
D = 1024
PLE_DIM = 256
D_FF = 2816
ML_HEADS, ML_DQK, ML_DV = 8, 64, 128
POOL_WINDOWS = (2, 4, 8, 16)
POOL_GW = D // len(POOL_WINDOWS)
POOL_STATE = 15
SB_HEADS, SB_DH = 16, 64
CONV_WIDTH = 31
CONV_STATE = CONV_WIDTH - 1
PAGE_SIZE = 128

V7X_VMEM_LIMIT = 56 * 1024 * 1024
LANES = 128
SUBLANES = 8

FF_CHUNK = 256
ML_CHUNK = 256
SB_BLOCK = 256
PAGES_PER_STEP = 4
POOL_HALO = 16
CONV_HALO = 32


def _row_tile(n):
    return 512 if n % 512 == 0 else n


def _params(*sem):
    return pltpu.CompilerParams(dimension_semantics=sem, vmem_limit_bytes=V7X_VMEM_LIMIT)


def _full(shape):
    return pl.BlockSpec(shape, lambda *_: (0,) * len(shape), pipeline_mode=pl.Buffered(1))


def _rows(tm, width):
    return pl.BlockSpec((tm, width), lambda i: (i, 0))


def _rms(x, g):
    return x * lax.rsqrt(jnp.mean(x * x, axis=-1, keepdims=True) + EPS) * g


def _bdot(a, b):
    return jnp.dot(a.astype(BF16), b.astype(BF16), preferred_element_type=F32)


def _bdot_nt(a, b):
    return lax.dot_general(a.astype(BF16), b.astype(BF16), (((1,), (1,)), ((), ())),
                           preferred_element_type=F32)


def _sigmoid(x):
    return 1.0 / (1.0 + jnp.exp(-x))


def _log_sigmoid(x):
    return jnp.minimum(x, 0.0) - jnp.log(1.0 + jnp.exp(-jnp.abs(x)))


def _ffn_ple_kernel(h_ref, p_ref, gf_ref, wgu_ref, wd_ref, gp_ref, wgate_ref, wple_ref, o_ref):
    h = h_ref[...]
    u = _rms(h, gf_ref[...]).astype(BF16)
    acc = jnp.zeros(h.shape, F32)
    for c in range(D_FF // FF_CHUNK):
        lo = c * FF_CHUNK
        gate = jnp.dot(u, wgu_ref[:, lo:lo + FF_CHUNK], preferred_element_type=F32)
        up = jnp.dot(u, wgu_ref[:, D_FF + lo:D_FF + lo + FF_CHUNK], preferred_element_type=F32)
        act = (gate * _sigmoid(gate) * up).astype(BF16)
        acc = acc + jnp.dot(act, wd_ref[lo:lo + FF_CHUNK, :], preferred_element_type=F32)
    h1 = h + acc
    u2 = _rms(h1, gp_ref[...])
    gate = _sigmoid(_bdot(u2, wgate_ref[...]))
    o_ref[...] = h1 + gate * _bdot(p_ref[...], wple_ref[...])


def _ffn_ple(h, p, g_ffn, w_gu, w_down, g_ple, w_gate, w_ple):
    n = h.shape[0]
    tm = _row_tile(n)
    return pl.pallas_call(
        _ffn_ple_kernel,
        out_shape=jax.ShapeDtypeStruct((n, D), F32),
        grid=(n // tm,),
        in_specs=[_rows(tm, D), _rows(tm, PLE_DIM), _full((1, D)), _full((D, 2 * D_FF)), _full((D_FF, D)),
                  _full((1, D)), _full((D, D)), _full((PLE_DIM, D))],
        out_specs=_rows(tm, D),
        compiler_params=_params("parallel"),
        name="ffn_ple",
    )(h, p, g_ffn, w_gu, w_down, g_ple, w_gate, w_ple)


def _out_proj_kernel(h_ref, y_ref, w_ref, o_ref):
    o_ref[...] = h_ref[...] + _bdot(y_ref[...], w_ref[...])


def _out_proj(h, y, w):
    n = h.shape[0]
    tm = _row_tile(n)
    return pl.pallas_call(
        _out_proj_kernel,
        out_shape=jax.ShapeDtypeStruct((n, D), F32),
        grid=(n // tm,),
        in_specs=[_rows(tm, D), _rows(tm, D), _full((D, D))],
        out_specs=_rows(tm, D),
        compiler_params=_params("parallel"),
        name="out_proj",
    )(h, y, w)


def _ml_in_kernel(h_ref, g_ref, wq_ref, wk_ref, wv_ref, wo_ref, wg_ref, bg_ref,
                  q_ref, k_ref, v_ref, og_ref, gates_ref):
    uf = _rms(h_ref[...], g_ref[...])
    u = uf.astype(BF16)
    q_ref[...] = jnp.dot(u, wq_ref[...], preferred_element_type=F32)
    k_ref[...] = jnp.dot(u, wk_ref[...], preferred_element_type=F32) * (ML_DQK ** -0.5)
    v_ref[...] = jnp.dot(u, wv_ref[...], preferred_element_type=F32)
    og_ref[...] = _sigmoid(jnp.dot(u, wo_ref[...], preferred_element_type=F32))
    gates = jnp.dot(uf, wg_ref[...], preferred_element_type=F32, precision=HIGHEST) + bg_ref[...]
    is_forget = lax.broadcasted_iota(jnp.int32, gates.shape, 1) >= ML_HEADS
    gates_ref[...] = jnp.where(is_forget, _log_sigmoid(gates), gates)


def _ml_in(h, g, wq, wk, wv, wo, wg, bg):
    n = h.shape[0]
    tm = _row_tile(n)
    hq = ML_HEADS * ML_DQK
    return pl.pallas_call(
        _ml_in_kernel,
        out_shape=[jax.ShapeDtypeStruct((n, hq), F32), jax.ShapeDtypeStruct((n, hq), F32),
                   jax.ShapeDtypeStruct((n, D), F32), jax.ShapeDtypeStruct((n, D), F32),
                   jax.ShapeDtypeStruct((n, 2 * ML_HEADS), F32)],
        grid=(n // tm,),
        in_specs=[_rows(tm, D), _full((1, D)), _full((D, hq)), _full((D, hq)), _full((D, D)), _full((D, D)),
                  _full((D, 2 * ML_HEADS)), _full((1, 2 * ML_HEADS))],
        out_specs=[_rows(tm, hq), _rows(tm, hq), _rows(tm, D), _rows(tm, D), _rows(tm, 2 * ML_HEADS)],
        compiler_params=_params("parallel"),
        name="ml_in",
    )(h, g, wq, wk, wv, wo, wg, bg)


def _head_norm_gate(hh, g, og):
    return og * (hh * lax.rsqrt(jnp.mean(hh * hh, axis=-1, keepdims=True) + EPS) * g)


def _ml_chunk_kernel(q_ref, k_ref, v_ref, og_ref, gc_ref, gr_ref, gout_ref, y_ref, c_ref, n_ref, m_ref):
    @pl.when(pl.program_id(1) == 0)
    def _():
        c_ref[...] = jnp.zeros(c_ref.shape, F32)
        n_ref[...] = jnp.zeros(n_ref.shape, F32)
        m_ref[...] = jnp.zeros(m_ref.shape, F32)

    lc = q_ref.shape[1]
    row = lax.broadcasted_iota(jnp.int32, (lc, lc), 0)
    col = lax.broadcasted_iota(jnp.int32, (lc, lc), 1)
    causal = col <= row
    gates_c = gc_ref[0]
    gates_r = gr_ref[0]
    b_c = jnp.dot(causal.astype(F32), gates_c[:, ML_HEADS:], preferred_element_type=F32, precision=HIGHEST)
    b_r = jnp.dot(gates_r[ML_HEADS:, :], (row <= col).astype(F32), preferred_element_type=F32,
                  precision=HIGHEST)
    for hd in range(ML_HEADS):
        qh = q_ref[0, :, hd * ML_DQK:(hd + 1) * ML_DQK]
        kh = k_ref[0, :, hd * ML_DQK:(hd + 1) * ML_DQK]
        vh = v_ref[0, :, hd * ML_DV:(hd + 1) * ML_DV]
        bc = b_c[:, hd:hd + 1]
        br = b_r[hd:hd + 1, :]
        li_c = gates_c[:, hd:hd + 1]
        li_r = gates_r[hd:hd + 1, :]
        m_prev = m_ref[0, hd:hd + 1, 0:1]
        c_prev = c_ref[0, hd]
        n_prev = n_ref[0, hd:hd + 1, :]

        dlog = jnp.where(causal, bc - br + li_r, -jnp.inf)
        inter = bc + m_prev
        m_t = jnp.maximum(inter, jnp.max(dlog, axis=1, keepdims=True))
        w_intra = jnp.exp(dlog - m_t)
        w_inter = jnp.exp(inter - m_t)
        s_mat = w_intra * _bdot_nt(qh, kh)
        num = w_inter * _bdot(qh, c_prev) + _bdot(s_mat, vh)
        den = w_inter * jnp.sum(qh * n_prev, axis=1, keepdims=True) + jnp.sum(s_mat, axis=1, keepdims=True)
        hh = num / jnp.maximum(jnp.abs(den), jnp.exp(-m_t))
        y_ref[0, :, hd * ML_DV:(hd + 1) * ML_DV] = _head_norm_gate(
            hh, gout_ref[:, hd * ML_DV:(hd + 1) * ML_DV], og_ref[0, :, hd * ML_DV:(hd + 1) * ML_DV])

        g_end = br[:, lc - 1:lc]
        dl_end = g_end - bc + li_c
        m_new = jnp.maximum(g_end + m_prev, jnp.max(dl_end, axis=0, keepdims=True))
        a_prev = jnp.exp(g_end + m_prev - m_new)
        kw = jnp.exp(dl_end - m_new) * kh
        c_ref[0, hd] = a_prev * c_prev + lax.dot_general(
            kw.astype(BF16), vh.astype(BF16), (((0,), (0,)), ((), ())), preferred_element_type=F32)
        n_ref[0, hd:hd + 1, :] = a_prev * n_prev + jnp.sum(kw, axis=0, keepdims=True)
        m_ref[0, hd:hd + 1, :] = jnp.broadcast_to(m_new, (1, LANES))


def _ml_chunk(q, k, v, og, gates, g_out):
    b, l, _ = q.shape
    lc = ML_CHUNK if l % ML_CHUNK == 0 else l
    gates_t = jnp.swapaxes(gates, 1, 2)
    hq = ML_HEADS * ML_DQK
    blk = lambda w: pl.BlockSpec((1, lc, w), lambda i, j: (i, j, 0))
    return pl.pallas_call(
        _ml_chunk_kernel,
        out_shape=[jax.ShapeDtypeStruct((b, l, D), F32),
                   jax.ShapeDtypeStruct((b, ML_HEADS, ML_DQK, ML_DV), F32),
                   jax.ShapeDtypeStruct((b, ML_HEADS, ML_DQK), F32),
                   jax.ShapeDtypeStruct((b, ML_HEADS, LANES), F32)],
        grid=(b, l // lc),
        in_specs=[blk(hq), blk(hq), blk(D), blk(D), blk(2 * ML_HEADS),
                  pl.BlockSpec((1, 2 * ML_HEADS, lc), lambda i, j: (i, 0, j)),
                  pl.BlockSpec((1, D), lambda i, j: (0, 0))],
        out_specs=[blk(D),
                   pl.BlockSpec((1, ML_HEADS, ML_DQK, ML_DV), lambda i, j: (i, 0, 0, 0)),
                   pl.BlockSpec((1, ML_HEADS, ML_DQK), lambda i, j: (i, 0, 0)),
                   pl.BlockSpec((1, ML_HEADS, LANES), lambda i, j: (i, 0, 0))],
        compiler_params=_params("parallel", "arbitrary"),
        name="ml_chunk",
    )(q, k, v, og, gates, gates_t, g_out)


def _ml_step_kernel(qt_ref, kt_ref, v_ref, og_ref, gates_ref, c_ref, nt_ref, m_ref, gout_ref,
                    y_ref, c_out, nt_out, m_out):
    lane = lax.broadcasted_iota(jnp.int32, (1, ML_HEADS), 1)
    m_row = jnp.zeros((1, ML_HEADS), F32)
    for hd in range(ML_HEADS):
        qc = qt_ref[0, :, hd:hd + 1]
        kc = kt_ref[0, :, hd:hd + 1]
        nc = nt_ref[0, :, hd:hd + 1]
        vh = v_ref[0, :, hd * ML_DV:(hd + 1) * ML_DV]
        c_prev = c_ref[0, hd]
        li = gates_ref[0, :, hd:hd + 1]
        lf = gates_ref[0, :, ML_HEADS + hd:ML_HEADS + hd + 1]
        m_prev = m_ref[0, :, hd:hd + 1]
        inter = lf + m_prev
        m_t = jnp.maximum(inter, li)
        w_intra = jnp.exp(li - m_t)
        w_inter = jnp.exp(inter - m_t)
        s = w_intra * jnp.sum(qc * kc, axis=0, keepdims=True)
        num = w_inter * jnp.sum(qc * c_prev, axis=0, keepdims=True) + s * vh
        den = w_inter * jnp.sum(qc * nc, axis=0, keepdims=True) + s
        hh = num / jnp.maximum(jnp.abs(den), jnp.exp(-m_t))
        y_ref[0, :, hd * ML_DV:(hd + 1) * ML_DV] = _head_norm_gate(
            hh, gout_ref[:, hd * ML_DV:(hd + 1) * ML_DV], og_ref[0, :, hd * ML_DV:(hd + 1) * ML_DV])
        c_out[0, hd] = w_inter * c_prev + w_intra * (kc * vh)
        nt_out[0, :, hd:hd + 1] = w_inter * nc + w_intra * kc
        m_row = jnp.where(lane == hd, m_t, m_row)
    m_out[0] = m_row


def _ml_step(q, k, v, og, gates, c0, n0, m0, g_out):
    b = q.shape[0]
    to_t = lambda a: jnp.swapaxes(a.reshape(b, ML_HEADS, ML_DQK), 1, 2)
    t_spec = pl.BlockSpec((1, ML_DQK, ML_HEADS), lambda i: (i, 0, 0))
    row = lambda w: pl.BlockSpec((1, 1, w), lambda i: (i, 0, 0))
    c_spec = pl.BlockSpec((1, ML_HEADS, ML_DQK, ML_DV), lambda i: (i, 0, 0, 0))
    y, c1, nt1, m1 = pl.pallas_call(
        _ml_step_kernel,
        out_shape=[jax.ShapeDtypeStruct((b, 1, D), F32),
                   jax.ShapeDtypeStruct((b, ML_HEADS, ML_DQK, ML_DV), F32),
                   jax.ShapeDtypeStruct((b, ML_DQK, ML_HEADS), F32),
                   jax.ShapeDtypeStruct((b, 1, ML_HEADS), F32)],
        grid=(b,),
        in_specs=[t_spec, t_spec, row(D), row(D), row(2 * ML_HEADS), c_spec, t_spec, row(ML_HEADS),
                  pl.BlockSpec((1, D), lambda i: (0, 0))],
        out_specs=[row(D), c_spec, t_spec, row(ML_HEADS)],
        compiler_params=_params("parallel"),
        name="ml_step",
    )(to_t(q), to_t(k), v.reshape(b, 1, D), og.reshape(b, 1, D), gates.reshape(b, 1, 2 * ML_HEADS),
      c0, jnp.swapaxes(n0, 1, 2), m0.reshape(b, 1, ML_HEADS), g_out)
    return y.reshape(b, D), c1, jnp.swapaxes(nt1, 1, 2), m1.reshape(b, ML_HEADS)


def _pool_project(h, pooled, w_ref, scale_ref, o_ref):
    for g in range(len(POOL_WINDOWS)):
        cols = slice(g * POOL_GW, (g + 1) * POOL_GW)
        o_ref[:, cols] = h[:, cols] + _bdot(pooled[g], w_ref[g]) * scale_ref[:, cols]


def _pool_kernel(h_ref, g_ref, w_ref, scale_ref, o_ref, tail_ref, ext_ref):
    tm = h_ref.shape[1]
    step = pl.program_id(1)

    @pl.when(step == 0)
    def _():
        ext_ref[0:POOL_HALO, :] = jnp.zeros((POOL_HALO, D), F32)

    @pl.when(step > 0)
    def _():
        ext_ref[0:POOL_HALO, :] = ext_ref[tm:tm + POOL_HALO, :]

    h = h_ref[0]
    u = _rms(h, g_ref[...])
    ext_ref[POOL_HALO:POOL_HALO + tm, :] = u
    tail_ref[0] = u[tm - POOL_HALO:, :]
    pos = step * tm + lax.broadcasted_iota(jnp.int32, (tm, 1), 0)
    pooled = []
    for g, w in enumerate(POOL_WINDOWS):
        cols = slice(g * POOL_GW, (g + 1) * POOL_GW)
        wsum = u[:, cols]
        for back in range(1, w):
            wsum = wsum + ext_ref[POOL_HALO - back:POOL_HALO - back + tm, cols]
        cnt = jnp.minimum(pos + 1, w).astype(F32)
        pooled.append(wsum / cnt - u[:, cols])
    _pool_project(h, pooled, w_ref, scale_ref, o_ref.at[0])


def _pool_prompt(h, g, w, scale):
    b, l, _ = h.shape
    tm = _row_tile(l)
    return pl.pallas_call(
        _pool_kernel,
        out_shape=[jax.ShapeDtypeStruct((b, l, D), F32), jax.ShapeDtypeStruct((b, POOL_HALO, D), F32)],
        grid=(b, l // tm),
        in_specs=[pl.BlockSpec((1, tm, D), lambda i, j: (i, j, 0)),
                  pl.BlockSpec((1, D), lambda i, j: (0, 0)),
                  pl.BlockSpec((len(POOL_WINDOWS), POOL_GW, POOL_GW), lambda i, j: (0, 0, 0)),
                  pl.BlockSpec((1, D), lambda i, j: (0, 0))],
        out_specs=[pl.BlockSpec((1, tm, D), lambda i, j: (i, j, 0)),
                   pl.BlockSpec((1, POOL_HALO, D), lambda i, j: (i, 0, 0))],
        scratch_shapes=[pltpu.VMEM((POOL_HALO + tm, D), F32)],
        compiler_params=_params("parallel", "arbitrary"),
        name="pool",
    )(h, g, w, scale)


def _pool_step_kernel(h_ref, pref_ref, g_ref, w_ref, scale_ref, o_ref, state_ref, *, start_pos):
    h = h_ref[...]
    u = _rms(h, g_ref[...])
    pooled = []
    for g, w in enumerate(POOL_WINDOWS):
        cols = slice(g * POOL_GW, (g + 1) * POOL_GW)
        wsum = u[:, cols]
        for back in range(1, w):
            wsum = wsum + pref_ref[POOL_STATE - back, :, cols]
        pooled.append(wsum / float(min(start_pos + 1, w)) - u[:, cols])
    _pool_project(h, pooled, w_ref, scale_ref, o_ref)
    state_ref[0:POOL_STATE - 1] = pref_ref[1:POOL_STATE]
    state_ref[POOL_STATE - 1] = u


def _pool_step(h, prefix, g, w, scale, start_pos):
    b = h.shape[0]
    out, state_t = pl.pallas_call(
        functools.partial(_pool_step_kernel, start_pos=start_pos),
        out_shape=[jax.ShapeDtypeStruct((b, D), F32), jax.ShapeDtypeStruct((POOL_STATE, b, D), F32)],
        compiler_params=pltpu.CompilerParams(vmem_limit_bytes=V7X_VMEM_LIMIT),
        name="pool_step",
    )(h, jnp.swapaxes(prefix, 0, 1), g, w, scale)
    return out, jnp.swapaxes(state_t, 0, 1)


def _pair_norm(x, g):
    first = lax.broadcasted_iota(jnp.int32, x.shape, 1) < SB_DH
    sq = x * x
    s_first = jnp.sum(jnp.where(first, sq, 0.0), axis=1, keepdims=True)
    s_second = jnp.sum(jnp.where(first, 0.0, sq), axis=1, keepdims=True)
    inv = jnp.where(first, lax.rsqrt(s_first * (1.0 / SB_DH) + EPS), lax.rsqrt(s_second * (1.0 / SB_DH) + EPS))
    return x * inv * g


def _sb_qkv_kernel(h_ref, g_ref, wq_ref, wk_ref, wv_ref, gq_ref, gk_ref, q_ref, k_ref, v_ref):
    u = _rms(h_ref[...], g_ref[...]).astype(BF16)
    v_ref[...] = jnp.dot(u, wv_ref[...], preferred_element_type=F32)
    for w_ref, gain_ref, o_ref in ((wq_ref, gq_ref, q_ref), (wk_ref, gk_ref, k_ref)):
        x = jnp.dot(u, w_ref[...], preferred_element_type=F32)
        for p in range(D // LANES):
            cols = slice(p * LANES, (p + 1) * LANES)
            o_ref[:, cols] = _pair_norm(x[:, cols], gain_ref[...])


def _sb_qkv(h, g, wq, wk, wv, gq, gk):
    n = h.shape[0]
    tm = _row_tile(n)
    return pl.pallas_call(
        _sb_qkv_kernel,
        out_shape=[jax.ShapeDtypeStruct((n, D), F32)] * 3,
        grid=(n // tm,),
        in_specs=[_rows(tm, D), _full((1, D)), _full((D, D)), _full((D, D)), _full((D, D)),
                  _full((1, LANES)), _full((1, LANES))],
        out_specs=[_rows(tm, D)] * 3,
        compiler_params=_params("parallel"),
        name="sb_qkv",
    )(h, g, wq, wk, wv, gq, gk)


def _sb_weights(z, valid):
    softplus = jnp.maximum(z, 0.0) + jnp.log(1.0 + jnp.exp(-jnp.abs(z)))
    log_rest = -softplus
    if valid is not None:
        log_rest = jnp.where(valid, log_rest, 0.0)
    return z - softplus, log_rest


def _neg_abs(x):
    bits = lax.bitcast_convert_type(x, jnp.uint32) | jnp.uint32(0x80000000)
    return lax.bitcast_convert_type(bits, F32)


SB_MASKED_LOGIT = -1e30


def _sb_scores(q2_ref, k_ref, start, valid, split_ref, lb_ref, first_ref, slot):
    tq, tk = lb_ref.shape[2], lb_ref.shape[3]
    kb = k_ref[0, pl.ds(start, tk), :]
    lane = lax.broadcasted_iota(jnp.int32, kb.shape, 1)
    logits = []
    for hh in range(2):
        own = (lane < SB_DH) if hh == 0 else (lane >= SB_DH)
        bias_lanes = (lane >= SB_DH) & (lane < SB_DH + 2) if hh == 0 else (lane < 2)
        kh = jnp.where(own, kb, jnp.where(bias_lanes, 1.0, 0.0)).astype(BF16)
        logits.append(lax.dot_general(q2_ref[hh], kh, (((1,), (1,)), ((), ())), preferred_element_type=F32))
    for hh in range(2):
        z = logits[hh]
        if valid is not None:
            z = jnp.where(valid, z, SB_MASKED_LOGIT)
        softplus = jnp.maximum(z, 0.0) + jnp.log(1.0 + jnp.exp2(_neg_abs(z))) * LOG2E
        lb_ref[slot, hh] = z - softplus
        hi = softplus.astype(BF16)
        split_ref[slot, hh * tq:(hh + 1) * tq, 0:tk] = hi
        split_ref[slot, hh * tq:(hh + 1) * tq, tk:2 * tk] = (softplus - hi.astype(F32)).astype(BF16)
        first_ref[slot, hh] = softplus[:, 0:1]


def _sb_accumulate(v_ref, start, ntri2_ref, split_ref, lb_ref, first_ref, slot, carry_ref, o_ref):
    tq, tk = lb_ref.shape[2], lb_ref.shape[3]
    vb = v_ref[0, pl.ds(start, tk), :]
    lane = lax.broadcasted_iota(jnp.int32, vb.shape, 1)
    inner = [jnp.dot(split_ref[slot, hh * tq:(hh + 1) * tq, :], ntri2_ref[...], preferred_element_type=F32)
             for hh in range(2)]
    weights = []
    for hh in range(2):
        carry = carry_ref[hh]
        weights.append(jnp.exp2(lb_ref[slot, hh] + inner[hh] + carry).astype(BF16))
        carry_ref[hh] = carry + inner[hh][:, 0:1] - first_ref[slot, hh]
    parts = []
    for hh in range(2):
        own = (lane < SB_DH) if hh == 0 else (lane >= SB_DH)
        parts.append(jnp.dot(weights[hh], jnp.where(own, vb, 0.0).astype(BF16), preferred_element_type=F32))
    o_ref[0] += parts[0] + parts[1]


def _sb_attn_kernel(bias_ref, q_ref, k_ref, v_ref, ntri2_ref, o_ref,
                    q2_ref, split_ref, lb_ref, first_ref, carry_ref):
    tq = q_ref.shape[1]
    pair = pl.program_id(1)
    qi = pl.program_id(2)
    row = lax.broadcasted_iota(jnp.int32, (tq, tq), 0)
    col = lax.broadcasted_iota(jnp.int32, (tq, tq), 1)
    q = q_ref[0] * (SB_DH ** -0.5 * LOG2E)
    lane = lax.broadcasted_iota(jnp.int32, q.shape, 1)
    for hh in range(2):
        bias_hi = bias_ref[0, 2 * pair + hh]
        bias_lo = bias_ref[1, 2 * pair + hh]
        own = (lane < SB_DH) if hh == 0 else (lane >= SB_DH)
        base = SB_DH if hh == 0 else 0
        extra = jnp.where(lane == base, bias_hi, jnp.where(lane == base + 1, bias_lo, 0.0))
        q2_ref[hh] = jnp.where(own, q, extra).astype(BF16)
    carry_ref[...] = jnp.zeros(carry_ref.shape, F32)
    o_ref[0] = jnp.zeros(o_ref.shape[1:], F32)

    scores = functools.partial(_sb_scores, q2_ref, k_ref, split_ref=split_ref, lb_ref=lb_ref, first_ref=first_ref)
    accumulate = functools.partial(_sb_accumulate, v_ref, ntri2_ref=ntri2_ref, split_ref=split_ref, lb_ref=lb_ref,
                                   first_ref=first_ref, carry_ref=carry_ref, o_ref=o_ref)
    block = lambda j: pl.multiple_of(j * tq, tq)

    scores(start=block(qi), valid=col < row, slot=0)

    def two_blocks(p, _):
        j = qi - 2 * p
        accumulate(start=block(j), slot=0)
        scores(start=block(j - 1), valid=None, slot=1)
        accumulate(start=block(j - 1), slot=1)
        scores(start=block(j - 2), valid=None, slot=0)
        return 0

    lax.fori_loop(0, qi // 2, two_blocks, 0)

    @pl.when(qi % 2 == 0)
    def _():
        accumulate(start=0, slot=0)

    @pl.when(qi % 2 == 1)
    def _():
        accumulate(start=tq, slot=0)
        scores(start=0, valid=None, slot=1)
        accumulate(start=0, slot=1)


def _later_key_matrix(n):
    r = jnp.arange(n)
    return (r[:, None] > r[None, :]).astype(F32)


def _sb_attn_prompt(q, k, v, b_logit):
    b, l, _ = q.shape
    tq = SB_BLOCK if l % SB_BLOCK == 0 else l
    qspec = pl.BlockSpec((1, tq, LANES), lambda i, p, j: (i, j, p))
    kvspec = pl.BlockSpec((1, l, LANES), lambda i, p, j: (i, 0, p))
    ntri = -_later_key_matrix(tq).astype(BF16)
    bias2 = b_logit * LOG2E
    bias_hi = bias2.astype(BF16).astype(F32)
    bias_split = jnp.stack([bias_hi, bias2 - bias_hi])
    return pl.pallas_call(
        _sb_attn_kernel,
        out_shape=jax.ShapeDtypeStruct((b, l, D), F32),
        grid=(b, SB_HEADS // 2, l // tq),
        in_specs=[pl.BlockSpec(memory_space=pltpu.SMEM), qspec, kvspec, kvspec,
                  pl.BlockSpec((2 * tq, tq), lambda i, p, j: (0, 0))],
        out_specs=qspec,
        scratch_shapes=[pltpu.VMEM((2, tq, LANES), BF16), pltpu.VMEM((2, 2 * tq, 2 * tq), BF16),
                        pltpu.VMEM((2, 2, tq, tq), F32), pltpu.VMEM((2, 2, tq, 1), F32),
                        pltpu.VMEM((2, tq, 1), F32)],
        compiler_params=_params("parallel", "parallel", "arbitrary"),
        name="sb_attn",
    )(bias_split, q, k, v, jnp.concatenate([ntri, ntri], axis=0))


def _sb_paged_kernel(pt_ref, qt_ref, bias_ref, *refs):
    k_refs, v_refs = refs[:PAGES_PER_STEP], refs[PAGES_PER_STEP:2 * PAGES_PER_STEP]
    tri_ref, o_ref, qb_ref, a_ref, acc_ref, carry_ref = refs[2 * PAGES_PER_STEP:]
    step = pl.program_id(1)

    @pl.when(step == 0)
    def _():
        qt = qt_ref[0] * (SB_DH ** -0.5)
        for h in range(SB_HEADS):
            qb_ref[h] = jnp.broadcast_to(qt[:, h:h + 1], (SB_DH, PAGE_SIZE))
        acc_ref[...] = jnp.zeros(acc_ref.shape, F32)
        carry_ref[...] = jnp.zeros(carry_ref.shape, F32)

    head = lax.broadcasted_iota(jnp.int32, (SB_HEADS, PAGE_SIZE), 0)
    for k_ref, v_ref in zip(k_refs, v_refs):
        z = jnp.zeros((SB_HEADS, PAGE_SIZE), F32)
        for h in range(SB_HEADS):
            z_h = jnp.sum(k_ref[0, h] * qb_ref[h], axis=0, keepdims=True)
            z = jnp.where(head == h, z_h, z)
        log_beta, log_rest = _sb_weights(z + bias_ref[...], None)
        hi = log_rest.astype(BF16)
        rem = log_rest - hi.astype(F32)
        mid = rem.astype(BF16)
        lo = (rem - mid.astype(F32)).astype(BF16)
        terms = jnp.dot(jnp.concatenate([hi, mid, lo], axis=0), tri_ref[...], preferred_element_type=F32)
        inner = terms[:SB_HEADS] + terms[SB_HEADS:2 * SB_HEADS] + terms[2 * SB_HEADS:]
        carry = carry_ref[:, 0:1]
        a_ref[...] = jnp.exp(log_beta + inner + carry)
        carry_ref[...] = jnp.broadcast_to(carry + jnp.sum(log_rest, axis=1, keepdims=True), carry_ref.shape)
        for h in range(SB_HEADS):
            weighted = v_ref[0, h] * a_ref[h:h + 1, :]
            acc_ref[h] += jnp.broadcast_to(jnp.sum(weighted, axis=1, keepdims=True), (SB_DH, LANES))

    @pl.when(step == pl.num_programs(1) - 1)
    def _():
        diag = (lax.broadcasted_iota(jnp.int32, (SB_DH, LANES), 0)
                == lax.broadcasted_iota(jnp.int32, (SB_DH, LANES), 1))
        for h in range(SB_HEADS):
            o_ref[0, h:h + 1, :] = jnp.sum(jnp.where(diag, acc_ref[h], 0.0), axis=0, keepdims=True)


def _sb_attn_paged(q, cache_k, cache_v, page_table, b_logit):
    b = q.shape[0]
    n_pages = page_table.shape[1]
    steps = n_pages // PAGES_PER_STEP
    kt = jnp.transpose(cache_k, (0, 2, 3, 1))
    vt = jnp.transpose(cache_v, (0, 2, 3, 1))
    qt = jnp.swapaxes(q.reshape(b, SB_HEADS, SB_DH), 1, 2)

    def page_spec(t):
        return pl.BlockSpec((1, SB_HEADS, SB_DH, PAGE_SIZE),
                            lambda i, j, pt: (pt[i * n_pages + n_pages - 1 - (j * PAGES_PER_STEP + t)], 0, 0, 0))

    pages = [page_spec(t) for t in range(PAGES_PER_STEP)]
    out = pl.pallas_call(
        _sb_paged_kernel,
        out_shape=jax.ShapeDtypeStruct((b, SB_HEADS, LANES), F32),
        grid_spec=pltpu.PrefetchScalarGridSpec(
            num_scalar_prefetch=1,
            grid=(b, steps),
            in_specs=[pl.BlockSpec((1, SB_DH, SB_HEADS), lambda i, j, pt: (i, 0, 0)),
                      pl.BlockSpec((SB_HEADS, 1), lambda i, j, pt: (0, 0))] + pages + pages
                     + [pl.BlockSpec((PAGE_SIZE, PAGE_SIZE), lambda i, j, pt: (0, 0))],
            out_specs=pl.BlockSpec((1, SB_HEADS, LANES), lambda i, j, pt: (i, 0, 0)),
            scratch_shapes=[pltpu.VMEM((SB_HEADS, SB_DH, PAGE_SIZE), F32), pltpu.VMEM((SB_HEADS, PAGE_SIZE), F32),
                            pltpu.VMEM((SB_HEADS, SB_DH, LANES), F32), pltpu.VMEM((SB_HEADS, LANES), F32)]),
        compiler_params=_params("parallel", "arbitrary"),
        name="sb_paged",
    )(page_table.reshape(-1), qt, b_logit.reshape(SB_HEADS, 1),
      *([kt] * PAGES_PER_STEP), *([vt] * PAGES_PER_STEP), _later_key_matrix(PAGE_SIZE).astype(BF16))
    return out[:, :, :SB_DH].reshape(b, D)


def _glu(h, g, w_ref, b_ref):
    a = _bdot(_rms(h, g), w_ref[...]) + b_ref[...]
    return a[:, :D] * _sigmoid(a[:, D:])


def _conv_tail(h, c, gln_ref, bln_ref, wpw_ref, bpw_ref):
    cc = c - jnp.mean(c, axis=-1, keepdims=True)
    y = cc * lax.rsqrt(jnp.mean(cc * cc, axis=-1, keepdims=True) + EPS) * gln_ref[...] + bln_ref[...]
    y = y * _sigmoid(y)
    return h + _bdot(y, wpw_ref[...]) + bpw_ref[...]


def _conv_kernel(h_ref, g_ref, wglu_ref, bglu_ref, wdw_ref, bdw_ref, gln_ref, bln_ref, wpw_ref, bpw_ref,
                 o_ref, tail_ref, ext_ref):
    tm = h_ref.shape[1]
    step = pl.program_id(1)

    @pl.when(step == 0)
    def _():
        ext_ref[0:CONV_HALO, :] = jnp.zeros((CONV_HALO, D), F32)

    @pl.when(step > 0)
    def _():
        ext_ref[0:CONV_HALO, :] = ext_ref[tm:tm + CONV_HALO, :]

    h = h_ref[0]
    glu = _glu(h, g_ref[...], wglu_ref, bglu_ref)
    ext_ref[CONV_HALO:CONV_HALO + tm, :] = glu
    tail_ref[0] = glu[tm - CONV_HALO:, :]
    c = jnp.broadcast_to(bdw_ref[...], (tm, D))
    for j in range(CONV_WIDTH):
        lo = CONV_HALO - CONV_STATE + j
        c = c + ext_ref[lo:lo + tm, :] * wdw_ref[j:j + 1, :]
    o_ref[0] = _conv_tail(h, c, gln_ref, bln_ref, wpw_ref, bpw_ref)


def _conv_prompt(h, g, w_glu, b_glu, w_dw, b_dw, g_ln, b_ln, w_pw, b_pw):
    b, l, _ = h.shape
    tm = _row_tile(l)
    const = lambda shape: pl.BlockSpec(shape, lambda i, j: (0,) * len(shape), pipeline_mode=pl.Buffered(1))
    return pl.pallas_call(
        _conv_kernel,
        out_shape=[jax.ShapeDtypeStruct((b, l, D), F32), jax.ShapeDtypeStruct((b, CONV_HALO, D), F32)],
        grid=(b, l // tm),
        in_specs=[pl.BlockSpec((1, tm, D), lambda i, j: (i, j, 0)),
                  const((1, D)), const((D, 2 * D)), const((1, 2 * D)), const((CONV_WIDTH, D)), const((1, D)),
                  const((1, D)), const((1, D)), const((D, D)), const((1, D))],
        out_specs=[pl.BlockSpec((1, tm, D), lambda i, j: (i, j, 0)),
                   pl.BlockSpec((1, CONV_HALO, D), lambda i, j: (i, 0, 0))],
        scratch_shapes=[pltpu.VMEM((CONV_HALO + tm, D), F32)],
        compiler_params=_params("parallel", "arbitrary"),
        name="conv",
    )(h, g, w_glu, b_glu, w_dw, b_dw, g_ln, b_ln, w_pw, b_pw)


def _conv_step_kernel(h_ref, pref_ref, g_ref, wglu_ref, bglu_ref, wdw_ref, bdw_ref, gln_ref, bln_ref,
                      wpw_ref, bpw_ref, o_ref, state_ref):
    h = h_ref[...]
    glu = _glu(h, g_ref[...], wglu_ref, bglu_ref)
    c = bdw_ref[...] + glu * wdw_ref[CONV_STATE:CONV_WIDTH, :]
    for j in range(CONV_STATE):
        c = c + pref_ref[j] * wdw_ref[j:j + 1, :]
    o_ref[...] = _conv_tail(h, c, gln_ref, bln_ref, wpw_ref, bpw_ref)
    state_ref[0:CONV_STATE - 1] = pref_ref[1:CONV_STATE]
    state_ref[CONV_STATE - 1] = glu


def _conv_step(h, prefix, g, w_glu, b_glu, w_dw, b_dw, g_ln, b_ln, w_pw, b_pw):
    b = h.shape[0]
    out, state_t = pl.pallas_call(
        _conv_step_kernel,
        out_shape=[jax.ShapeDtypeStruct((b, D), F32), jax.ShapeDtypeStruct((CONV_STATE, b, D), F32)],
        compiler_params=pltpu.CompilerParams(vmem_limit_bytes=V7X_VMEM_LIMIT),
        name="conv_step",
    )(h, jnp.swapaxes(prefix, 0, 1), g, w_glu, b_glu, w_dw, b_dw, g_ln, b_ln, w_pw, b_pw)
    return out, jnp.swapaxes(state_t, 0, 1)


def _prepare_weights(w):
    hq = ML_HEADS * ML_DQK
    row = lambda a: a.reshape(a.shape[0], 1, -1)
    ml_w_in = w['ml_w_in']
    sb_w = w['sb_w_qkv'].astype(BF16)
    tile_pair = lambda g: jnp.tile(g, (1, 2)).reshape(g.shape[0], 1, LANES)
    return dict(
        norm_mix=row(w['norm_mix']), norm_ffn=row(w['norm_ffn']), norm_ple=row(w['norm_ple']),
        w_ffn_gu=w['w_ffn_gu'].astype(BF16), w_ffn_down=w['w_ffn_down'].astype(BF16),
        w_ple=w['w_ple'].astype(BF16), w_ple_gate=w['w_ple_gate'].astype(BF16),
        ml_wq=ml_w_in[:, :, :hq].astype(BF16), ml_wk=ml_w_in[:, :, hq:2 * hq].astype(BF16),
        ml_wv=ml_w_in[:, :, 2 * hq:2 * hq + D].astype(BF16),
        ml_wo=ml_w_in[:, :, 2 * hq + D:2 * hq + 2 * D].astype(BF16),
        ml_wg=ml_w_in[:, :, 2 * hq + 2 * D:], ml_b_gate=row(w['ml_b_gate']),
        ml_g_out=w['ml_g_out'].reshape(-1, 1, D), ml_w_out=w['ml_w_out'].astype(BF16),
        pl_w=w['pl_w'].astype(BF16), pl_scale=row(w['pl_scale']),
        sb_wq=sb_w[:, :, :D], sb_wk=sb_w[:, :, D:2 * D], sb_wv=sb_w[:, :, 2 * D:],
        sb_g_q=tile_pair(w['sb_g_q']), sb_g_k=tile_pair(w['sb_g_k']), sb_b_logit=w['sb_b_logit'],
        sb_w_out=w['sb_w_out'].astype(BF16),
        cv_w_glu=w['cv_w_glu'].astype(BF16), cv_b_glu=row(w['cv_b_glu']), cv_w_dw=w['cv_w_dw'],
        cv_b_dw=row(w['cv_b_dw']), cv_g_ln=row(w['cv_g_ln']), cv_b_ln=row(w['cv_b_ln']),
        cv_w_pw=w['cv_w_pw'].astype(BF16), cv_b_pw=row(w['cv_b_pw']),
    )


def _trunk(x, p, w, state):
    b, l, _ = x.shape
    n = b * l
    h = x.reshape(n, D)
    depth = p.shape[0]
    new = {}
    for i in range(depth):
        kind, j = i % 4, i // 4
        g_mix = w['norm_mix'][i]
        if kind == 0:
            q, k, v, og, gates = _ml_in(h, g_mix, w['ml_wq'][j], w['ml_wk'][j], w['ml_wv'][j], w['ml_wo'][j],
                                        w['ml_wg'][j], w['ml_b_gate'][j])
            if state is None:
                y, c1, n1, m1 = _ml_chunk(q.reshape(b, l, -1), k.reshape(b, l, -1), v.reshape(b, l, D),
                                          og.reshape(b, l, D), gates.reshape(b, l, -1), w['ml_g_out'][j])
                m1 = m1[:, :, 0]
            else:
                y, c1, n1, m1 = _ml_step(q, k, v, og, gates, state['ml_c'][j], state['ml_n'][j],
                                         state['ml_m'][j], w['ml_g_out'][j])
            new.setdefault('ml_c', []).append(c1)
            new.setdefault('ml_n', []).append(n1)
            new.setdefault('ml_m', []).append(m1)
            h = _out_proj(h, y.reshape(n, D), w['ml_w_out'][j])
        elif kind == 1:
            if state is None:
                h3, tail = _pool_prompt(h.reshape(b, l, D), g_mix, w['pl_w'][j], w['pl_scale'][j])
                h, st = h3.reshape(n, D), tail[:, POOL_HALO - POOL_STATE:]
            else:
                h, st = _pool_step(h, state['pool'][j], g_mix, w['pl_w'][j], w['pl_scale'][j], state['start_pos'])
            new.setdefault('pool', []).append(st)
        elif kind == 2:
            q, k, v = _sb_qkv(h, g_mix, w['sb_wq'][j], w['sb_wk'][j], w['sb_wv'][j], w['sb_g_q'][j], w['sb_g_k'][j])
            if state is None:
                att = _sb_attn_prompt(q.reshape(b, l, D), k.reshape(b, l, D), v.reshape(b, l, D),
                                      w['sb_b_logit'][j]).reshape(n, D)
            else:
                att = _sb_attn_paged(q, state['sb_k'][j], state['sb_v'][j], state['page_table'], w['sb_b_logit'][j])
            new.setdefault('sb_k', []).append(k.reshape(b, l, SB_HEADS, SB_DH))
            new.setdefault('sb_v', []).append(v.reshape(b, l, SB_HEADS, SB_DH))
            h = _out_proj(h, att, w['sb_w_out'][j])
        else:
            cv = (g_mix, w['cv_w_glu'][j], w['cv_b_glu'][j], w['cv_w_dw'][j], w['cv_b_dw'][j], w['cv_g_ln'][j],
                  w['cv_b_ln'][j], w['cv_w_pw'][j], w['cv_b_pw'][j])
            if state is None:
                h3, tail = _conv_prompt(h.reshape(b, l, D), *cv)
                h, st = h3.reshape(n, D), tail[:, CONV_HALO - CONV_STATE:]
            else:
                h, st = _conv_step(h, state['conv'][j], *cv)
            new.setdefault('conv', []).append(st)
        h = _ffn_ple(h, p[i].reshape(n, PLE_DIM), w['norm_ffn'][i], w['w_ffn_gu'][i], w['w_ffn_down'][i],
                     w['norm_ple'][i], w['w_ple_gate'][i], w['w_ple'][i])
    stacked = tuple(jnp.stack(new[name]) for name in ('ml_c', 'ml_n', 'ml_m', 'pool', 'sb_k', 'sb_v', 'conv'))
    return (h.reshape(b, l, D),) + stacked


def kernel(x_prompt, x_sample, state_mlstm_C, state_mlstm_n, state_mlstm_m, state_pool, cache_sb_k, cache_sb_v,
           state_conv, page_table, p_prompt, p_sample, norm_mix, norm_ffn, w_ffn_gu, w_ffn_down, w_ple, norm_ple,
           w_ple_gate, ml_w_in, ml_b_gate, ml_g_out, ml_w_out, pl_w, pl_scale, sb_w_qkv, sb_g_q, sb_g_k, sb_b_logit,
           sb_w_out, cv_w_glu, cv_b_glu, cv_w_dw, cv_b_dw, cv_g_ln, cv_b_ln, cv_w_pw, cv_b_pw):
    w = _prepare_weights(dict(
        norm_mix=norm_mix, norm_ffn=norm_ffn, w_ffn_gu=w_ffn_gu, w_ffn_down=w_ffn_down, w_ple=w_ple,
        norm_ple=norm_ple, w_ple_gate=w_ple_gate, ml_w_in=ml_w_in, ml_b_gate=ml_b_gate, ml_g_out=ml_g_out,
        ml_w_out=ml_w_out, pl_w=pl_w, pl_scale=pl_scale, sb_w_qkv=sb_w_qkv, sb_g_q=sb_g_q, sb_g_k=sb_g_k,
        sb_b_logit=sb_b_logit, sb_w_out=sb_w_out, cv_w_glu=cv_w_glu, cv_b_glu=cv_b_glu, cv_w_dw=cv_w_dw,
        cv_b_dw=cv_b_dw, cv_g_ln=cv_g_ln, cv_b_ln=cv_b_ln, cv_w_pw=cv_w_pw, cv_b_pw=cv_b_pw))
    prompt = _trunk(x_prompt, p_prompt, w, None)
    sample_state = dict(ml_c=state_mlstm_C, ml_n=state_mlstm_n, ml_m=state_mlstm_m, pool=state_pool,
                        sb_k=cache_sb_k, sb_v=cache_sb_v, page_table=page_table, conv=state_conv,
                        start_pos=page_table.shape[1] * PAGE_SIZE)
    sample = _trunk(x_sample, p_sample, w, sample_state)
    return (prompt[0], sample[0]) + prompt[1:] + sample[1:]
```

```python
import functools

import jax
import jax.numpy as jnp
from jax import lax
from jax.experimental import pallas as pl
from jax.experimental.pallas import tpu as pltpu

F32 = jnp.float32
BF16 = jnp.bfloat16
HIGHEST = lax.Precision.HIGHEST

EPS = 1e-6
LOG2E = 1.4426950408889634
D = 1024
PLE_DIM = 256
D_FF = 2816
ML_HEADS, ML_DQK, ML_DV = 8, 64, 128
POOL_WINDOWS = (2, 4, 8, 16)
POOL_GW = D // len(POOL_WINDOWS)
POOL_STATE = 15
SB_HEADS, SB_DH = 16, 64
CONV_WIDTH = 31
CONV_STATE = CONV_WIDTH - 1
PAGE_SIZE = 128

V7X_VMEM_LIMIT = 56 * 1024 * 1024
LANES = 128
SUBLANES = 8

FF_CHUNK = 256
ML_CHUNK = 256
SB_BLOCK = 256
PAGES_PER_STEP = 8
POOL_HALO = 16
CONV_HALO = 32


def _row_tile(n):
    return 512 if n % 512 == 0 else n


def _params(*sem):
    return pltpu.CompilerParams(dimension_semantics=sem, vmem_limit_bytes=V7X_VMEM_LIMIT)


def _full(shape):
    return pl.BlockSpec(shape, lambda *_: (0,) * len(shape), pipeline_mode=pl.Buffered(1))


def _rows(tm, width):
    return pl.BlockSpec((tm, width), lambda i: (i, 0))


def _rms(x, g):
    return x * lax.rsqrt(jnp.mean(x * x, axis=-1, keepdims=True) + EPS) * g


def _bdot(a, b):
    return jnp.dot(a.astype(BF16), b.astype(BF16), preferred_element_type=F32)


def _bdot_nt(a, b):
    return lax.dot_general(a.astype(BF16), b.astype(BF16), (((1,), (1,)), ((), ())),
                           preferred_element_type=F32)


def _sigmoid(x):
    return 1.0 / (1.0 + jnp.exp(-x))


def _log_sigmoid(x):
    return jnp.minimum(x, 0.0) - jnp.log(1.0 + jnp.exp(-jnp.abs(x)))


def _ffn_ple_kernel(h_ref, p_ref, gf_ref, wgu_ref, wd_ref, gp_ref, wgate_ref, wple_ref, o_ref):
    h = h_ref[...]
    u = _rms(h, gf_ref[...]).astype(BF16)
    acc = jnp.zeros(h.shape, F32)
    for c in range(D_FF // FF_CHUNK):
        lo = c * FF_CHUNK
        gate = jnp.dot(u, wgu_ref[:, lo:lo + FF_CHUNK], preferred_element_type=F32)
        up = jnp.dot(u, wgu_ref[:, D_FF + lo:D_FF + lo + FF_CHUNK], preferred_element_type=F32)
        act = (gate * _sigmoid(gate) * up).astype(BF16)
        acc = acc + jnp.dot(act, wd_ref[lo:lo + FF_CHUNK, :], preferred_element_type=F32)
    h1 = h + acc
    u2 = _rms(h1, gp_ref[...])
    gate = _sigmoid(_bdot(u2, wgate_ref[...]))
    o_ref[...] = h1 + gate * _bdot(p_ref[...], wple_ref[...])


def _ffn_ple(h, p, g_ffn, w_gu, w_down, g_ple, w_gate, w_ple):
    n = h.shape[0]
    tm = _row_tile(n)
    return pl.pallas_call(
        _ffn_ple_kernel,
        out_shape=jax.ShapeDtypeStruct((n, D), F32),
        grid=(n // tm,),
        in_specs=[_rows(tm, D), _rows(tm, PLE_DIM), _full((1, D)), _full((D, 2 * D_FF)), _full((D_FF, D)),
                  _full((1, D)), _full((D, D)), _full((PLE_DIM, D))],
        out_specs=_rows(tm, D),
        compiler_params=_params("parallel"),
        name="ffn_ple",
    )(h, p, g_ffn, w_gu, w_down, g_ple, w_gate, w_ple)


def _out_proj_kernel(h_ref, y_ref, w_ref, o_ref):
    o_ref[...] = h_ref[...] + _bdot(y_ref[...], w_ref[...])


def _out_proj(h, y, w):
    n = h.shape[0]
    tm = _row_tile(n)
    return pl.pallas_call(
        _out_proj_kernel,
        out_shape=jax.ShapeDtypeStruct((n, D), F32),
        grid=(n // tm,),
        in_specs=[_rows(tm, D), _rows(tm, D), _full((D, D))],
        out_specs=_rows(tm, D),
        compiler_params=_params("parallel"),
        name="out_proj",
    )(h, y, w)


def _ml_in_kernel(h_ref, g_ref, wq_ref, wk_ref, wv_ref, wo_ref, wg_ref, bg_ref,
                  q_ref, k_ref, v_ref, og_ref, gates_ref):
    uf = _rms(h_ref[...], g_ref[...])
    u = uf.astype(BF16)
    q_ref[...] = jnp.dot(u, wq_ref[...], preferred_element_type=F32)
    k_ref[...] = jnp.dot(u, wk_ref[...], preferred_element_type=F32) * (ML_DQK ** -0.5)
    v_ref[...] = jnp.dot(u, wv_ref[...], preferred_element_type=F32)
    og_ref[...] = _sigmoid(jnp.dot(u, wo_ref[...], preferred_element_type=F32))
    gates = jnp.dot(uf, wg_ref[...], preferred_element_type=F32, precision=HIGHEST) + bg_ref[...]
    is_forget = lax.broadcasted_iota(jnp.int32, gates.shape, 1) >= ML_HEADS
    gates_ref[...] = jnp.where(is_forget, _log_sigmoid(gates), gates)


def _ml_in(h, g, wq, wk, wv, wo, wg, bg):
    n = h.shape[0]
    tm = _row_tile(n)
    hq = ML_HEADS * ML_DQK
    return pl.pallas_call(
        _ml_in_kernel,
        out_shape=[jax.ShapeDtypeStruct((n, hq), F32), jax.ShapeDtypeStruct((n, hq), F32),
                   jax.ShapeDtypeStruct((n, D), F32), jax.ShapeDtypeStruct((n, D), F32),
                   jax.ShapeDtypeStruct((n, 2 * ML_HEADS), F32)],
        grid=(n // tm,),
        in_specs=[_rows(tm, D), _full((1, D)), _full((D, hq)), _full((D, hq)), _full((D, D)), _full((D, D)),
                  _full((D, 2 * ML_HEADS)), _full((1, 2 * ML_HEADS))],
        out_specs=[_rows(tm, hq), _rows(tm, hq), _rows(tm, D), _rows(tm, D), _rows(tm, 2 * ML_HEADS)],
        compiler_params=_params("parallel"),
        name="ml_in",
    )(h, g, wq, wk, wv, wo, wg, bg)


def _head_norm_gate(hh, g, og):
    return og * (hh * lax.rsqrt(jnp.mean(hh * hh, axis=-1, keepdims=True) + EPS) * g)


def _ml_chunk_kernel(q_ref, k_ref, v_ref, og_ref, gc_ref, gr_ref, gout_ref, y_ref, c_ref, n_ref, m_ref):
    @pl.when(pl.program_id(1) == 0)
    def _():
        c_ref[...] = jnp.zeros(c_ref.shape, F32)
        n_ref[...] = jnp.zeros(n_ref.shape, F32)
        m_ref[...] = jnp.zeros(m_ref.shape, F32)

    lc = q_ref.shape[1]
    row = lax.broadcasted_iota(jnp.int32, (lc, lc), 0)
    col = lax.broadcasted_iota(jnp.int32, (lc, lc), 1)
    causal = col <= row
    gates_c = gc_ref[0]
    gates_r = gr_ref[0]
    b_c = jnp.dot(causal.astype(F32), gates_c[:, ML_HEADS:], preferred_element_type=F32, precision=HIGHEST)
    b_r = jnp.dot(gates_r[ML_HEADS:, :], (row <= col).astype(F32), preferred_element_type=F32,
                  precision=HIGHEST)
    for hd in range(ML_HEADS):
        qh = q_ref[0, :, hd * ML_DQK:(hd + 1) * ML_DQK]
        kh = k_ref[0, :, hd * ML_DQK:(hd + 1) * ML_DQK]
        vh = v_ref[0, :, hd * ML_DV:(hd + 1) * ML_DV]
        bc = b_c[:, hd:hd + 1]
        br = b_r[hd:hd + 1, :]
        li_c = gates_c[:, hd:hd + 1]
        li_r = gates_r[hd:hd + 1, :]
        m_prev = m_ref[0, hd:hd + 1, 0:1]
        c_prev = c_ref[0, hd]
        n_prev = n_ref[0, hd:hd + 1, :]

        dlog = jnp.where(causal, bc - br + li_r, -jnp.inf)
        inter = bc + m_prev
        m_t = jnp.maximum(inter, jnp.max(dlog, axis=1, keepdims=True))
        w_intra = jnp.exp(dlog - m_t)
        w_inter = jnp.exp(inter - m_t)
        s_mat = w_intra * _bdot_nt(qh, kh)
        num = w_inter * _bdot(qh, c_prev) + _bdot(s_mat, vh)
        den = w_inter * jnp.sum(qh * n_prev, axis=1, keepdims=True) + jnp.sum(s_mat, axis=1, keepdims=True)
        hh = num / jnp.maximum(jnp.abs(den), jnp.exp(-m_t))
        y_ref[0, :, hd * ML_DV:(hd + 1) * ML_DV] = _head_norm_gate(
            hh, gout_ref[:, hd * ML_DV:(hd + 1) * ML_DV], og_ref[0, :, hd * ML_DV:(hd + 1) * ML_DV])

        g_end = br[:, lc - 1:lc]
        dl_end = g_end - bc + li_c
        m_new = jnp.maximum(g_end + m_prev, jnp.max(dl_end, axis=0, keepdims=True))
        a_prev = jnp.exp(g_end + m_prev - m_new)
        kw = jnp.exp(dl_end - m_new) * kh
        c_ref[0, hd] = a_prev * c_prev + lax.dot_general(
            kw.astype(BF16), vh.astype(BF16), (((0,), (0,)), ((), ())), preferred_element_type=F32)
        n_ref[0, hd:hd + 1, :] = a_prev * n_prev + jnp.sum(kw, axis=0, keepdims=True)
        m_ref[0, hd:hd + 1, :] = jnp.broadcast_to(m_new, (1, LANES))


def _ml_chunk(q, k, v, og, gates, g_out):
    b, l, _ = q.shape
    lc = ML_CHUNK if l % ML_CHUNK == 0 else l
    gates_t = jnp.swapaxes(gates, 1, 2)
    hq = ML_HEADS * ML_DQK
    blk = lambda w: pl.BlockSpec((1, lc, w), lambda i, j: (i, j, 0))
    return pl.pallas_call(
        _ml_chunk_kernel,
        out_shape=[jax.ShapeDtypeStruct((b, l, D), F32),
                   jax.ShapeDtypeStruct((b, ML_HEADS, ML_DQK, ML_DV), F32),
                   jax.ShapeDtypeStruct((b, ML_HEADS, ML_DQK), F32),
                   jax.ShapeDtypeStruct((b, ML_HEADS, LANES), F32)],
        grid=(b, l // lc),
        in_specs=[blk(hq), blk(hq), blk(D), blk(D), blk(2 * ML_HEADS),
                  pl.BlockSpec((1, 2 * ML_HEADS, lc), lambda i, j: (i, 0, j)),
                  pl.BlockSpec((1, D), lambda i, j: (0, 0))],
        out_specs=[blk(D),
                   pl.BlockSpec((1, ML_HEADS, ML_DQK, ML_DV), lambda i, j: (i, 0, 0, 0)),
                   pl.BlockSpec((1, ML_HEADS, ML_DQK), lambda i, j: (i, 0, 0)),
                   pl.BlockSpec((1, ML_HEADS, LANES), lambda i, j: (i, 0, 0))],
        compiler_params=_params("parallel", "arbitrary"),
        name="ml_chunk",
    )(q, k, v, og, gates, gates_t, g_out)


def _ml_step_kernel(qt_ref, kt_ref, v_ref, og_ref, gates_ref, c_ref, nt_ref, m_ref, gout_ref,
                    y_ref, c_out, nt_out, m_out):
    lane = lax.broadcasted_iota(jnp.int32, (1, ML_HEADS), 1)
    m_row = jnp.zeros((1, ML_HEADS), F32)
    for hd in range(ML_HEADS):
        qc = qt_ref[0, :, hd:hd + 1]
        kc = kt_ref[0, :, hd:hd + 1]
        nc = nt_ref[0, :, hd:hd + 1]
        vh = v_ref[0, :, hd * ML_DV:(hd + 1) * ML_DV]
        c_prev = c_ref[0, hd]
        li = gates_ref[0, :, hd:hd + 1]
        lf = gates_ref[0, :, ML_HEADS + hd:ML_HEADS + hd + 1]
        m_prev = m_ref[0, :, hd:hd + 1]
        inter = lf + m_prev
        m_t = jnp.maximum(inter, li)
        w_intra = jnp.exp(li - m_t)
        w_inter = jnp.exp(inter - m_t)
        s = w_intra * jnp.sum(qc * kc, axis=0, keepdims=True)
        num = w_inter * jnp.sum(qc * c_prev, axis=0, keepdims=True) + s * vh
        den = w_inter * jnp.sum(qc * nc, axis=0, keepdims=True) + s
        hh = num / jnp.maximum(jnp.abs(den), jnp.exp(-m_t))
        y_ref[0, :, hd * ML_DV:(hd + 1) * ML_DV] = _head_norm_gate(
            hh, gout_ref[:, hd * ML_DV:(hd + 1) * ML_DV], og_ref[0, :, hd * ML_DV:(hd + 1) * ML_DV])
        c_out[0, hd] = w_inter * c_prev + w_intra * (kc * vh)
        nt_out[0, :, hd:hd + 1] = w_inter * nc + w_intra * kc
        m_row = jnp.where(lane == hd, m_t, m_row)
    m_out[0] = m_row


def _ml_step(q, k, v, og, gates, c0, n0, m0, g_out):
    b = q.shape[0]
    to_t = lambda a: jnp.swapaxes(a.reshape(b, ML_HEADS, ML_DQK), 1, 2)
    t_spec = pl.BlockSpec((1, ML_DQK, ML_HEADS), lambda i: (i, 0, 0))
    row = lambda w: pl.BlockSpec((1, 1, w), lambda i: (i, 0, 0))
    c_spec = pl.BlockSpec((1, ML_HEADS, ML_DQK, ML_DV), lambda i: (i, 0, 0, 0))
    y, c1, nt1, m1 = pl.pallas_call(
        _ml_step_kernel,
        out_shape=[jax.ShapeDtypeStruct((b, 1, D), F32),
                   jax.ShapeDtypeStruct((b, ML_HEADS, ML_DQK, ML_DV), F32),
                   jax.ShapeDtypeStruct((b, ML_DQK, ML_HEADS), F32),
                   jax.ShapeDtypeStruct((b, 1, ML_HEADS), F32)],
        grid=(b,),
        in_specs=[t_spec, t_spec, row(D), row(D), row(2 * ML_HEADS), c_spec, t_spec, row(ML_HEADS),
                  pl.BlockSpec((1, D), lambda i: (0, 0))],
        out_specs=[row(D), c_spec, t_spec, row(ML_HEADS)],
        compiler_params=_params("parallel"),
        name="ml_step",
    )(to_t(q), to_t(k), v.reshape(b, 1, D), og.reshape(b, 1, D), gates.reshape(b, 1, 2 * ML_HEADS),
      c0, jnp.swapaxes(n0, 1, 2), m0.reshape(b, 1, ML_HEADS), g_out)
    return y.reshape(b, D), c1, jnp.swapaxes(nt1, 1, 2), m1.reshape(b, ML_HEADS)


def _pool_project(h, pooled, w_ref, scale_ref, o_ref):
    for g in range(len(POOL_WINDOWS)):
        cols = slice(g * POOL_GW, (g + 1) * POOL_GW)
        o_ref[:, cols] = h[:, cols] + _bdot(pooled[g], w_ref[g]) * scale_ref[:, cols]


def _pool_kernel(h_ref, g_ref, w_ref, scale_ref, o_ref, tail_ref, ext_ref):
    tm = h_ref.shape[1]
    step = pl.program_id(1)

    @pl.when(step == 0)
    def _():
        ext_ref[0:POOL_HALO, :] = jnp.zeros((POOL_HALO, D), F32)

    @pl.when(step > 0)
    def _():
        ext_ref[0:POOL_HALO, :] = ext_ref[tm:tm + POOL_HALO, :]

    h = h_ref[0]
    u = _rms(h, g_ref[...])
    ext_ref[POOL_HALO:POOL_HALO + tm, :] = u
    tail_ref[0] = u[tm - POOL_HALO:, :]
    pos = step * tm + lax.broadcasted_iota(jnp.int32, (tm, 1), 0)
    pooled = []
    for g, w in enumerate(POOL_WINDOWS):
        cols = slice(g * POOL_GW, (g + 1) * POOL_GW)
        wsum = u[:, cols]
        for back in range(1, w):
            wsum = wsum + ext_ref[POOL_HALO - back:POOL_HALO - back + tm, cols]
        cnt = jnp.minimum(pos + 1, w).astype(F32)
        pooled.append(wsum / cnt - u[:, cols])
    _pool_project(h, pooled, w_ref, scale_ref, o_ref.at[0])


def _pool_prompt(h, g, w, scale):
    b, l, _ = h.shape
    tm = _row_tile(l)
    return pl.pallas_call(
        _pool_kernel,
        out_shape=[jax.ShapeDtypeStruct((b, l, D), F32), jax.ShapeDtypeStruct((b, POOL_HALO, D), F32)],
        grid=(b, l // tm),
        in_specs=[pl.BlockSpec((1, tm, D), lambda i, j: (i, j, 0)),
                  pl.BlockSpec((1, D), lambda i, j: (0, 0)),
                  pl.BlockSpec((len(POOL_WINDOWS), POOL_GW, POOL_GW), lambda i, j: (0, 0, 0)),
                  pl.BlockSpec((1, D), lambda i, j: (0, 0))],
        out_specs=[pl.BlockSpec((1, tm, D), lambda i, j: (i, j, 0)),
                   pl.BlockSpec((1, POOL_HALO, D), lambda i, j: (i, 0, 0))],
        scratch_shapes=[pltpu.VMEM((POOL_HALO + tm, D), F32)],
        compiler_params=_params("parallel", "arbitrary"),
        name="pool",
    )(h, g, w, scale)


def _pool_step_kernel(h_ref, pref_ref, g_ref, w_ref, scale_ref, o_ref, state_ref, *, start_pos):
    h = h_ref[...]
    u = _rms(h, g_ref[...])
    pooled = []
    for g, w in enumerate(POOL_WINDOWS):
        cols = slice(g * POOL_GW, (g + 1) * POOL_GW)
        wsum = u[:, cols]
        for back in range(1, w):
            wsum = wsum + pref_ref[POOL_STATE - back, :, cols]
        pooled.append(wsum / float(min(start_pos + 1, w)) - u[:, cols])
    _pool_project(h, pooled, w_ref, scale_ref, o_ref)
    state_ref[0:POOL_STATE - 1] = pref_ref[1:POOL_STATE]
    state_ref[POOL_STATE - 1] = u


def _pool_step(h, prefix, g, w, scale, start_pos):
    b = h.shape[0]
    out, state_t = pl.pallas_call(
        functools.partial(_pool_step_kernel, start_pos=start_pos),
        out_shape=[jax.ShapeDtypeStruct((b, D), F32), jax.ShapeDtypeStruct((POOL_STATE, b, D), F32)],
        compiler_params=pltpu.CompilerParams(vmem_limit_bytes=V7X_VMEM_LIMIT),
        name="pool_step",
    )(h, jnp.swapaxes(prefix, 0, 1), g, w, scale)
    return out, jnp.swapaxes(state_t, 0, 1)


def _pair_norm(x, g):
    first = lax.broadcasted_iota(jnp.int32, x.shape, 1) < SB_DH
    sq = x * x
    s_first = jnp.sum(jnp.where(first, sq, 0.0), axis=1, keepdims=True)
    s_second = jnp.sum(jnp.where(first, 0.0, sq), axis=1, keepdims=True)
    inv = jnp.where(first, lax.rsqrt(s_first * (1.0 / SB_DH) + EPS), lax.rsqrt(s_second * (1.0 / SB_DH) + EPS))
    return x * inv * g


def _sb_qkv_kernel(h_ref, g_ref, wq_ref, wk_ref, wv_ref, gq_ref, gk_ref, q_ref, k_ref, v_ref):
    u = _rms(h_ref[...], g_ref[...]).astype(BF16)
    v_ref[...] = jnp.dot(u, wv_ref[...], preferred_element_type=F32)
    for w_ref, gain_ref, o_ref in ((wq_ref, gq_ref, q_ref), (wk_ref, gk_ref, k_ref)):
        x = jnp.dot(u, w_ref[...], preferred_element_type=F32)
        for p in range(D // LANES):
            cols = slice(p * LANES, (p + 1) * LANES)
            o_ref[:, cols] = _pair_norm(x[:, cols], gain_ref[...])


def _sb_qkv(h, g, wq, wk, wv, gq, gk):
    n = h.shape[0]
    tm = _row_tile(n)
    return pl.pallas_call(
        _sb_qkv_kernel,
        out_shape=[jax.ShapeDtypeStruct((n, D), F32)] * 3,
        grid=(n // tm,),
        in_specs=[_rows(tm, D), _full((1, D)), _full((D, D)), _full((D, D)), _full((D, D)),
                  _full((1, LANES)), _full((1, LANES))],
        out_specs=[_rows(tm, D)] * 3,
        compiler_params=_params("parallel"),
        name="sb_qkv",
    )(h, g, wq, wk, wv, gq, gk)


def _sb_weights(z, valid):
    softplus = jnp.maximum(z, 0.0) + jnp.log(1.0 + jnp.exp(-jnp.abs(z)))
    log_rest = -softplus
    if valid is not None:
        log_rest = jnp.where(valid, log_rest, 0.0)
    return z - softplus, log_rest


def _neg_abs(x):
    bits = lax.bitcast_convert_type(x, jnp.uint32) | jnp.uint32(0x80000000)
    return lax.bitcast_convert_type(bits, F32)


SB_MASKED_LOGIT = -1e30


SB_STAGES = 5
SB_UNROLL = 4


def _sb_attn_kernel(blk_q_ref, blk_k_ref, bias_ref, q_ref, k_ref, v_ref, ntri_ref, mask_ref, o_ref,
                    q2_ref, z_ref, sp16_ref, lba_ref, firsta_ref, inner_ref, lbb_ref, firstb_ref, a_ref,
                    carry_ref):
    n_pairs = blk_q_ref.shape[0]
    tq = z_ref.shape[2]
    pair = pl.program_id(1)
    lane = lax.broadcasted_iota(jnp.int32, (tq, LANES), 1)
    own = [lane < SB_DH, lane >= SB_DH]

    def prepare_queries(i, _):
        q = q_ref[0, pl.ds(pl.multiple_of(i * tq, tq), tq), :] * (SB_DH ** -0.5 * LOG2E)
        for hh in range(2):
            base = SB_DH if hh == 0 else 0
            extra = jnp.where(lane == base, bias_ref[0, 2 * pair + hh],
                              jnp.where(lane == base + 1, bias_ref[1, 2 * pair + hh], 0.0))
            q2_ref[i, hh] = jnp.where(own[hh], q, extra).astype(BF16)
        return 0

    lax.fori_loop(0, q_ref.shape[1] // tq, prepare_queries, 0)
    for ref in (z_ref, sp16_ref, lba_ref, firsta_ref, inner_ref, lbb_ref, firstb_ref, a_ref, carry_ref):
        ref[...] = jnp.zeros(ref.shape, ref.dtype)
    o_ref[...] = jnp.zeros(o_ref.shape, F32)

    def pair_at(t, stage):
        idx = t - stage
        live = (idx >= 0) & (idx < n_pairs)
        idx = jnp.clip(idx, 0, n_pairs - 1)
        return blk_q_ref[idx], blk_k_ref[idx], live

    def rows(block):
        return pl.ds(pl.multiple_of(block * tq, tq), tq)

    def logits(t, slot):
        qi, kj, _ = pair_at(t, 0)
        kb = k_ref[0, rows(kj), :]
        for hh in range(2):
            bias_lanes = (lane >= SB_DH) & (lane < SB_DH + 2) if hh == 0 else (lane < 2)
            kh = jnp.where(own[hh], kb, jnp.where(bias_lanes, 1.0, 0.0)).astype(BF16)
            z_ref[slot, hh] = lax.dot_general(q2_ref[qi, hh], kh, (((1,), (1,)), ((), ())),
                                              preferred_element_type=F32)

    def softplus(t, slot):
        qi, kj, _ = pair_at(t, 1)
        mask = mask_ref[(qi == kj).astype(jnp.int32)]
        for hh in range(2):
            z = z_ref[slot, hh] + mask
            sp = jnp.maximum(z, 0.0) + jnp.log(1.0 + jnp.exp2(_neg_abs(z))) * LOG2E
            lba_ref[slot, hh] = z - sp
            sp16_ref[slot, hh] = sp.astype(BF16)
            firsta_ref[slot, hh] = sp[:, 0:1]

    def later_sum(t, slot):
        for hh in range(2):
            inner_ref[slot, hh] = jnp.dot(sp16_ref[slot, hh], ntri_ref[...], preferred_element_type=F32)
            lbb_ref[slot, hh] = lba_ref[slot, hh]
            firstb_ref[slot, hh] = firsta_ref[slot, hh]

    def weights(t, slot):
        qi, _, live = pair_at(t, 3)
        for hh in range(2):
            carry = carry_ref[qi, hh]
            inner = inner_ref[slot, hh]
            a_ref[slot, hh] = jnp.exp2(lbb_ref[slot, hh] + inner + carry).astype(BF16)
            carry_ref[qi, hh] = jnp.where(live, carry + inner[:, 0:1] - firstb_ref[slot, hh], carry)

    def values(t, slot):
        qi, kj, live = pair_at(t, 4)
        vb = v_ref[0, rows(kj), :]
        parts = [jnp.dot(a_ref[slot, hh], jnp.where(own[hh], vb, 0.0).astype(BF16), preferred_element_type=F32)
                 for hh in range(2)]
        o_ref[0, rows(qi), :] += jnp.where(live, parts[0] + parts[1], 0.0)

    def steps(p, _):
        for sub in range(SB_UNROLL):
            t = SB_UNROLL * p + sub
            even = sub % 2
            logits(t, even)
            later_sum(t, even)
            values(t, even)
            softplus(t, 1 - even)
            weights(t, 1 - even)
        return 0

    lax.fori_loop(0, pl.cdiv(n_pairs + SB_STAGES - 1, SB_UNROLL), steps, 0)


def _later_key_matrix(n):
    r = jnp.arange(n)
    return (r[:, None] > r[None, :]).astype(F32)


def _sb_attn_prompt(q, k, v, b_logit):
    b, l, _ = q.shape
    tq = SB_BLOCK if l % SB_BLOCK == 0 else l
    n_q = l // tq
    pairs = [(i, j) for i in range(n_q) for j in range(i, -1, -1)]
    blk_q = jnp.asarray([i for i, _ in pairs], jnp.int32)
    blk_k = jnp.asarray([j for _, j in pairs], jnp.int32)
    seq = pl.BlockSpec((1, l, LANES), lambda i, p, *_: (i, 0, p))
    ntri = -_later_key_matrix(tq).astype(BF16)
    r = jnp.arange(tq)
    mask = jnp.stack([jnp.zeros((tq, tq), F32), jnp.where(r[None, :] < r[:, None], 0.0, SB_MASKED_LOGIT)])
    bias2 = b_logit * LOG2E
    bias_hi = bias2.astype(BF16).astype(F32)
    bias_split = jnp.stack([bias_hi, bias2 - bias_hi])
    slots = lambda *shape: (2, 2) + shape
    return pl.pallas_call(
        _sb_attn_kernel,
        out_shape=jax.ShapeDtypeStruct((b, l, D), F32),
        grid_spec=pltpu.PrefetchScalarGridSpec(
            num_scalar_prefetch=2,
            grid=(b, SB_HEADS // 2),
            in_specs=[pl.BlockSpec(memory_space=pltpu.SMEM), seq, seq, seq,
                      pl.BlockSpec((tq, tq), lambda i, p, *_: (0, 0)),
                      pl.BlockSpec((2, tq, tq), lambda i, p, *_: (0, 0, 0))],
            out_specs=seq,
            scratch_shapes=[pltpu.VMEM((n_q, 2, tq, LANES), BF16),
                            pltpu.VMEM(slots(tq, tq), F32),
                            pltpu.VMEM(slots(tq, tq), BF16),
                            pltpu.VMEM(slots(tq, tq), F32),
                            pltpu.VMEM(slots(tq, 1), F32),
                            pltpu.VMEM(slots(tq, tq), F32),
                            pltpu.VMEM(slots(tq, tq), F32),
                            pltpu.VMEM(slots(tq, 1), F32),
                            pltpu.VMEM(slots(tq, tq), BF16),
                            pltpu.VMEM((n_q, 2, tq, 1), F32)]),
        compiler_params=_params("parallel", "parallel"),
        name="sb_attn",
    )(blk_q, blk_k, bias_split, q, k, v, ntri, mask)


def _sb_paged_kernel(pt_ref, qt_ref, bias_ref, *refs):
    k_refs, v_refs = refs[:PAGES_PER_STEP], refs[PAGES_PER_STEP:2 * PAGES_PER_STEP]
    tri_ref, o_ref, qb_ref, a_ref, acc_ref, carry_ref = refs[2 * PAGES_PER_STEP:]
    step = pl.program_id(1)

    @pl.when(step == 0)
    def _():
        qt = qt_ref[0] * (SB_DH ** -0.5)
        for h in range(SB_HEADS):
            qb_ref[h] = jnp.broadcast_to(qt[:, h:h + 1], (SB_DH, PAGE_SIZE))
        acc_ref[...] = jnp.zeros(acc_ref.shape, F32)
        carry_ref[...] = jnp.zeros(carry_ref.shape, F32)

    head = lax.broadcasted_iota(jnp.int32, (SB_HEADS, PAGE_SIZE), 0)
    for k_ref, v_ref in zip(k_refs, v_refs):
        z = jnp.zeros((SB_HEADS, PAGE_SIZE), F32)
        for h in range(SB_HEADS):
            z_h = jnp.sum(k_ref[0, h] * qb_ref[h], axis=0, keepdims=True)
            z = jnp.where(head == h, z_h, z)
        log_beta, log_rest = _sb_weights(z + bias_ref[...], None)
        hi = log_rest.astype(BF16)
        rem = log_rest - hi.astype(F32)
        mid = rem.astype(BF16)
        lo = (rem - mid.astype(F32)).astype(BF16)
        terms = jnp.dot(jnp.concatenate([hi, mid, lo], axis=0), tri_ref[...], preferred_element_type=F32)
        inner = terms[:SB_HEADS] + terms[SB_HEADS:2 * SB_HEADS] + terms[2 * SB_HEADS:]
        carry = carry_ref[:, 0:1]
        a_ref[...] = jnp.exp(log_beta + inner + carry)
        carry_ref[...] = jnp.broadcast_to(carry + jnp.sum(log_rest, axis=1, keepdims=True), carry_ref.shape)
        for h in range(SB_HEADS):
            weighted = v_ref[0, h] * a_ref[h:h + 1, :]
            acc_ref[h] += jnp.broadcast_to(jnp.sum(weighted, axis=1, keepdims=True), (SB_DH, LANES))

    @pl.when(step == pl.num_programs(1) - 1)
    def _():
        diag = (lax.broadcasted_iota(jnp.int32, (SB_DH, LANES), 0)
                == lax.broadcasted_iota(jnp.int32, (SB_DH, LANES), 1))
        for h in range(SB_HEADS):
            o_ref[0, h:h + 1, :] = jnp.sum(jnp.where(diag, acc_ref[h], 0.0), axis=0, keepdims=True)


def _sb_attn_paged(q, cache_k, cache_v, page_table, b_logit):
    b = q.shape[0]
    n_pages = page_table.shape[1]
    steps = n_pages // PAGES_PER_STEP
    kt = jnp.transpose(cache_k, (0, 2, 3, 1))
    vt = jnp.transpose(cache_v, (0, 2, 3, 1))
    qt = jnp.swapaxes(q.reshape(b, SB_HEADS, SB_DH), 1, 2)

    def page_spec(t):
        return pl.BlockSpec((1, SB_HEADS, SB_DH, PAGE_SIZE),
                            lambda i, j, pt: (pt[i * n_pages + n_pages - 1 - (j * PAGES_PER_STEP + t)], 0, 0, 0))

    pages = [page_spec(t) for t in range(PAGES_PER_STEP)]
    out = pl.pallas_call(
        _sb_paged_kernel,
        out_shape=jax.ShapeDtypeStruct((b, SB_HEADS, LANES), F32),
        grid_spec=pltpu.PrefetchScalarGridSpec(
            num_scalar_prefetch=1,
            grid=(b, steps),
            in_specs=[pl.BlockSpec((1, SB_DH, SB_HEADS), lambda i, j, pt: (i, 0, 0)),
                      pl.BlockSpec((SB_HEADS, 1), lambda i, j, pt: (0, 0))] + pages + pages
                     + [pl.BlockSpec((PAGE_SIZE, PAGE_SIZE), lambda i, j, pt: (0, 0))],
            out_specs=pl.BlockSpec((1, SB_HEADS, LANES), lambda i, j, pt: (i, 0, 0)),
            scratch_shapes=[pltpu.VMEM((SB_HEADS, SB_DH, PAGE_SIZE), F32), pltpu.VMEM((SB_HEADS, PAGE_SIZE), F32),
                            pltpu.VMEM((SB_HEADS, SB_DH, LANES), F32), pltpu.VMEM((SB_HEADS, LANES), F32)]),
        compiler_params=_params("parallel", "arbitrary"),
        name="sb_paged",
    )(page_table.reshape(-1), qt, b_logit.reshape(SB_HEADS, 1),
      *([kt] * PAGES_PER_STEP), *([vt] * PAGES_PER_STEP), _later_key_matrix(PAGE_SIZE).astype(BF16))
    return out[:, :, :SB_DH].reshape(b, D)


def _glu(h, g, w_ref, b_ref):
    a = _bdot(_rms(h, g), w_ref[...]) + b_ref[...]
    return a[:, :D] * _sigmoid(a[:, D:])


def _conv_tail(h, c, gln_ref, bln_ref, wpw_ref, bpw_ref):
    cc = c - jnp.mean(c, axis=-1, keepdims=True)
    y = cc * lax.rsqrt(jnp.mean(cc * cc, axis=-1, keepdims=True) + EPS) * gln_ref[...] + bln_ref[...]
    y = y * _sigmoid(y)
    return h + _bdot(y, wpw_ref[...]) + bpw_ref[...]


def _conv_kernel(h_ref, g_ref, wglu_ref, bglu_ref, wdw_ref, bdw_ref, gln_ref, bln_ref, wpw_ref, bpw_ref,
                 o_ref, tail_ref, ext_ref, win_ref):
    tm = h_ref.shape[1]
    step = pl.program_id(1)

    @pl.when(step == 0)
    def _():
        ext_ref[0:CONV_HALO, :] = jnp.zeros((CONV_HALO, D), F32)

    @pl.when(step > 0)
    def _():
        ext_ref[0:CONV_HALO, :] = ext_ref[tm:tm + CONV_HALO, :]

    h = h_ref[0]
    glu = _glu(h, g_ref[...], wglu_ref, bglu_ref)
    ext_ref[CONV_HALO:CONV_HALO + tm, :] = glu
    tail_ref[0] = glu[tm - CONV_HALO:, :]
    c = jnp.broadcast_to(bdw_ref[...], (tm, D))
    first = CONV_HALO - CONV_STATE
    for shift in range(SUBLANES):
        last = max(lo for lo in range(shift, CONV_HALO + 1, SUBLANES))
        if shift == 0:
            window_ref = ext_ref
        else:
            window_ref = win_ref
            win_ref[0:last - shift + tm, :] = ext_ref[shift:last + tm, :]
        for lo in range(shift, last + 1, SUBLANES):
            j = lo - first
            if 0 <= j < CONV_WIDTH:
                c = c + window_ref[lo - shift:lo - shift + tm, :] * wdw_ref[j:j + 1, :]
    o_ref[0] = _conv_tail(h, c, gln_ref, bln_ref, wpw_ref, bpw_ref)


def _conv_prompt(h, g, w_glu, b_glu, w_dw, b_dw, g_ln, b_ln, w_pw, b_pw):
    b, l, _ = h.shape
    tm = _row_tile(l)
    const = lambda shape: pl.BlockSpec(shape, lambda i, j: (0,) * len(shape), pipeline_mode=pl.Buffered(1))
    return pl.pallas_call(
        _conv_kernel,
        out_shape=[jax.ShapeDtypeStruct((b, l, D), F32), jax.ShapeDtypeStruct((b, CONV_HALO, D), F32)],
        grid=(b, l // tm),
        in_specs=[pl.BlockSpec((1, tm, D), lambda i, j: (i, j, 0)),
                  const((1, D)), const((D, 2 * D)), const((1, 2 * D)), const((CONV_WIDTH, D)), const((1, D)),
                  const((1, D)), const((1, D)), const((D, D)), const((1, D))],
        out_specs=[pl.BlockSpec((1, tm, D), lambda i, j: (i, j, 0)),
                   pl.BlockSpec((1, CONV_HALO, D), lambda i, j: (i, 0, 0))],
        scratch_shapes=[pltpu.VMEM((CONV_HALO + tm, D), F32), pltpu.VMEM((CONV_HALO + tm, D), F32)],
        compiler_params=_params("parallel", "arbitrary"),
        name="conv",
    )(h, g, w_glu, b_glu, w_dw, b_dw, g_ln, b_ln, w_pw, b_pw)


def _conv_step_kernel(h_ref, pref_ref, g_ref, wglu_ref, bglu_ref, wdw_ref, bdw_ref, gln_ref, bln_ref,
                      wpw_ref, bpw_ref, o_ref, state_ref):
    h = h_ref[...]
    glu = _glu(h, g_ref[...], wglu_ref, bglu_ref)
    c = bdw_ref[...] + glu * wdw_ref[CONV_STATE:CONV_WIDTH, :]
    for j in range(CONV_STATE):
        c = c + pref_ref[j] * wdw_ref[j:j + 1, :]
    o_ref[...] = _conv_tail(h, c, gln_ref, bln_ref, wpw_ref, bpw_ref)
    state_ref[0:CONV_STATE - 1] = pref_ref[1:CONV_STATE]
    state_ref[CONV_STATE - 1] = glu


def _conv_step(h, prefix, g, w_glu, b_glu, w_dw, b_dw, g_ln, b_ln, w_pw, b_pw):
    b = h.shape[0]
    out, state_t = pl.pallas_call(
        _conv_step_kernel,
        out_shape=[jax.ShapeDtypeStruct((b, D), F32), jax.ShapeDtypeStruct((CONV_STATE, b, D), F32)],
        compiler_params=pltpu.CompilerParams(vmem_limit_bytes=V7X_VMEM_LIMIT),
        name="conv_step",
    )(h, jnp.swapaxes(prefix, 0, 1), g, w_glu, b_glu, w_dw, b_dw, g_ln, b_ln, w_pw, b_pw)
    return out, jnp.swapaxes(state_t, 0, 1)


def _prepare_weights(w):
    hq = ML_HEADS * ML_DQK
    row = lambda a: a.reshape(a.shape[0], 1, -1)
    ml_w_in = w['ml_w_in']
    sb_w = w['sb_w_qkv'].astype(BF16)
    tile_pair = lambda g: jnp.tile(g, (1, 2)).reshape(g.shape[0], 1, LANES)
    return dict(
        norm_mix=row(w['norm_mix']), norm_ffn=row(w['norm_ffn']), norm_ple=row(w['norm_ple']),
        w_ffn_gu=w['w_ffn_gu'].astype(BF16), w_ffn_down=w['w_ffn_down'].astype(BF16),
        w_ple=w['w_ple'].astype(BF16), w_ple_gate=w['w_ple_gate'].astype(BF16),
        ml_wq=ml_w_in[:, :, :hq].astype(BF16), ml_wk=ml_w_in[:, :, hq:2 * hq].astype(BF16),
        ml_wv=ml_w_in[:, :, 2 * hq:2 * hq + D].astype(BF16),
        ml_wo=ml_w_in[:, :, 2 * hq + D:2 * hq + 2 * D].astype(BF16),
        ml_wg=ml_w_in[:, :, 2 * hq + 2 * D:], ml_b_gate=row(w['ml_b_gate']),
        ml_g_out=w['ml_g_out'].reshape(-1, 1, D), ml_w_out=w['ml_w_out'].astype(BF16),
        pl_w=w['pl_w'].astype(BF16), pl_scale=row(w['pl_scale']),
        sb_wq=sb_w[:, :, :D], sb_wk=sb_w[:, :, D:2 * D], sb_wv=sb_w[:, :, 2 * D:],
        sb_g_q=tile_pair(w['sb_g_q']), sb_g_k=tile_pair(w['sb_g_k']), sb_b_logit=w['sb_b_logit'],
        sb_w_out=w['sb_w_out'].astype(BF16),
        cv_w_glu=w['cv_w_glu'].astype(BF16), cv_b_glu=row(w['cv_b_glu']), cv_w_dw=w['cv_w_dw'],
        cv_b_dw=row(w['cv_b_dw']), cv_g_ln=row(w['cv_g_ln']), cv_b_ln=row(w['cv_b_ln']),
        cv_w_pw=w['cv_w_pw'].astype(BF16), cv_b_pw=row(w['cv_b_pw']),
    )


def _trunk(x, p, w, state):
    b, l, _ = x.shape
    n = b * l
    h = x.reshape(n, D)
    depth = p.shape[0]
    new = {}
    for i in range(depth):
        kind, j = i % 4, i // 4
        g_mix = w['norm_mix'][i]
        if kind == 0:
            q, k, v, og, gates = _ml_in(h, g_mix, w['ml_wq'][j], w['ml_wk'][j], w['ml_wv'][j], w['ml_wo'][j],
                                        w['ml_wg'][j], w['ml_b_gate'][j])
            if state is None:
                y, c1, n1, m1 = _ml_chunk(q.reshape(b, l, -1), k.reshape(b, l, -1), v.reshape(b, l, D),
                                          og.reshape(b, l, D), gates.reshape(b, l, -1), w['ml_g_out'][j])
                m1 = m1[:, :, 0]
            else:
                y, c1, n1, m1 = _ml_step(q, k, v, og, gates, state['ml_c'][j], state['ml_n'][j],
                                         state['ml_m'][j], w['ml_g_out'][j])
            new.setdefault('ml_c', []).append(c1)
            new.setdefault('ml_n', []).append(n1)
            new.setdefault('ml_m', []).append(m1)
            h = _out_proj(h, y.reshape(n, D), w['ml_w_out'][j])
        elif kind == 1:
            if state is None:
                h3, tail = _pool_prompt(h.reshape(b, l, D), g_mix, w['pl_w'][j], w['pl_scale'][j])
                h, st = h3.reshape(n, D), tail[:, POOL_HALO - POOL_STATE:]
            else:
                h, st = _pool_step(h, state['pool'][j], g_mix, w['pl_w'][j], w['pl_scale'][j], state['start_pos'])
            new.setdefault('pool', []).append(st)
        elif kind == 2:
            q, k, v = _sb_qkv(h, g_mix, w['sb_wq'][j], w['sb_wk'][j], w['sb_wv'][j], w['sb_g_q'][j], w['sb_g_k'][j])
            if state is None:
                att = _sb_attn_prompt(q.reshape(b, l, D), k.reshape(b, l, D), v.reshape(b, l, D),
                                      w['sb_b_logit'][j]).reshape(n, D)
            else:
                att = _sb_attn_paged(q, state['sb_k'][j], state['sb_v'][j], state['page_table'], w['sb_b_logit'][j])
            new.setdefault('sb_k', []).append(k.reshape(b, l, SB_HEADS, SB_DH))
            new.setdefault('sb_v', []).append(v.reshape(b, l, SB_HEADS, SB_DH))
            h = _out_proj(h, att, w['sb_w_out'][j])
        else:
            cv = (g_mix, w['cv_w_glu'][j], w['cv_b_glu'][j], w['cv_w_dw'][j], w['cv_b_dw'][j], w['cv_g_ln'][j],
                  w['cv_b_ln'][j], w['cv_w_pw'][j], w['cv_b_pw'][j])
            if state is None:
                h3, tail = _conv_prompt(h.reshape(b, l, D), *cv)
                h, st = h3.reshape(n, D), tail[:, CONV_HALO - CONV_STATE:]
            else:
                h, st = _conv_step(h, state['conv'][j], *cv)
            new.setdefault('conv', []).append(st)
        h = _ffn_ple(h, p[i].reshape(n, PLE_DIM), w['norm_ffn'][i], w['w_ffn_gu'][i], w['w_ffn_down'][i],
                     w['norm_ple'][i], w['w_ple_gate'][i], w['w_ple'][i])
    stacked = tuple(jnp.stack(new[name]) for name in ('ml_c', 'ml_n', 'ml_m', 'pool', 'sb_k', 'sb_v', 'conv'))
    return (h.reshape(b, l, D),) + stacked


def kernel(x_prompt, x_sample, state_mlstm_C, state_mlstm_n, state_mlstm_m, state_pool, cache_sb_k, cache_sb_v,
           state_conv, page_table, p_prompt, p_sample, norm_mix, norm_ffn, w_ffn_gu, w_ffn_down, w_ple, norm_ple,
           w_ple_gate, ml_w_in, ml_b_gate, ml_g_out, ml_w_out, pl_w, pl_scale, sb_w_qkv, sb_g_q, sb_g_k, sb_b_logit,
           sb_w_out, cv_w_glu, cv_b_glu, cv_w_dw, cv_b_dw, cv_g_ln, cv_b_ln, cv_w_pw, cv_b_pw):
    w = _prepare_weights(dict(
        norm_mix=norm_mix, norm_ffn=norm_ffn, w_ffn_gu=w_ffn_gu, w_ffn_down=w_ffn_down, w_ple=w_ple,
        norm_ple=norm_ple, w_ple_gate=w_ple_gate, ml_w_in=ml_w_in, ml_b_gate=ml_b_gate, ml_g_out=ml_g_out,
        ml_w_out=ml_w_out, pl_w=pl_w, pl_scale=pl_scale, sb_w_qkv=sb_w_qkv, sb_g_q=sb_g_q, sb_g_k=sb_g_k,
        sb_b_logit=sb_b_logit, sb_w_out=sb_w_out, cv_w_glu=cv_w_glu, cv_b_glu=cv_b_glu, cv_w_dw=cv_w_dw,
        cv_b_dw=cv_b_dw, cv_g_ln=cv_g_ln, cv_b_ln=cv_b_ln, cv_w_pw=cv_w_pw, cv_b_pw=cv_b_pw))
    prompt = _trunk(x_prompt, p_prompt, w, None)
    sample_state = dict(ml_c=state_mlstm_C, ml_n=state_mlstm_n, ml_m=state_mlstm_m, pool=state_pool,
                        sb_k=cache_sb_k, sb_v=cache_sb_v, page_table=page_table, conv=state_conv,
                        start_pos=page_table.shape[1] * PAGE_SIZE)
    sample = _trunk(x_sample, p_sample, w, sample_state)
    return (prompt[0], sample[0]) + prompt[1:] + sample[1:]
```

```python
import functools

import jax
import jax.numpy as jnp
from jax import lax
from jax.experimental import pallas as pl
from jax.experimental.pallas import tpu as pltpu

F32 = jnp.float32
BF16 = jnp.bfloat16
HIGHEST = lax.Precision.HIGHEST

EPS = 1e-6
LOG2E = 1.4426950408889634
D = 1024
PLE_DIM = 256
D_FF = 2816
ML_HEADS, ML_DQK, ML_DV = 8, 64, 128
POOL_WINDOWS = (2, 4, 8, 16)
POOL_GW = D // len(POOL_WINDOWS)
POOL_STATE = 15
SB_HEADS, SB_DH = 16, 64
CONV_WIDTH = 31
CONV_STATE = CONV_WIDTH - 1
PAGE_SIZE = 128

V7X_VMEM_LIMIT = 56 * 1024 * 1024
LANES = 128
SUBLANES = 8

FF_CHUNK = 256
FFN_ROWS = 512
ML_CHUNK = 256
SB_BLOCK = 256
PAGES_PER_STEP = 8
POOL_HALO = 16
CONV_HALO = 32


def _row_tile(n, target=512):
    return target if n % target == 0 else n


def _params(*sem):
    return pltpu.CompilerParams(dimension_semantics=sem, vmem_limit_bytes=V7X_VMEM_LIMIT)


def _full(shape):
    return pl.BlockSpec(shape, lambda *_: (0,) * len(shape), pipeline_mode=pl.Buffered(1))


def _rows(tm, width):
    return pl.BlockSpec((tm, width), lambda i: (i, 0))


def _rms(x, g):
    return x * lax.rsqrt(jnp.mean(x * x, axis=-1, keepdims=True) + EPS) * g


def _bdot(a, b):
    return jnp.dot(a.astype(BF16), b.astype(BF16), preferred_element_type=F32)


def _bdot_nt(a, b):
    return lax.dot_general(a.astype(BF16), b.astype(BF16), (((1,), (1,)), ((), ())),
                           preferred_element_type=F32)


def _sigmoid(x):
    return 1.0 / (1.0 + jnp.exp(-x))


def _log_sigmoid(x):
    return jnp.minimum(x, 0.0) - jnp.log(1.0 + jnp.exp(-jnp.abs(x)))


def _ffn_ple_kernel(*refs, mixer_proj):
    if mixer_proj:
        h_ref, y_ref, wout_ref, p_ref, gf_ref, wgu_ref, wd_ref, gp_ref, wgate_ref, wple_ref, o_ref = refs
        h = h_ref[...] + _bdot(y_ref[...], wout_ref[...])
    else:
        h_ref, p_ref, gf_ref, wgu_ref, wd_ref, gp_ref, wgate_ref, wple_ref, o_ref = refs
        h = h_ref[...]
    u = _rms(h, gf_ref[...]).astype(BF16)
    acc = jnp.zeros(h.shape, F32)
    for c in range(D_FF // FF_CHUNK):
        lo = c * FF_CHUNK
        gate = jnp.dot(u, wgu_ref[:, lo:lo + FF_CHUNK], preferred_element_type=F32)
        up = jnp.dot(u, wgu_ref[:, D_FF + lo:D_FF + lo + FF_CHUNK], preferred_element_type=F32)
        act = (gate * _sigmoid(gate) * up).astype(BF16)
        acc = acc + jnp.dot(act, wd_ref[lo:lo + FF_CHUNK, :], preferred_element_type=F32)
    h1 = h + acc
    u2 = _rms(h1, gp_ref[...])
    gate = _sigmoid(_bdot(u2, wgate_ref[...]))
    o_ref[...] = h1 + gate * _bdot(p_ref[...], wple_ref[...])


def _ffn_ple(h, p, g_ffn, w_gu, w_down, g_ple, w_gate, w_ple, mixer_proj=None):
    n = h.shape[0]
    tm = _row_tile(n, FFN_ROWS)
    proj_args = [] if mixer_proj is None else list(mixer_proj)
    proj_specs = [] if mixer_proj is None else [_rows(tm, D), _full((D, D))]
    return pl.pallas_call(
        functools.partial(_ffn_ple_kernel, mixer_proj=mixer_proj is not None),
        out_shape=jax.ShapeDtypeStruct((n, D), F32),
        grid=(n // tm,),
        in_specs=[_rows(tm, D)] + proj_specs + [_rows(tm, PLE_DIM), _full((1, D)), _full((D, 2 * D_FF)),
                                                _full((D_FF, D)), _full((1, D)), _full((D, D)), _full((PLE_DIM, D))],
        out_specs=_rows(tm, D),
        compiler_params=_params("parallel"),
        name="ffn_ple",
    )(h, *proj_args, p, g_ffn, w_gu, w_down, g_ple, w_gate, w_ple)


def _ml_in_kernel(h_ref, g_ref, wq_ref, wk_ref, wv_ref, wo_ref, wg_ref, bg_ref,
                  q_ref, k_ref, v_ref, og_ref, gates_ref):
    uf = _rms(h_ref[...], g_ref[...])
    u = uf.astype(BF16)
    q_ref[...] = jnp.dot(u, wq_ref[...], preferred_element_type=F32)
    k_ref[...] = jnp.dot(u, wk_ref[...], preferred_element_type=F32) * (ML_DQK ** -0.5)
    v_ref[...] = jnp.dot(u, wv_ref[...], preferred_element_type=F32)
    og_ref[...] = _sigmoid(jnp.dot(u, wo_ref[...], preferred_element_type=F32))
    gates = jnp.dot(uf, wg_ref[...], preferred_element_type=F32, precision=HIGHEST) + bg_ref[...]
    is_forget = lax.broadcasted_iota(jnp.int32, gates.shape, 1) >= ML_HEADS
    gates_ref[...] = jnp.where(is_forget, _log_sigmoid(gates), gates)


def _ml_in(h, g, wq, wk, wv, wo, wg, bg):
    n = h.shape[0]
    tm = _row_tile(n)
    hq = ML_HEADS * ML_DQK
    return pl.pallas_call(
        _ml_in_kernel,
        out_shape=[jax.ShapeDtypeStruct((n, hq), F32), jax.ShapeDtypeStruct((n, hq), F32),
                   jax.ShapeDtypeStruct((n, D), F32), jax.ShapeDtypeStruct((n, D), F32),
                   jax.ShapeDtypeStruct((n, 2 * ML_HEADS), F32)],
        grid=(n // tm,),
        in_specs=[_rows(tm, D), _full((1, D)), _full((D, hq)), _full((D, hq)), _full((D, D)), _full((D, D)),
                  _full((D, 2 * ML_HEADS)), _full((1, 2 * ML_HEADS))],
        out_specs=[_rows(tm, hq), _rows(tm, hq), _rows(tm, D), _rows(tm, D), _rows(tm, 2 * ML_HEADS)],
        compiler_params=_params("parallel"),
        name="ml_in",
    )(h, g, wq, wk, wv, wo, wg, bg)


def _head_norm_gate(hh, g, og):
    return og * (hh * lax.rsqrt(jnp.mean(hh * hh, axis=-1, keepdims=True) + EPS) * g)


def _ml_chunk_kernel(q_ref, k_ref, v_ref, og_ref, gc_ref, gr_ref, gout_ref, y_ref, c_ref, n_ref, m_ref):
    @pl.when(pl.program_id(1) == 0)
    def _():
        c_ref[...] = jnp.zeros(c_ref.shape, F32)
        n_ref[...] = jnp.zeros(n_ref.shape, F32)
        m_ref[...] = jnp.zeros(m_ref.shape, F32)

    lc = q_ref.shape[1]
    row = lax.broadcasted_iota(jnp.int32, (lc, lc), 0)
    col = lax.broadcasted_iota(jnp.int32, (lc, lc), 1)
    causal = col <= row
    gates_c = gc_ref[0]
    gates_r = gr_ref[0]
    b_c = jnp.dot(causal.astype(F32), gates_c[:, ML_HEADS:], preferred_element_type=F32, precision=HIGHEST)
    b_r = jnp.dot(gates_r[ML_HEADS:, :], (row <= col).astype(F32), preferred_element_type=F32,
                  precision=HIGHEST)
    for hd in range(ML_HEADS):
        qh = q_ref[0, :, hd * ML_DQK:(hd + 1) * ML_DQK]
        kh = k_ref[0, :, hd * ML_DQK:(hd + 1) * ML_DQK]
        vh = v_ref[0, :, hd * ML_DV:(hd + 1) * ML_DV]
        bc = b_c[:, hd:hd + 1]
        br = b_r[hd:hd + 1, :]
        li_c = gates_c[:, hd:hd + 1]
        li_r = gates_r[hd:hd + 1, :]
        m_prev = m_ref[0, hd:hd + 1, 0:1]
        c_prev = c_ref[0, hd]
        n_prev = n_ref[0, hd:hd + 1, :]

        dlog = jnp.where(causal, bc - br + li_r, -jnp.inf)
        inter = bc + m_prev
        m_t = jnp.maximum(inter, jnp.max(dlog, axis=1, keepdims=True))
        w_intra = jnp.exp(dlog - m_t)
        w_inter = jnp.exp(inter - m_t)
        s_mat = w_intra * _bdot_nt(qh, kh)
        num = w_inter * _bdot(qh, c_prev) + _bdot(s_mat, vh)
        den = w_inter * jnp.sum(qh * n_prev, axis=1, keepdims=True) + jnp.sum(s_mat, axis=1, keepdims=True)
        hh = num / jnp.maximum(jnp.abs(den), jnp.exp(-m_t))
        y_ref[0, :, hd * ML_DV:(hd + 1) * ML_DV] = _head_norm_gate(
            hh, gout_ref[:, hd * ML_DV:(hd + 1) * ML_DV], og_ref[0, :, hd * ML_DV:(hd + 1) * ML_DV])

        g_end = br[:, lc - 1:lc]
        dl_end = g_end - bc + li_c
        m_new = jnp.maximum(g_end + m_prev, jnp.max(dl_end, axis=0, keepdims=True))
        a_prev = jnp.exp(g_end + m_prev - m_new)
        kw = jnp.exp(dl_end - m_new) * kh
        c_ref[0, hd] = a_prev * c_prev + lax.dot_general(
            kw.astype(BF16), vh.astype(BF16), (((0,), (0,)), ((), ())), preferred_element_type=F32)
        n_ref[0, hd:hd + 1, :] = a_prev * n_prev + jnp.sum(kw, axis=0, keepdims=True)
        m_ref[0, hd:hd + 1, :] = jnp.broadcast_to(m_new, (1, LANES))


def _ml_chunk(q, k, v, og, gates, g_out):
    b, l, _ = q.shape
    lc = ML_CHUNK if l % ML_CHUNK == 0 else l
    gates_t = jnp.swapaxes(gates, 1, 2)
    hq = ML_HEADS * ML_DQK
    blk = lambda w: pl.BlockSpec((1, lc, w), lambda i, j: (i, j, 0))
    return pl.pallas_call(
        _ml_chunk_kernel,
        out_shape=[jax.ShapeDtypeStruct((b, l, D), F32),
                   jax.ShapeDtypeStruct((b, ML_HEADS, ML_DQK, ML_DV), F32),
                   jax.ShapeDtypeStruct((b, ML_HEADS, ML_DQK), F32),
                   jax.ShapeDtypeStruct((b, ML_HEADS, LANES), F32)],
        grid=(b, l // lc),
        in_specs=[blk(hq), blk(hq), blk(D), blk(D), blk(2 * ML_HEADS),
                  pl.BlockSpec((1, 2 * ML_HEADS, lc), lambda i, j: (i, 0, j)),
                  pl.BlockSpec((1, D), lambda i, j: (0, 0))],
        out_specs=[blk(D),
                   pl.BlockSpec((1, ML_HEADS, ML_DQK, ML_DV), lambda i, j: (i, 0, 0, 0)),
                   pl.BlockSpec((1, ML_HEADS, ML_DQK), lambda i, j: (i, 0, 0)),
                   pl.BlockSpec((1, ML_HEADS, LANES), lambda i, j: (i, 0, 0))],
        compiler_params=_params("parallel", "arbitrary"),
        name="ml_chunk",
    )(q, k, v, og, gates, gates_t, g_out)


def _ml_step_kernel(qt_ref, kt_ref, v_ref, og_ref, gates_ref, c_ref, nt_ref, m_ref, gout_ref,
                    y_ref, c_out, nt_out, m_out):
    lane = lax.broadcasted_iota(jnp.int32, (1, ML_HEADS), 1)
    m_row = jnp.zeros((1, ML_HEADS), F32)
    for hd in range(ML_HEADS):
        qc = qt_ref[0, :, hd:hd + 1]
        kc = kt_ref[0, :, hd:hd + 1]
        nc = nt_ref[0, :, hd:hd + 1]
        vh = v_ref[0, :, hd * ML_DV:(hd + 1) * ML_DV]
        c_prev = c_ref[0, hd]
        li = gates_ref[0, :, hd:hd + 1]
        lf = gates_ref[0, :, ML_HEADS + hd:ML_HEADS + hd + 1]
        m_prev = m_ref[0, :, hd:hd + 1]
        inter = lf + m_prev
        m_t = jnp.maximum(inter, li)
        w_intra = jnp.exp(li - m_t)
        w_inter = jnp.exp(inter - m_t)
        s = w_intra * jnp.sum(qc * kc, axis=0, keepdims=True)
        num = w_inter * jnp.sum(qc * c_prev, axis=0, keepdims=True) + s * vh
        den = w_inter * jnp.sum(qc * nc, axis=0, keepdims=True) + s
        hh = num / jnp.maximum(jnp.abs(den), jnp.exp(-m_t))
        y_ref[0, :, hd * ML_DV:(hd + 1) * ML_DV] = _head_norm_gate(
            hh, gout_ref[:, hd * ML_DV:(hd + 1) * ML_DV], og_ref[0, :, hd * ML_DV:(hd + 1) * ML_DV])
        c_out[0, hd] = w_inter * c_prev + w_intra * (kc * vh)
        nt_out[0, :, hd:hd + 1] = w_inter * nc + w_intra * kc
        m_row = jnp.where(lane == hd, m_t, m_row)
    m_out[0] = m_row


def _ml_step(q, k, v, og, gates, c0, n0, m0, g_out):
    b = q.shape[0]
    to_t = lambda a: jnp.swapaxes(a.reshape(b, ML_HEADS, ML_DQK), 1, 2)
    t_spec = pl.BlockSpec((1, ML_DQK, ML_HEADS), lambda i: (i, 0, 0))
    row = lambda w: pl.BlockSpec((1, 1, w), lambda i: (i, 0, 0))
    c_spec = pl.BlockSpec((1, ML_HEADS, ML_DQK, ML_DV), lambda i: (i, 0, 0, 0))
    y, c1, nt1, m1 = pl.pallas_call(
        _ml_step_kernel,
        out_shape=[jax.ShapeDtypeStruct((b, 1, D), F32),
                   jax.ShapeDtypeStruct((b, ML_HEADS, ML_DQK, ML_DV), F32),
                   jax.ShapeDtypeStruct((b, ML_DQK, ML_HEADS), F32),
                   jax.ShapeDtypeStruct((b, 1, ML_HEADS), F32)],
        grid=(b,),
        in_specs=[t_spec, t_spec, row(D), row(D), row(2 * ML_HEADS), c_spec, t_spec, row(ML_HEADS),
                  pl.BlockSpec((1, D), lambda i: (0, 0))],
        out_specs=[row(D), c_spec, t_spec, row(ML_HEADS)],
        compiler_params=_params("parallel"),
        name="ml_step",
    )(to_t(q), to_t(k), v.reshape(b, 1, D), og.reshape(b, 1, D), gates.reshape(b, 1, 2 * ML_HEADS),
      c0, jnp.swapaxes(n0, 1, 2), m0.reshape(b, 1, ML_HEADS), g_out)
    return y.reshape(b, D), c1, jnp.swapaxes(nt1, 1, 2), m1.reshape(b, ML_HEADS)


def _pool_project(h, pooled, w_ref, scale_ref, o_ref):
    for g in range(len(POOL_WINDOWS)):
        cols = slice(g * POOL_GW, (g + 1) * POOL_GW)
        o_ref[:, cols] = h[:, cols] + _bdot(pooled[g], w_ref[g]) * scale_ref[:, cols]


def _pool_kernel(h_ref, g_ref, w_ref, scale_ref, o_ref, tail_ref, ext_ref):
    tm = h_ref.shape[1]
    step = pl.program_id(1)

    @pl.when(step == 0)
    def _():
        ext_ref[0:POOL_HALO, :] = jnp.zeros((POOL_HALO, D), F32)

    @pl.when(step > 0)
    def _():
        ext_ref[0:POOL_HALO, :] = ext_ref[tm:tm + POOL_HALO, :]

    h = h_ref[0]
    u = _rms(h, g_ref[...])
    ext_ref[POOL_HALO:POOL_HALO + tm, :] = u
    tail_ref[0] = u[tm - POOL_HALO:, :]
    pos = step * tm + lax.broadcasted_iota(jnp.int32, (tm, 1), 0)
    pooled = []
    for g, w in enumerate(POOL_WINDOWS):
        cols = slice(g * POOL_GW, (g + 1) * POOL_GW)
        wsum = u[:, cols]
        for back in range(1, w):
            wsum = wsum + ext_ref[POOL_HALO - back:POOL_HALO - back + tm, cols]
        cnt = jnp.minimum(pos + 1, w).astype(F32)
        pooled.append(wsum / cnt - u[:, cols])
    _pool_project(h, pooled, w_ref, scale_ref, o_ref.at[0])


def _pool_prompt(h, g, w, scale):
    b, l, _ = h.shape
    tm = _row_tile(l)
    return pl.pallas_call(
        _pool_kernel,
        out_shape=[jax.ShapeDtypeStruct((b, l, D), F32), jax.ShapeDtypeStruct((b, POOL_HALO, D), F32)],
        grid=(b, l // tm),
        in_specs=[pl.BlockSpec((1, tm, D), lambda i, j: (i, j, 0)),
                  pl.BlockSpec((1, D), lambda i, j: (0, 0)),
                  pl.BlockSpec((len(POOL_WINDOWS), POOL_GW, POOL_GW), lambda i, j: (0, 0, 0)),
                  pl.BlockSpec((1, D), lambda i, j: (0, 0))],
        out_specs=[pl.BlockSpec((1, tm, D), lambda i, j: (i, j, 0)),
                   pl.BlockSpec((1, POOL_HALO, D), lambda i, j: (i, 0, 0))],
        scratch_shapes=[pltpu.VMEM((POOL_HALO + tm, D), F32)],
        compiler_params=_params("parallel", "arbitrary"),
        name="pool",
    )(h, g, w, scale)


def _pool_step_kernel(h_ref, pref_ref, g_ref, w_ref, scale_ref, o_ref, state_ref, *, start_pos):
    h = h_ref[...]
    u = _rms(h, g_ref[...])
    pooled = []
    for g, w in enumerate(POOL_WINDOWS):
        cols = slice(g * POOL_GW, (g + 1) * POOL_GW)
        wsum = u[:, cols]
        for back in range(1, w):
            wsum = wsum + pref_ref[POOL_STATE - back, :, cols]
        pooled.append(wsum / float(min(start_pos + 1, w)) - u[:, cols])
    _pool_project(h, pooled, w_ref, scale_ref, o_ref)
    state_ref[0:POOL_STATE - 1] = pref_ref[1:POOL_STATE]
    state_ref[POOL_STATE - 1] = u


def _pool_step(h, prefix, g, w, scale, start_pos):
    b = h.shape[0]
    out, state_t = pl.pallas_call(
        functools.partial(_pool_step_kernel, start_pos=start_pos),
        out_shape=[jax.ShapeDtypeStruct((b, D), F32), jax.ShapeDtypeStruct((POOL_STATE, b, D), F32)],
        compiler_params=pltpu.CompilerParams(vmem_limit_bytes=V7X_VMEM_LIMIT),
        name="pool_step",
    )(h, jnp.swapaxes(prefix, 0, 1), g, w, scale)
    return out, jnp.swapaxes(state_t, 0, 1)


def _pair_norm(x, g):
    first = lax.broadcasted_iota(jnp.int32, x.shape, 1) < SB_DH
    sq = x * x
    s_first = jnp.sum(jnp.where(first, sq, 0.0), axis=1, keepdims=True)
    s_second = jnp.sum(jnp.where(first, 0.0, sq), axis=1, keepdims=True)
    inv = jnp.where(first, lax.rsqrt(s_first * (1.0 / SB_DH) + EPS), lax.rsqrt(s_second * (1.0 / SB_DH) + EPS))
    return x * inv * g


def _sb_qkv_kernel(h_ref, g_ref, wq_ref, wk_ref, wv_ref, gq_ref, gk_ref, q_ref, k_ref, v_ref, *feature_major):
    u = _rms(h_ref[...], g_ref[...]).astype(BF16)
    v = jnp.dot(u, wv_ref[...], preferred_element_type=F32)
    v_ref[...] = v
    q = jnp.dot(u, wq_ref[...], preferred_element_type=F32)
    k = jnp.dot(u, wk_ref[...], preferred_element_type=F32)
    for p in range(D // LANES):
        cols = slice(p * LANES, (p + 1) * LANES)
        q_ref[:, cols] = _pair_norm(q[:, cols], gq_ref[...])
        k_pair = _pair_norm(k[:, cols], gk_ref[...])
        k_ref[:, cols] = k_pair
        if feature_major:
            for x, t_ref in ((k_pair, feature_major[0]), (v[:, cols], feature_major[1])):
                xt = x.T
                t_ref[0, 2 * p] = xt[:SB_DH]
                t_ref[0, 2 * p + 1] = xt[SB_DH:]


def _sb_qkv(h, g, wq, wk, wv, gq, gk, seq_len=None):
    n = h.shape[0]
    tm = _row_tile(n)
    out_shape = [jax.ShapeDtypeStruct((n, D), F32)] * 3
    out_specs = [_rows(tm, D)] * 3
    if seq_len is not None:
        per_seq = seq_len // tm
        out_shape += [jax.ShapeDtypeStruct((n // seq_len, SB_HEADS, SB_DH, seq_len), F32)] * 2
        out_specs += [pl.BlockSpec((1, SB_HEADS, SB_DH, tm), lambda i: (i // per_seq, 0, 0, i % per_seq))] * 2
    return pl.pallas_call(
        _sb_qkv_kernel,
        out_shape=out_shape,
        grid=(n // tm,),
        in_specs=[_rows(tm, D), _full((1, D)), _full((D, D)), _full((D, D)), _full((D, D)),
                  _full((1, LANES)), _full((1, LANES))],
        out_specs=out_specs,
        compiler_params=_params("parallel"),
        name="sb_qkv",
    )(h, g, wq, wk, wv, gq, gk)


def _sb_weights(z, valid):
    softplus = jnp.maximum(z, 0.0) + jnp.log(1.0 + jnp.exp(-jnp.abs(z)))
    log_rest = -softplus
    if valid is not None:
        log_rest = jnp.where(valid, log_rest, 0.0)
    return z - softplus, log_rest


def _neg_abs(x):
    bits = lax.bitcast_convert_type(x, jnp.uint32) | jnp.uint32(0x80000000)
    return lax.bitcast_convert_type(bits, F32)


SB_MASKED_LOGIT = -1e30


SB_STAGES = 5
SB_UNROLL = 4


def _sb_attn_kernel(blk_q_ref, blk_k_ref, bias_ref, q_ref, k_ref, v_ref, ntri_ref, mask_ref, o_ref,
                    q2_ref, z_ref, sp16_ref, lba_ref, firsta_ref, inner_ref, lbb_ref, firstb_ref, a_ref,
                    carry_ref):
    n_pairs = blk_q_ref.shape[0]
    tq = z_ref.shape[2]
    pair = pl.program_id(1)
    lane = lax.broadcasted_iota(jnp.int32, (tq, LANES), 1)
    own = [lane < SB_DH, lane >= SB_DH]

    def prepare_queries(i, _):
        q = q_ref[0, pl.ds(pl.multiple_of(i * tq, tq), tq), :] * (SB_DH ** -0.5 * LOG2E)
        for hh in range(2):
            base = SB_DH if hh == 0 else 0
            extra = jnp.where(lane == base, bias_ref[0, 2 * pair + hh],
                              jnp.where(lane == base + 1, bias_ref[1, 2 * pair + hh], 0.0))
            q2_ref[i, hh] = jnp.where(own[hh], q, extra).astype(BF16)
        return 0

    lax.fori_loop(0, q_ref.shape[1] // tq, prepare_queries, 0)
    for ref in (z_ref, sp16_ref, lba_ref, firsta_ref, inner_ref, lbb_ref, firstb_ref, a_ref, carry_ref):
        ref[...] = jnp.zeros(ref.shape, ref.dtype)
    o_ref[...] = jnp.zeros(o_ref.shape, F32)

    def pair_at(t, stage):
        idx = t - stage
        live = (idx >= 0) & (idx < n_pairs)
        idx = jnp.clip(idx, 0, n_pairs - 1)
        return blk_q_ref[idx], blk_k_ref[idx], live

    def rows(block):
        return pl.ds(pl.multiple_of(block * tq, tq), tq)

    def logits(t, slot):
        qi, kj, _ = pair_at(t, 0)
        kb = k_ref[0, rows(kj), :]
        for hh in range(2):
            bias_lanes = (lane >= SB_DH) & (lane < SB_DH + 2) if hh == 0 else (lane < 2)
            kh = jnp.where(own[hh], kb, jnp.where(bias_lanes, 1.0, 0.0)).astype(BF16)
            z_ref[slot, hh] = lax.dot_general(q2_ref[qi, hh], kh, (((1,), (1,)), ((), ())),
                                              preferred_element_type=F32)

    def softplus(t, slot):
        qi, kj, _ = pair_at(t, 1)
        mask = mask_ref[(qi == kj).astype(jnp.int32)]
        for hh in range(2):
            z = z_ref[slot, hh] + mask
            sp = jnp.maximum(z, 0.0) + jnp.log(1.0 + jnp.exp2(_neg_abs(z))) * LOG2E
            lba_ref[slot, hh] = z - sp
            sp16_ref[slot, hh] = sp.astype(BF16)
            firsta_ref[slot, hh] = sp[:, 0:1]

    def later_sum(t, slot):
        for hh in range(2):
            inner_ref[slot, hh] = jnp.dot(sp16_ref[slot, hh], ntri_ref[...], preferred_element_type=F32)
            lbb_ref[slot, hh] = lba_ref[slot, hh]
            firstb_ref[slot, hh] = firsta_ref[slot, hh]

    def weights(t, slot):
        qi, _, live = pair_at(t, 3)
        for hh in range(2):
            carry = carry_ref[qi, hh]
            inner = inner_ref[slot, hh]
            a_ref[slot, hh] = jnp.exp2(lbb_ref[slot, hh] + inner + carry).astype(BF16)
            carry_ref[qi, hh] = jnp.where(live, carry + inner[:, 0:1] - firstb_ref[slot, hh], carry)

    def values(t, slot):
        qi, kj, live = pair_at(t, 4)
        vb = v_ref[0, rows(kj), :]
        parts = [jnp.dot(a_ref[slot, hh], jnp.where(own[hh], vb, 0.0).astype(BF16), preferred_element_type=F32)
                 for hh in range(2)]
        o_ref[0, rows(qi), :] += jnp.where(live, parts[0] + parts[1], 0.0)

    def steps(p, _):
        for sub in range(SB_UNROLL):
            t = SB_UNROLL * p + sub
            even = sub % 2
            logits(t, even)
            later_sum(t, even)
            values(t, even)
            softplus(t, 1 - even)
            weights(t, 1 - even)
        return 0

    lax.fori_loop(0, pl.cdiv(n_pairs + SB_STAGES - 1, SB_UNROLL), steps, 0)


def _later_key_matrix(n):
    r = jnp.arange(n)
    return (r[:, None] > r[None, :]).astype(F32)


def _sb_attn_prompt(q, k, v, b_logit):
    b, l, _ = q.shape
    tq = SB_BLOCK if l % SB_BLOCK == 0 else l
    n_q = l // tq
    pairs = [(i, j) for i in range(n_q) for j in range(i, -1, -1)]
    blk_q = jnp.asarray([i for i, _ in pairs], jnp.int32)
    blk_k = jnp.asarray([j for _, j in pairs], jnp.int32)
    seq = pl.BlockSpec((1, l, LANES), lambda i, p, *_: (i, 0, p))
    ntri = -_later_key_matrix(tq).astype(BF16)
    r = jnp.arange(tq)
    mask = jnp.stack([jnp.zeros((tq, tq), F32), jnp.where(r[None, :] < r[:, None], 0.0, SB_MASKED_LOGIT)])
    bias2 = b_logit * LOG2E
    bias_hi = bias2.astype(BF16).astype(F32)
    bias_split = jnp.stack([bias_hi, bias2 - bias_hi])
    slots = lambda *shape: (2, 2) + shape
    return pl.pallas_call(
        _sb_attn_kernel,
        out_shape=jax.ShapeDtypeStruct((b, l, D), F32),
        grid_spec=pltpu.PrefetchScalarGridSpec(
            num_scalar_prefetch=2,
            grid=(b, SB_HEADS // 2),
            in_specs=[pl.BlockSpec(memory_space=pltpu.SMEM), seq, seq, seq,
                      pl.BlockSpec((tq, tq), lambda i, p, *_: (0, 0)),
                      pl.BlockSpec((2, tq, tq), lambda i, p, *_: (0, 0, 0))],
            out_specs=seq,
            scratch_shapes=[pltpu.VMEM((n_q, 2, tq, LANES), BF16),
                            pltpu.VMEM(slots(tq, tq), F32),
                            pltpu.VMEM(slots(tq, tq), BF16),
                            pltpu.VMEM(slots(tq, tq), F32),
                            pltpu.VMEM(slots(tq, 1), F32),
                            pltpu.VMEM(slots(tq, tq), F32),
                            pltpu.VMEM(slots(tq, tq), F32),
                            pltpu.VMEM(slots(tq, 1), F32),
                            pltpu.VMEM(slots(tq, tq), BF16),
                            pltpu.VMEM((n_q, 2, tq, 1), F32)]),
        compiler_params=_params("parallel", "parallel"),
        name="sb_attn",
    )(blk_q, blk_k, bias_split, q, k, v, ntri, mask)


def _sb_paged_kernel(pt_ref, qt_ref, bias_ref, *refs):
    k_refs, v_refs = refs[:PAGES_PER_STEP], refs[PAGES_PER_STEP:2 * PAGES_PER_STEP]
    tri_ref, o_ref, qb_ref, a_ref, acc_ref, carry_ref = refs[2 * PAGES_PER_STEP:]
    step = pl.program_id(1)

    @pl.when(step == 0)
    def _():
        qt = qt_ref[0] * (SB_DH ** -0.5)
        for h in range(SB_HEADS):
            qb_ref[h] = jnp.broadcast_to(qt[:, h:h + 1], (SB_DH, PAGE_SIZE))
        acc_ref[...] = jnp.zeros(acc_ref.shape, F32)
        carry_ref[...] = jnp.zeros(carry_ref.shape, F32)

    head = lax.broadcasted_iota(jnp.int32, (SB_HEADS, PAGE_SIZE), 0)
    for k_ref, v_ref in zip(k_refs, v_refs):
        z = jnp.zeros((SB_HEADS, PAGE_SIZE), F32)
        for h in range(SB_HEADS):
            z_h = jnp.sum(k_ref[0, h] * qb_ref[h], axis=0, keepdims=True)
            z = jnp.where(head == h, z_h, z)
        log_beta, log_rest = _sb_weights(z + bias_ref[...], None)
        hi = log_rest.astype(BF16)
        rem = log_rest - hi.astype(F32)
        mid = rem.astype(BF16)
        lo = (rem - mid.astype(F32)).astype(BF16)
        terms = jnp.dot(jnp.concatenate([hi, mid, lo], axis=0), tri_ref[...], preferred_element_type=F32)
        inner = terms[:SB_HEADS] + terms[SB_HEADS:2 * SB_HEADS] + terms[2 * SB_HEADS:]
        carry = carry_ref[:, 0:1]
        a_ref[...] = jnp.exp(log_beta + inner + carry)
        carry_ref[...] = jnp.broadcast_to(carry + jnp.sum(log_rest, axis=1, keepdims=True), carry_ref.shape)
        for h in range(SB_HEADS):
            acc_ref[h] += v_ref[0, h] * a_ref[h:h + 1, :]

    @pl.when(step == pl.num_programs(1) - 1)
    def _():
        diag = (lax.broadcasted_iota(jnp.int32, (SB_DH, LANES), 0)
                == lax.broadcasted_iota(jnp.int32, (SB_DH, LANES), 1))
        for h in range(SB_HEADS):
            total = jnp.sum(acc_ref[h], axis=1, keepdims=True)
            o_ref[0, h:h + 1, :] = jnp.sum(jnp.where(diag, total, 0.0), axis=0, keepdims=True)


def _sb_attn_paged(q, cache_k, cache_v, page_table, b_logit):
    b = q.shape[0]
    n_pages = page_table.shape[1]
    steps = n_pages // PAGES_PER_STEP
    kt = jnp.transpose(cache_k, (0, 2, 3, 1))
    vt = jnp.transpose(cache_v, (0, 2, 3, 1))
    qt = jnp.swapaxes(q.reshape(b, SB_HEADS, SB_DH), 1, 2)

    def page_spec(t):
        return pl.BlockSpec((1, SB_HEADS, SB_DH, PAGE_SIZE),
                            lambda i, j, pt: (pt[i * n_pages + n_pages - 1 - (j * PAGES_PER_STEP + t)], 0, 0, 0))

    pages = [page_spec(t) for t in range(PAGES_PER_STEP)]
    out = pl.pallas_call(
        _sb_paged_kernel,
        out_shape=jax.ShapeDtypeStruct((b, SB_HEADS, LANES), F32),
        grid_spec=pltpu.PrefetchScalarGridSpec(
            num_scalar_prefetch=1,
            grid=(b, steps),
            in_specs=[pl.BlockSpec((1, SB_DH, SB_HEADS), lambda i, j, pt: (i, 0, 0)),
                      pl.BlockSpec((SB_HEADS, 1), lambda i, j, pt: (0, 0))] + pages + pages
                     + [pl.BlockSpec((PAGE_SIZE, PAGE_SIZE), lambda i, j, pt: (0, 0))],
            out_specs=pl.BlockSpec((1, SB_HEADS, LANES), lambda i, j, pt: (i, 0, 0)),
            scratch_shapes=[pltpu.VMEM((SB_HEADS, SB_DH, PAGE_SIZE), F32), pltpu.VMEM((SB_HEADS, PAGE_SIZE), F32),
                            pltpu.VMEM((SB_HEADS, SB_DH, LANES), F32), pltpu.VMEM((SB_HEADS, LANES), F32)]),
        compiler_params=_params("parallel", "arbitrary"),
        name="sb_paged",
    )(page_table.reshape(-1), qt, b_logit.reshape(SB_HEADS, 1),
      *([kt] * PAGES_PER_STEP), *([vt] * PAGES_PER_STEP), _later_key_matrix(PAGE_SIZE).astype(BF16))
    return out[:, :, :SB_DH].reshape(b, D)


def _glu(h, g, w_ref, b_ref):
    a = _bdot(_rms(h, g), w_ref[...]) + b_ref[...]
    return a[:, :D] * _sigmoid(a[:, D:])


def _conv_tail(h, c, gln_ref, bln_ref, wpw_ref, bpw_ref):
    cc = c - jnp.mean(c, axis=-1, keepdims=True)
    y = cc * lax.rsqrt(jnp.mean(cc * cc, axis=-1, keepdims=True) + EPS) * gln_ref[...] + bln_ref[...]
    y = y * _sigmoid(y)
    return h + _bdot(y, wpw_ref[...]) + bpw_ref[...]


def _conv_kernel(h_ref, g_ref, wglu_ref, bglu_ref, wdw_ref, bdw_ref, gln_ref, bln_ref, wpw_ref, bpw_ref,
                 o_ref, tail_ref, ext_ref, win_ref):
    tm = h_ref.shape[1]
    step = pl.program_id(1)

    @pl.when(step == 0)
    def _():
        ext_ref[0:CONV_HALO, :] = jnp.zeros((CONV_HALO, D), F32)

    @pl.when(step > 0)
    def _():
        ext_ref[0:CONV_HALO, :] = ext_ref[tm:tm + CONV_HALO, :]

    h = h_ref[0]
    glu = _glu(h, g_ref[...], wglu_ref, bglu_ref)
    ext_ref[CONV_HALO:CONV_HALO + tm, :] = glu
    tail_ref[0] = glu[tm - CONV_HALO:, :]
    c = jnp.broadcast_to(bdw_ref[...], (tm, D))
    first = CONV_HALO - CONV_STATE
    for shift in range(SUBLANES):
        last = max(lo for lo in range(shift, CONV_HALO + 1, SUBLANES))
        if shift == 0:
            window_ref = ext_ref
        else:
            window_ref = win_ref
            win_ref[0:last - shift + tm, :] = ext_ref[shift:last + tm, :]
        for lo in range(shift, last + 1, SUBLANES):
            j = lo - first
            if 0 <= j < CONV_WIDTH:
                c = c + window_ref[lo - shift:lo - shift + tm, :] * wdw_ref[j:j + 1, :]
    o_ref[0] = _conv_tail(h, c, gln_ref, bln_ref, wpw_ref, bpw_ref)


def _conv_prompt(h, g, w_glu, b_glu, w_dw, b_dw, g_ln, b_ln, w_pw, b_pw):
    b, l, _ = h.shape
    tm = _row_tile(l)
    const = lambda shape: pl.BlockSpec(shape, lambda i, j: (0,) * len(shape), pipeline_mode=pl.Buffered(1))
    return pl.pallas_call(
        _conv_kernel,
        out_shape=[jax.ShapeDtypeStruct((b, l, D), F32), jax.ShapeDtypeStruct((b, CONV_HALO, D), F32)],
        grid=(b, l // tm),
        in_specs=[pl.BlockSpec((1, tm, D), lambda i, j: (i, j, 0)),
                  const((1, D)), const((D, 2 * D)), const((1, 2 * D)), const((CONV_WIDTH, D)), const((1, D)),
                  const((1, D)), const((1, D)), const((D, D)), const((1, D))],
        out_specs=[pl.BlockSpec((1, tm, D), lambda i, j: (i, j, 0)),
                   pl.BlockSpec((1, CONV_HALO, D), lambda i, j: (i, 0, 0))],
        scratch_shapes=[pltpu.VMEM((CONV_HALO + tm, D), F32), pltpu.VMEM((CONV_HALO + tm, D), F32)],
        compiler_params=_params("parallel", "arbitrary"),
        name="conv",
    )(h, g, w_glu, b_glu, w_dw, b_dw, g_ln, b_ln, w_pw, b_pw)


def _conv_step_kernel(h_ref, pref_ref, g_ref, wglu_ref, bglu_ref, wdw_ref, bdw_ref, gln_ref, bln_ref,
                      wpw_ref, bpw_ref, o_ref, state_ref):
    h = h_ref[...]
    glu = _glu(h, g_ref[...], wglu_ref, bglu_ref)
    c = bdw_ref[...] + glu * wdw_ref[CONV_STATE:CONV_WIDTH, :]
    for j in range(CONV_STATE):
        c = c + pref_ref[j] * wdw_ref[j:j + 1, :]
    o_ref[...] = _conv_tail(h, c, gln_ref, bln_ref, wpw_ref, bpw_ref)
    state_ref[0:CONV_STATE - 1] = pref_ref[1:CONV_STATE]
    state_ref[CONV_STATE - 1] = glu


def _conv_step(h, prefix, g, w_glu, b_glu, w_dw, b_dw, g_ln, b_ln, w_pw, b_pw):
    b = h.shape[0]
    out, state_t = pl.pallas_call(
        _conv_step_kernel,
        out_shape=[jax.ShapeDtypeStruct((b, D), F32), jax.ShapeDtypeStruct((CONV_STATE, b, D), F32)],
        compiler_params=pltpu.CompilerParams(vmem_limit_bytes=V7X_VMEM_LIMIT),
        name="conv_step",
    )(h, jnp.swapaxes(prefix, 0, 1), g, w_glu, b_glu, w_dw, b_dw, g_ln, b_ln, w_pw, b_pw)
    return out, jnp.swapaxes(state_t, 0, 1)


def _prepare_weights(w):
    hq = ML_HEADS * ML_DQK
    row = lambda a: a.reshape(a.shape[0], 1, -1)
    ml_w_in = w['ml_w_in']
    sb_w = w['sb_w_qkv'].astype(BF16)
    tile_pair = lambda g: jnp.tile(g, (1, 2)).reshape(g.shape[0], 1, LANES)
    return dict(
        norm_mix=row(w['norm_mix']), norm_ffn=row(w['norm_ffn']), norm_ple=row(w['norm_ple']),
        w_ffn_gu=w['w_ffn_gu'].astype(BF16), w_ffn_down=w['w_ffn_down'].astype(BF16),
        w_ple=w['w_ple'].astype(BF16), w_ple_gate=w['w_ple_gate'].astype(BF16),
        ml_wq=ml_w_in[:, :, :hq].astype(BF16), ml_wk=ml_w_in[:, :, hq:2 * hq].astype(BF16),
        ml_wv=ml_w_in[:, :, 2 * hq:2 * hq + D].astype(BF16),
        ml_wo=ml_w_in[:, :, 2 * hq + D:2 * hq + 2 * D].astype(BF16),
        ml_wg=ml_w_in[:, :, 2 * hq + 2 * D:], ml_b_gate=row(w['ml_b_gate']),
        ml_g_out=w['ml_g_out'].reshape(-1, 1, D), ml_w_out=w['ml_w_out'].astype(BF16),
        pl_w=w['pl_w'].astype(BF16), pl_scale=row(w['pl_scale']),
        sb_wq=sb_w[:, :, :D], sb_wk=sb_w[:, :, D:2 * D], sb_wv=sb_w[:, :, 2 * D:],
        sb_g_q=tile_pair(w['sb_g_q']), sb_g_k=tile_pair(w['sb_g_k']), sb_b_logit=w['sb_b_logit'],
        sb_w_out=w['sb_w_out'].astype(BF16),
        cv_w_glu=w['cv_w_glu'].astype(BF16), cv_b_glu=row(w['cv_b_glu']), cv_w_dw=w['cv_w_dw'],
        cv_b_dw=row(w['cv_b_dw']), cv_g_ln=row(w['cv_g_ln']), cv_b_ln=row(w['cv_b_ln']),
        cv_w_pw=w['cv_w_pw'].astype(BF16), cv_b_pw=row(w['cv_b_pw']),
    )


def _trunk(x, p, w, state):
    b, l, _ = x.shape
    n = b * l
    h = x.reshape(n, D)
    depth = p.shape[0]
    new = {}
    for i in range(depth):
        kind, j = i % 4, i // 4
        g_mix = w['norm_mix'][i]
        mixer_proj = None
        if kind == 0:
            q, k, v, og, gates = _ml_in(h, g_mix, w['ml_wq'][j], w['ml_wk'][j], w['ml_wv'][j], w['ml_wo'][j],
                                        w['ml_wg'][j], w['ml_b_gate'][j])
            if state is None:
                y, c1, n1, m1 = _ml_chunk(q.reshape(b, l, -1), k.reshape(b, l, -1), v.reshape(b, l, D),
                                          og.reshape(b, l, D), gates.reshape(b, l, -1), w['ml_g_out'][j])
                m1 = m1[:, :, 0]
            else:
                y, c1, n1, m1 = _ml_step(q, k, v, og, gates, state['ml_c'][j], state['ml_n'][j],
                                         state['ml_m'][j], w['ml_g_out'][j])
            new.setdefault('ml_c', []).append(c1)
            new.setdefault('ml_n', []).append(n1)
            new.setdefault('ml_m', []).append(m1)
            mixer_proj = (y.reshape(n, D), w['ml_w_out'][j])
        elif kind == 1:
            if state is None:
                h3, tail = _pool_prompt(h.reshape(b, l, D), g_mix, w['pl_w'][j], w['pl_scale'][j])
                h, st = h3.reshape(n, D), tail[:, POOL_HALO - POOL_STATE:]
            else:
                h, st = _pool_step(h, state['pool'][j], g_mix, w['pl_w'][j], w['pl_scale'][j], state['start_pos'])
            new.setdefault('pool', []).append(st)
        elif kind == 2:
            sb = (h, g_mix, w['sb_wq'][j], w['sb_wk'][j], w['sb_wv'][j], w['sb_g_q'][j], w['sb_g_k'][j])
            if state is None:
                q, k, v, kt, vt = _sb_qkv(*sb, seq_len=l)
                att = _sb_attn_prompt(q.reshape(b, l, D), k.reshape(b, l, D), v.reshape(b, l, D),
                                      w['sb_b_logit'][j]).reshape(n, D)
                k_new, v_new = jnp.transpose(kt, (0, 3, 1, 2)), jnp.transpose(vt, (0, 3, 1, 2))
            else:
                q, k, v = _sb_qkv(*sb)
                att = _sb_attn_paged(q, state['sb_k'][j], state['sb_v'][j], state['page_table'], w['sb_b_logit'][j])
                k_new, v_new = k.reshape(b, l, SB_HEADS, SB_DH), v.reshape(b, l, SB_HEADS, SB_DH)
            new.setdefault('sb_k', []).append(k_new)
            new.setdefault('sb_v', []).append(v_new)
            mixer_proj = (att, w['sb_w_out'][j])
        else:
            cv = (g_mix, w['cv_w_glu'][j], w['cv_b_glu'][j], w['cv_w_dw'][j], w['cv_b_dw'][j], w['cv_g_ln'][j],
                  w['cv_b_ln'][j], w['cv_w_pw'][j], w['cv_b_pw'][j])
            if state is None:
                h3, tail = _conv_prompt(h.reshape(b, l, D), *cv)
                h, st = h3.reshape(n, D), tail[:, CONV_HALO - CONV_STATE:]
            else:
                h, st = _conv_step(h, state['conv'][j], *cv)
            new.setdefault('conv', []).append(st)
        h = _ffn_ple(h, p[i].reshape(n, PLE_DIM), w['norm_ffn'][i], w['w_ffn_gu'][i], w['w_ffn_down'][i],
                     w['norm_ple'][i], w['w_ple_gate'][i], w['w_ple'][i], mixer_proj)
    stacked = tuple(jnp.stack(new[name]) for name in ('ml_c', 'ml_n', 'ml_m', 'pool', 'sb_k', 'sb_v', 'conv'))
    return (h.reshape(b, l, D),) + stacked


def kernel(x_prompt, x_sample, state_mlstm_C, state_mlstm_n, state_mlstm_m, state_pool, cache_sb_k, cache_sb_v,
           state_conv, page_table, p_prompt, p_sample, norm_mix, norm_ffn, w_ffn_gu, w_ffn_down, w_ple, norm_ple,
           w_ple_gate, ml_w_in, ml_b_gate, ml_g_out, ml_w_out, pl_w, pl_scale, sb_w_qkv, sb_g_q, sb_g_k, sb_b_logit,
           sb_w_out, cv_w_glu, cv_b_glu, cv_w_dw, cv_b_dw, cv_g_ln, cv_b_ln, cv_w_pw, cv_b_pw):
    w = _prepare_weights(dict(
        norm_mix=norm_mix, norm_ffn=norm_ffn, w_ffn_gu=w_ffn_gu, w_ffn_down=w_ffn_down, w_ple=w_ple,
        norm_ple=norm_ple, w_ple_gate=w_ple_gate, ml_w_in=ml_w_in, ml_b_gate=ml_b_gate, ml_g_out=ml_g_out,
        ml_w_out=ml_w_out, pl_w=pl_w, pl_scale=pl_scale, sb_w_qkv=sb_w_qkv, sb_g_q=sb_g_q, sb_g_k=sb_g_k,
        sb_b_logit=sb_b_logit, sb_w_out=sb_w_out, cv_w_glu=cv_w_glu, cv_b_glu=cv_b_glu, cv_w_dw=cv_w_dw,
        cv_b_dw=cv_b_dw, cv_g_ln=cv_g_ln, cv_b_ln=cv_b_ln, cv_w_pw=cv_w_pw, cv_b_pw=cv_b_pw))
    prompt = _trunk(x_prompt, p_prompt, w, None)
    sample_state = dict(ml_c=state_mlstm_C, ml_n=state_mlstm_n, ml_m=state_mlstm_m, pool=state_pool,
                        sb_k=cache_sb_k, sb_v=cache_sb_v, page_table=page_table, conv=state_conv,
                        start_pos=page_table.shape[1] * PAGE_SIZE)
    sample = _trunk(x_sample, p_sample, w, sample_state)
    return (prompt[0], sample[0]) + prompt[1:] + sample[1:]
```

```python
import functools

import jax
import jax.numpy as jnp
from jax import lax
from jax.experimental import pallas as pl
from jax.experimental.pallas import tpu as pltpu

F32 = jnp.float32
BF16 = jnp.bfloat16
HIGHEST = lax.Precision.HIGHEST

EPS = 1e-6
LOG2E = 1.4426950408889634
D = 1024
PLE_DIM = 256
D_FF = 2816
ML_HEADS, ML_DQK, ML_DV = 8, 64, 128
POOL_WINDOWS = (2, 4, 8, 16)
POOL_GW = D // len(POOL_WINDOWS)
POOL_STATE = 15
SB_HEADS, SB_DH = 16, 64
CONV_WIDTH = 31
CONV_STATE = CONV_WIDTH - 1
PAGE_SIZE = 128

V7X_VMEM_LIMIT = 56 * 1024 * 1024
LANES = 128
SUBLANES = 8

FF_CHUNK = 256
FFN_ROWS = 512
ML_CHUNK = 256
SB_BLOCK = 256
PAGES_PER_STEP = 8
ML_STEP_SEQS = 8
POOL_HALO = 16
CONV_HALO = 32


def _row_tile(n, target=512):
    return target if n % target == 0 else n


def _params(*sem):
    return pltpu.CompilerParams(dimension_semantics=sem, vmem_limit_bytes=V7X_VMEM_LIMIT)


def _full(shape):
    return pl.BlockSpec(shape, lambda *_: (0,) * len(shape), pipeline_mode=pl.Buffered(1))


def _rows(tm, width):
    return pl.BlockSpec((tm, width), lambda i: (i, 0))


def _rms(x, g):
    return x * lax.rsqrt(jnp.mean(x * x, axis=-1, keepdims=True) + EPS) * g


def _bdot(a, b):
    return jnp.dot(a.astype(BF16), b.astype(BF16), preferred_element_type=F32)


def _bdot_nt(a, b):
    return lax.dot_general(a.astype(BF16), b.astype(BF16), (((1,), (1,)), ((), ())),
                           preferred_element_type=F32)


def _sigmoid(x):
    return 1.0 / (1.0 + jnp.exp(-x))


def _log_sigmoid(x):
    return jnp.minimum(x, 0.0) - jnp.log(1.0 + jnp.exp(-jnp.abs(x)))


def _ffn_ple_kernel(*refs, mixer_proj):
    if mixer_proj:
        h_ref, y_ref, wout_ref, p_ref, gf_ref, wgu_ref, wd_ref, gp_ref, wgate_ref, wple_ref, o_ref = refs
        h = h_ref[...] + _bdot(y_ref[...], wout_ref[...])
    else:
        h_ref, p_ref, gf_ref, wgu_ref, wd_ref, gp_ref, wgate_ref, wple_ref, o_ref = refs
        h = h_ref[...]
    u = _rms(h, gf_ref[...]).astype(BF16)
    acc = jnp.zeros(h.shape, F32)
    for c in range(D_FF // FF_CHUNK):
        lo = c * FF_CHUNK
        gate = jnp.dot(u, wgu_ref[:, lo:lo + FF_CHUNK], preferred_element_type=F32)
        up = jnp.dot(u, wgu_ref[:, D_FF + lo:D_FF + lo + FF_CHUNK], preferred_element_type=F32)
        act = (gate * _sigmoid(gate) * up).astype(BF16)
        acc = acc + jnp.dot(act, wd_ref[lo:lo + FF_CHUNK, :], preferred_element_type=F32)
    h1 = h + acc
    u2 = _rms(h1, gp_ref[...])
    gate = _sigmoid(_bdot(u2, wgate_ref[...]))
    o_ref[...] = h1 + gate * _bdot(p_ref[...], wple_ref[...])


def _ffn_ple(h, p, g_ffn, w_gu, w_down, g_ple, w_gate, w_ple, mixer_proj=None):
    n = h.shape[0]
    tm = _row_tile(n, FFN_ROWS)
    proj_args = [] if mixer_proj is None else list(mixer_proj)
    proj_specs = [] if mixer_proj is None else [_rows(tm, D), _full((D, D))]
    return pl.pallas_call(
        functools.partial(_ffn_ple_kernel, mixer_proj=mixer_proj is not None),
        out_shape=jax.ShapeDtypeStruct((n, D), F32),
        grid=(n // tm,),
        in_specs=[_rows(tm, D)] + proj_specs + [_rows(tm, PLE_DIM), _full((1, D)), _full((D, 2 * D_FF)),
                                                _full((D_FF, D)), _full((1, D)), _full((D, D)), _full((PLE_DIM, D))],
        out_specs=_rows(tm, D),
        compiler_params=_params("parallel"),
        name="ffn_ple",
    )(h, *proj_args, p, g_ffn, w_gu, w_down, g_ple, w_gate, w_ple)


def _ml_in_kernel(h_ref, g_ref, wq_ref, wk_ref, wv_ref, wo_ref, wg_ref, bg_ref,
                  q_ref, k_ref, v_ref, og_ref, gates_ref):
    uf = _rms(h_ref[...], g_ref[...])
    u = uf.astype(BF16)
    q_ref[...] = jnp.dot(u, wq_ref[...], preferred_element_type=F32)
    k_ref[...] = jnp.dot(u, wk_ref[...], preferred_element_type=F32) * (ML_DQK ** -0.5)
    v_ref[...] = jnp.dot(u, wv_ref[...], preferred_element_type=F32)
    og_ref[...] = _sigmoid(jnp.dot(u, wo_ref[...], preferred_element_type=F32))
    gates = jnp.dot(uf, wg_ref[...], preferred_element_type=F32, precision=HIGHEST) + bg_ref[...]
    is_forget = lax.broadcasted_iota(jnp.int32, gates.shape, 1) >= ML_HEADS
    gates_ref[...] = jnp.where(is_forget, _log_sigmoid(gates), gates)


def _ml_in(h, g, wq, wk, wv, wo, wg, bg):
    n = h.shape[0]
    tm = _row_tile(n)
    hq = ML_HEADS * ML_DQK
    return pl.pallas_call(
        _ml_in_kernel,
        out_shape=[jax.ShapeDtypeStruct((n, hq), F32), jax.ShapeDtypeStruct((n, hq), F32),
                   jax.ShapeDtypeStruct((n, D), F32), jax.ShapeDtypeStruct((n, D), F32),
                   jax.ShapeDtypeStruct((n, 2 * ML_HEADS), F32)],
        grid=(n // tm,),
        in_specs=[_rows(tm, D), _full((1, D)), _full((D, hq)), _full((D, hq)), _full((D, D)), _full((D, D)),
                  _full((D, 2 * ML_HEADS)), _full((1, 2 * ML_HEADS))],
        out_specs=[_rows(tm, hq), _rows(tm, hq), _rows(tm, D), _rows(tm, D), _rows(tm, 2 * ML_HEADS)],
        compiler_params=_params("parallel"),
        name="ml_in",
    )(h, g, wq, wk, wv, wo, wg, bg)


def _head_norm_gate(hh, g, og):
    return og * (hh * lax.rsqrt(jnp.mean(hh * hh, axis=-1, keepdims=True) + EPS) * g)


def _ml_chunk_kernel(q_ref, k_ref, v_ref, og_ref, gc_ref, gr_ref, gout_ref, y_ref, c_ref, n_ref, m_ref):
    @pl.when(pl.program_id(1) == 0)
    def _():
        c_ref[...] = jnp.zeros(c_ref.shape, F32)
        n_ref[...] = jnp.zeros(n_ref.shape, F32)
        m_ref[...] = jnp.zeros(m_ref.shape, F32)

    lc = q_ref.shape[1]
    row = lax.broadcasted_iota(jnp.int32, (lc, lc), 0)
    col = lax.broadcasted_iota(jnp.int32, (lc, lc), 1)
    causal = col <= row
    gates_c = gc_ref[0]
    gates_r = gr_ref[0]
    b_c = jnp.dot(causal.astype(F32), gates_c[:, ML_HEADS:], preferred_element_type=F32, precision=HIGHEST)
    b_r = jnp.dot(gates_r[ML_HEADS:, :], (row <= col).astype(F32), preferred_element_type=F32,
                  precision=HIGHEST)
    for hd in range(ML_HEADS):
        qh = q_ref[0, :, hd * ML_DQK:(hd + 1) * ML_DQK]
        kh = k_ref[0, :, hd * ML_DQK:(hd + 1) * ML_DQK]
        vh = v_ref[0, :, hd * ML_DV:(hd + 1) * ML_DV]
        bc = b_c[:, hd:hd + 1]
        br = b_r[hd:hd + 1, :]
        li_c = gates_c[:, hd:hd + 1]
        li_r = gates_r[hd:hd + 1, :]
        m_prev = m_ref[0, hd:hd + 1, 0:1]
        c_prev = c_ref[0, hd]
        n_prev = n_ref[0, hd:hd + 1, :]

        dlog = jnp.where(causal, bc - br + li_r, -jnp.inf)
        inter = bc + m_prev
        m_t = jnp.maximum(inter, jnp.max(dlog, axis=1, keepdims=True))
        w_intra = jnp.exp(dlog - m_t)
        w_inter = jnp.exp(inter - m_t)
        s_mat = w_intra * _bdot_nt(qh, kh)
        num = w_inter * _bdot(qh, c_prev) + _bdot(s_mat, vh)
        den = w_inter * jnp.sum(qh * n_prev, axis=1, keepdims=True) + jnp.sum(s_mat, axis=1, keepdims=True)
        hh = num / jnp.maximum(jnp.abs(den), jnp.exp(-m_t))
        y_ref[0, :, hd * ML_DV:(hd + 1) * ML_DV] = _head_norm_gate(
            hh, gout_ref[:, hd * ML_DV:(hd + 1) * ML_DV], og_ref[0, :, hd * ML_DV:(hd + 1) * ML_DV])

        g_end = br[:, lc - 1:lc]
        dl_end = g_end - bc + li_c
        m_new = jnp.maximum(g_end + m_prev, jnp.max(dl_end, axis=0, keepdims=True))
        a_prev = jnp.exp(g_end + m_prev - m_new)
        kw = jnp.exp(dl_end - m_new) * kh
        c_ref[0, hd] = a_prev * c_prev + lax.dot_general(
            kw.astype(BF16), vh.astype(BF16), (((0,), (0,)), ((), ())), preferred_element_type=F32)
        n_ref[0, hd:hd + 1, :] = a_prev * n_prev + jnp.sum(kw, axis=0, keepdims=True)
        m_ref[0, hd:hd + 1, :] = jnp.broadcast_to(m_new, (1, LANES))


def _ml_chunk(q, k, v, og, gates, g_out):
    b, l, _ = q.shape
    lc = ML_CHUNK if l % ML_CHUNK == 0 else l
    gates_t = jnp.swapaxes(gates, 1, 2)
    hq = ML_HEADS * ML_DQK
    blk = lambda w: pl.BlockSpec((1, lc, w), lambda i, j: (i, j, 0))
    return pl.pallas_call(
        _ml_chunk_kernel,
        out_shape=[jax.ShapeDtypeStruct((b, l, D), F32),
                   jax.ShapeDtypeStruct((b, ML_HEADS, ML_DQK, ML_DV), F32),
                   jax.ShapeDtypeStruct((b, ML_HEADS, ML_DQK), F32),
                   jax.ShapeDtypeStruct((b, ML_HEADS, LANES), F32)],
        grid=(b, l // lc),
        in_specs=[blk(hq), blk(hq), blk(D), blk(D), blk(2 * ML_HEADS),
                  pl.BlockSpec((1, 2 * ML_HEADS, lc), lambda i, j: (i, 0, j)),
                  pl.BlockSpec((1, D), lambda i, j: (0, 0))],
        out_specs=[blk(D),
                   pl.BlockSpec((1, ML_HEADS, ML_DQK, ML_DV), lambda i, j: (i, 0, 0, 0)),
                   pl.BlockSpec((1, ML_HEADS, ML_DQK), lambda i, j: (i, 0, 0)),
                   pl.BlockSpec((1, ML_HEADS, LANES), lambda i, j: (i, 0, 0))],
        compiler_params=_params("parallel", "arbitrary"),
        name="ml_chunk",
    )(q, k, v, og, gates, gates_t, g_out)


def _ml_step_kernel(qt_ref, kt_ref, v_ref, og_ref, gates_ref, c_ref, nt_ref, m_ref, gout_ref,
                    y_ref, c_out, nt_out, m_out):
    lane = lax.broadcasted_iota(jnp.int32, (1, ML_HEADS), 1)
    for s in range(qt_ref.shape[0]):
        m_row = jnp.zeros((1, ML_HEADS), F32)
        for hd in range(ML_HEADS):
            qc = qt_ref[s, :, hd:hd + 1]
            kc = kt_ref[s, :, hd:hd + 1]
            nc = nt_ref[s, :, hd:hd + 1]
            vh = v_ref[s, :, hd * ML_DV:(hd + 1) * ML_DV]
            c_prev = c_ref[s, hd]
            li = gates_ref[s, :, hd:hd + 1]
            lf = gates_ref[s, :, ML_HEADS + hd:ML_HEADS + hd + 1]
            m_prev = m_ref[s, :, hd:hd + 1]
            inter = lf + m_prev
            m_t = jnp.maximum(inter, li)
            w_intra = jnp.exp(li - m_t)
            w_inter = jnp.exp(inter - m_t)
            qk = w_intra * jnp.sum(qc * kc, axis=0, keepdims=True)
            num = w_inter * jnp.sum(qc * c_prev, axis=0, keepdims=True) + qk * vh
            den = w_inter * jnp.sum(qc * nc, axis=0, keepdims=True) + qk
            hh = num / jnp.maximum(jnp.abs(den), jnp.exp(-m_t))
            y_ref[s, :, hd * ML_DV:(hd + 1) * ML_DV] = _head_norm_gate(
                hh, gout_ref[:, hd * ML_DV:(hd + 1) * ML_DV], og_ref[s, :, hd * ML_DV:(hd + 1) * ML_DV])
            c_out[s, hd] = w_inter * c_prev + w_intra * (kc * vh)
            nt_out[s, :, hd:hd + 1] = w_inter * nc + w_intra * kc
            m_row = jnp.where(lane == hd, m_t, m_row)
        m_out[s] = m_row


def _ml_step(q, k, v, og, gates, c0, n0, m0, g_out):
    b = q.shape[0]
    per = _row_tile(b, ML_STEP_SEQS)
    to_t = lambda a: jnp.swapaxes(a.reshape(b, ML_HEADS, ML_DQK), 1, 2)
    t_spec = pl.BlockSpec((per, ML_DQK, ML_HEADS), lambda i: (i, 0, 0))
    row = lambda w: pl.BlockSpec((per, 1, w), lambda i: (i, 0, 0))
    c_spec = pl.BlockSpec((per, ML_HEADS, ML_DQK, ML_DV), lambda i: (i, 0, 0, 0))
    y, c1, nt1, m1 = pl.pallas_call(
        _ml_step_kernel,
        out_shape=[jax.ShapeDtypeStruct((b, 1, D), F32),
                   jax.ShapeDtypeStruct((b, ML_HEADS, ML_DQK, ML_DV), F32),
                   jax.ShapeDtypeStruct((b, ML_DQK, ML_HEADS), F32),
                   jax.ShapeDtypeStruct((b, 1, ML_HEADS), F32)],
        grid=(b // per,),
        in_specs=[t_spec, t_spec, row(D), row(D), row(2 * ML_HEADS), c_spec, t_spec, row(ML_HEADS),
                  pl.BlockSpec((1, D), lambda i: (0, 0))],
        out_specs=[row(D), c_spec, t_spec, row(ML_HEADS)],
        compiler_params=_params("parallel"),
        name="ml_step",
    )(to_t(q), to_t(k), v.reshape(b, 1, D), og.reshape(b, 1, D), gates.reshape(b, 1, 2 * ML_HEADS),
      c0, jnp.swapaxes(n0, 1, 2), m0.reshape(b, 1, ML_HEADS), g_out)
    return y.reshape(b, D), c1, jnp.swapaxes(nt1, 1, 2), m1.reshape(b, ML_HEADS)


def _pool_project(h, pooled, w_ref, scale_ref, o_ref):
    for g in range(len(POOL_WINDOWS)):
        cols = slice(g * POOL_GW, (g + 1) * POOL_GW)
        o_ref[:, cols] = h[:, cols] + _bdot(pooled[g], w_ref[g]) * scale_ref[:, cols]


def _pool_kernel(h_ref, g_ref, w_ref, scale_ref, o_ref, tail_ref, ext_ref):
    tm = h_ref.shape[1]
    step = pl.program_id(1)

    @pl.when(step == 0)
    def _():
        ext_ref[0:POOL_HALO, :] = jnp.zeros((POOL_HALO, D), F32)

    @pl.when(step > 0)
    def _():
        ext_ref[0:POOL_HALO, :] = ext_ref[tm:tm + POOL_HALO, :]

    h = h_ref[0]
    u = _rms(h, g_ref[...])
    ext_ref[POOL_HALO:POOL_HALO + tm, :] = u
    tail_ref[0] = u[tm - POOL_HALO:, :]
    pos = step * tm + lax.broadcasted_iota(jnp.int32, (tm, 1), 0)
    pooled = []
    for g, w in enumerate(POOL_WINDOWS):
        cols = slice(g * POOL_GW, (g + 1) * POOL_GW)
        wsum = u[:, cols]
        for back in range(1, w):
            wsum = wsum + ext_ref[POOL_HALO - back:POOL_HALO - back + tm, cols]
        cnt = jnp.minimum(pos + 1, w).astype(F32)
        pooled.append(wsum / cnt - u[:, cols])
    _pool_project(h, pooled, w_ref, scale_ref, o_ref.at[0])


def _pool_prompt(h, g, w, scale):
    b, l, _ = h.shape
    tm = _row_tile(l)
    return pl.pallas_call(
        _pool_kernel,
        out_shape=[jax.ShapeDtypeStruct((b, l, D), F32), jax.ShapeDtypeStruct((b, POOL_HALO, D), F32)],
        grid=(b, l // tm),
        in_specs=[pl.BlockSpec((1, tm, D), lambda i, j: (i, j, 0)),
                  pl.BlockSpec((1, D), lambda i, j: (0, 0)),
                  pl.BlockSpec((len(POOL_WINDOWS), POOL_GW, POOL_GW), lambda i, j: (0, 0, 0)),
                  pl.BlockSpec((1, D), lambda i, j: (0, 0))],
        out_specs=[pl.BlockSpec((1, tm, D), lambda i, j: (i, j, 0)),
                   pl.BlockSpec((1, POOL_HALO, D), lambda i, j: (i, 0, 0))],
        scratch_shapes=[pltpu.VMEM((POOL_HALO + tm, D), F32)],
        compiler_params=_params("parallel", "arbitrary"),
        name="pool",
    )(h, g, w, scale)


def _pool_step_kernel(h_ref, pref_ref, g_ref, w_ref, scale_ref, o_ref, state_ref, *, start_pos):
    h = h_ref[...]
    u = _rms(h, g_ref[...])
    pooled = []
    for g, w in enumerate(POOL_WINDOWS):
        cols = slice(g * POOL_GW, (g + 1) * POOL_GW)
        wsum = u[:, cols]
        for back in range(1, w):
            wsum = wsum + pref_ref[POOL_STATE - back, :, cols]
        pooled.append(wsum / float(min(start_pos + 1, w)) - u[:, cols])
    _pool_project(h, pooled, w_ref, scale_ref, o_ref)
    state_ref[0:POOL_STATE - 1] = pref_ref[1:POOL_STATE]
    state_ref[POOL_STATE - 1] = u


def _pool_step(h, prefix, g, w, scale, start_pos):
    b = h.shape[0]
    out, state_t = pl.pallas_call(
        functools.partial(_pool_step_kernel, start_pos=start_pos),
        out_shape=[jax.ShapeDtypeStruct((b, D), F32), jax.ShapeDtypeStruct((POOL_STATE, b, D), F32)],
        compiler_params=pltpu.CompilerParams(vmem_limit_bytes=V7X_VMEM_LIMIT),
        name="pool_step",
    )(h, jnp.swapaxes(prefix, 0, 1), g, w, scale)
    return out, jnp.swapaxes(state_t, 0, 1)


def _pair_norm(x, g):
    first = lax.broadcasted_iota(jnp.int32, x.shape, 1) < SB_DH
    sq = x * x
    s_first = jnp.sum(jnp.where(first, sq, 0.0), axis=1, keepdims=True)
    s_second = jnp.sum(jnp.where(first, 0.0, sq), axis=1, keepdims=True)
    inv = jnp.where(first, lax.rsqrt(s_first * (1.0 / SB_DH) + EPS), lax.rsqrt(s_second * (1.0 / SB_DH) + EPS))
    return x * inv * g


def _sb_qkv_kernel(h_ref, g_ref, wq_ref, wk_ref, wv_ref, gq_ref, gk_ref, q_ref, k_ref, v_ref, *feature_major):
    u = _rms(h_ref[...], g_ref[...]).astype(BF16)
    v = jnp.dot(u, wv_ref[...], preferred_element_type=F32)
    v_ref[...] = v
    q = jnp.dot(u, wq_ref[...], preferred_element_type=F32)
    k = jnp.dot(u, wk_ref[...], preferred_element_type=F32)
    for p in range(D // LANES):
        cols = slice(p * LANES, (p + 1) * LANES)
        q_ref[:, cols] = _pair_norm(q[:, cols], gq_ref[...])
        k_pair = _pair_norm(k[:, cols], gk_ref[...])
        k_ref[:, cols] = k_pair
        if feature_major:
            for x, t_ref in ((k_pair, feature_major[0]), (v[:, cols], feature_major[1])):
                xt = x.T
                t_ref[0, 2 * p] = xt[:SB_DH]
                t_ref[0, 2 * p + 1] = xt[SB_DH:]


def _sb_qkv(h, g, wq, wk, wv, gq, gk, seq_len=None):
    n = h.shape[0]
    tm = _row_tile(n)
    out_shape = [jax.ShapeDtypeStruct((n, D), F32)] * 3
    out_specs = [_rows(tm, D)] * 3
    if seq_len is not None:
        per_seq = seq_len // tm
        out_shape += [jax.ShapeDtypeStruct((n // seq_len, SB_HEADS, SB_DH, seq_len), F32)] * 2
        out_specs += [pl.BlockSpec((1, SB_HEADS, SB_DH, tm), lambda i: (i // per_seq, 0, 0, i % per_seq))] * 2
    return pl.pallas_call(
        _sb_qkv_kernel,
        out_shape=out_shape,
        grid=(n // tm,),
        in_specs=[_rows(tm, D), _full((1, D)), _full((D, D)), _full((D, D)), _full((D, D)),
                  _full((1, LANES)), _full((1, LANES))],
        out_specs=out_specs,
        compiler_params=_params("parallel"),
        name="sb_qkv",
    )(h, g, wq, wk, wv, gq, gk)


def _sb_weights(z, valid):
    softplus = jnp.maximum(z, 0.0) + jnp.log(1.0 + jnp.exp(-jnp.abs(z)))
    log_rest = -softplus
    if valid is not None:
        log_rest = jnp.where(valid, log_rest, 0.0)
    return z - softplus, log_rest


def _neg_abs(x):
    bits = lax.bitcast_convert_type(x, jnp.uint32) | jnp.uint32(0x80000000)
    return lax.bitcast_convert_type(bits, F32)


SB_MASKED_LOGIT = -1e30


SB_STAGES = 5
SB_UNROLL = 4


def _sb_attn_kernel(blk_q_ref, blk_k_ref, bias_ref, q_ref, k_ref, v_ref, ntri_ref, mask_ref, o_ref,
                    q2_ref, k2_ref, v2_ref, z_ref, sp16_ref, za_ref, inner_ref, zb_ref, a_ref, carry_ref):
    n_pairs = blk_q_ref.shape[0]
    tq = z_ref.shape[2]
    pair = pl.program_id(1)
    lane = lax.broadcasted_iota(jnp.int32, (tq, LANES), 1)
    own = [lane < SB_DH, lane >= SB_DH]

    def prepare_operands(i, _):
        block = pl.ds(pl.multiple_of(i * tq, tq), tq)
        q = q_ref[0, block, :] * (SB_DH ** -0.5 * LOG2E)
        kb = k_ref[0, block, :]
        vb = v_ref[0, block, :]
        for hh in range(2):
            base = SB_DH if hh == 0 else 0
            extra = jnp.where(lane == base, bias_ref[0, 2 * pair + hh],
                              jnp.where(lane == base + 1, bias_ref[1, 2 * pair + hh], 0.0))
            q2_ref[i, hh] = jnp.where(own[hh], q, extra).astype(BF16)
            ones = jnp.where((lane == base) | (lane == base + 1), 1.0, 0.0)
            k2_ref[i, hh] = jnp.where(own[hh], kb, ones).astype(BF16)
            v2_ref[i, hh] = jnp.where(own[hh], vb, 0.0).astype(BF16)
        return 0

    lax.fori_loop(0, q_ref.shape[1] // tq, prepare_operands, 0)
    for ref in (z_ref, sp16_ref, za_ref, inner_ref, zb_ref, a_ref, carry_ref):
        ref[...] = jnp.zeros(ref.shape, ref.dtype)
    o_ref[...] = jnp.zeros(o_ref.shape, F32)

    def pair_at(t, stage):
        idx = t - stage
        live = (idx >= 0) & (idx < n_pairs)
        idx = jnp.clip(idx, 0, n_pairs - 1)
        return blk_q_ref[idx], blk_k_ref[idx], live

    def rows(block):
        return pl.ds(pl.multiple_of(block * tq, tq), tq)

    def logits(t, slot):
        qi, kj, _ = pair_at(t, 0)
        for hh in range(2):
            z_ref[slot, hh] = lax.dot_general(q2_ref[qi, hh], k2_ref[kj, hh], (((1,), (1,)), ((), ())),
                                              preferred_element_type=F32)

    def softplus(t, slot):
        qi, kj, _ = pair_at(t, 1)
        mask = mask_ref[(qi == kj).astype(jnp.int32)]
        for hh in range(2):
            z = z_ref[slot, hh] + mask
            sp = jnp.maximum(z, 0.0) + jnp.log(1.0 + jnp.exp2(_neg_abs(z))) * LOG2E
            za_ref[slot, hh] = z
            sp16_ref[slot, hh] = sp.astype(BF16)

    def key_sums(t, slot):
        for hh in range(2):
            inner_ref[slot, hh] = jnp.dot(sp16_ref[slot, hh], ntri_ref[...], preferred_element_type=F32)
            zb_ref[slot, hh] = za_ref[slot, hh]

    def weights(t, slot):
        qi, _, live = pair_at(t, 3)
        for hh in range(2):
            carry = carry_ref[qi, hh]
            inner = inner_ref[slot, hh]
            a_ref[slot, hh] = jnp.exp2(zb_ref[slot, hh] + inner + carry).astype(BF16)
            carry_ref[qi, hh] = jnp.where(live, carry + inner[:, 0:1], carry)

    def values(t, slot):
        qi, kj, live = pair_at(t, 4)
        parts = [jnp.dot(a_ref[slot, hh], v2_ref[kj, hh], preferred_element_type=F32) for hh in range(2)]
        o_ref[0, rows(qi), :] += jnp.where(live, parts[0] + parts[1], 0.0)

    def steps(p, _):
        for sub in range(SB_UNROLL):
            t = SB_UNROLL * p + sub
            even = sub % 2
            logits(t, even)
            key_sums(t, even)
            values(t, even)
            softplus(t, 1 - even)
            weights(t, 1 - even)
        return 0

    lax.fori_loop(0, pl.cdiv(n_pairs + SB_STAGES - 1, SB_UNROLL), steps, 0)


def _later_key_matrix(n, inclusive=False):
    r = jnp.arange(n)
    later = (r[:, None] >= r[None, :]) if inclusive else (r[:, None] > r[None, :])
    return later.astype(F32)


def _sb_attn_prompt(q, k, v, b_logit):
    b, l, _ = q.shape
    tq = SB_BLOCK if l % SB_BLOCK == 0 else l
    n_q = l // tq
    pairs = [(i, j) for i in range(n_q) for j in range(i, -1, -1)]
    blk_q = jnp.asarray([i for i, _ in pairs], jnp.int32)
    blk_k = jnp.asarray([j for _, j in pairs], jnp.int32)
    seq = pl.BlockSpec((1, l, LANES), lambda i, p, *_: (i, 0, p))
    ntri = -_later_key_matrix(tq, inclusive=True).astype(BF16)
    r = jnp.arange(tq)
    mask = jnp.stack([jnp.zeros((tq, tq), F32), jnp.where(r[None, :] < r[:, None], 0.0, SB_MASKED_LOGIT)])
    bias2 = b_logit * LOG2E
    bias_hi = bias2.astype(BF16).astype(F32)
    bias_split = jnp.stack([bias_hi, bias2 - bias_hi])
    slots = lambda *shape: (2, 2) + shape
    return pl.pallas_call(
        _sb_attn_kernel,
        out_shape=jax.ShapeDtypeStruct((b, l, D), F32),
        grid_spec=pltpu.PrefetchScalarGridSpec(
            num_scalar_prefetch=2,
            grid=(b, SB_HEADS // 2),
            in_specs=[pl.BlockSpec(memory_space=pltpu.SMEM), seq, seq, seq,
                      pl.BlockSpec((tq, tq), lambda i, p, *_: (0, 0)),
                      pl.BlockSpec((2, tq, tq), lambda i, p, *_: (0, 0, 0))],
            out_specs=seq,
            scratch_shapes=[pltpu.VMEM((n_q, 2, tq, LANES), BF16),
                            pltpu.VMEM((n_q, 2, tq, LANES), BF16),
                            pltpu.VMEM((n_q, 2, tq, LANES), BF16),
                            pltpu.VMEM(slots(tq, tq), F32),
                            pltpu.VMEM(slots(tq, tq), BF16),
                            pltpu.VMEM(slots(tq, tq), F32),
                            pltpu.VMEM(slots(tq, tq), F32),
                            pltpu.VMEM(slots(tq, tq), F32),
                            pltpu.VMEM(slots(tq, tq), BF16),
                            pltpu.VMEM((n_q, 2, tq, 1), F32)]),
        compiler_params=_params("parallel", "parallel"),
        name="sb_attn",
    )(blk_q, blk_k, bias_split, q, k, v, ntri, mask)


def _sb_paged_kernel(pt_ref, qt_ref, bias_ref, *refs):
    k_refs, v_refs = refs[:PAGES_PER_STEP], refs[PAGES_PER_STEP:2 * PAGES_PER_STEP]
    tri_ref, o_ref, qb_ref, a_ref, acc_ref, carry_ref = refs[2 * PAGES_PER_STEP:]
    step = pl.program_id(1)

    @pl.when(step == 0)
    def _():
        qt = qt_ref[0] * (SB_DH ** -0.5)
        for h in range(SB_HEADS):
            qb_ref[h] = jnp.broadcast_to(qt[:, h:h + 1], (SB_DH, PAGE_SIZE))
        acc_ref[...] = jnp.zeros(acc_ref.shape, F32)
        carry_ref[...] = jnp.zeros(carry_ref.shape, F32)

    head = lax.broadcasted_iota(jnp.int32, (SB_HEADS, PAGE_SIZE), 0)
    for k_ref, v_ref in zip(k_refs, v_refs):
        z = jnp.zeros((SB_HEADS, PAGE_SIZE), F32)
        for h in range(SB_HEADS):
            z_h = jnp.sum(k_ref[0, h] * qb_ref[h], axis=0, keepdims=True)
            z = jnp.where(head == h, z_h, z)
        log_beta, log_rest = _sb_weights(z + bias_ref[...], None)
        hi = log_rest.astype(BF16)
        rem = log_rest - hi.astype(F32)
        mid = rem.astype(BF16)
        lo = (rem - mid.astype(F32)).astype(BF16)
        terms = jnp.dot(jnp.concatenate([hi, mid, lo], axis=0), tri_ref[...], preferred_element_type=F32)
        inner = terms[:SB_HEADS] + terms[SB_HEADS:2 * SB_HEADS] + terms[2 * SB_HEADS:]
        carry = carry_ref[:, 0:1]
        a_ref[...] = jnp.exp(log_beta + inner + carry)
        carry_ref[...] = jnp.broadcast_to(carry + jnp.sum(log_rest, axis=1, keepdims=True), carry_ref.shape)
        for h in range(SB_HEADS):
            acc_ref[h] += v_ref[0, h] * a_ref[h:h + 1, :]

    @pl.when(step == pl.num_programs(1) - 1)
    def _():
        diag = (lax.broadcasted_iota(jnp.int32, (SB_DH, LANES), 0)
                == lax.broadcasted_iota(jnp.int32, (SB_DH, LANES), 1))
        for h in range(SB_HEADS):
            total = jnp.sum(acc_ref[h], axis=1, keepdims=True)
            o_ref[0, h:h + 1, :] = jnp.sum(jnp.where(diag, total, 0.0), axis=0, keepdims=True)


def _sb_attn_paged(q, cache_k, cache_v, page_table, b_logit):
    b = q.shape[0]
    n_pages = page_table.shape[1]
    steps = n_pages // PAGES_PER_STEP
    kt = jnp.transpose(cache_k, (0, 2, 3, 1))
    vt = jnp.transpose(cache_v, (0, 2, 3, 1))
    qt = jnp.swapaxes(q.reshape(b, SB_HEADS, SB_DH), 1, 2)

    def page_spec(t):
        return pl.BlockSpec((1, SB_HEADS, SB_DH, PAGE_SIZE),
                            lambda i, j, pt: (pt[i * n_pages + n_pages - 1 - (j * PAGES_PER_STEP + t)], 0, 0, 0))

    pages = [page_spec(t) for t in range(PAGES_PER_STEP)]
    out = pl.pallas_call(
        _sb_paged_kernel,
        out_shape=jax.ShapeDtypeStruct((b, SB_HEADS, LANES), F32),
        grid_spec=pltpu.PrefetchScalarGridSpec(
            num_scalar_prefetch=1,
            grid=(b, steps),
            in_specs=[pl.BlockSpec((1, SB_DH, SB_HEADS), lambda i, j, pt: (i, 0, 0)),
                      pl.BlockSpec((SB_HEADS, 1), lambda i, j, pt: (0, 0))] + pages + pages
                     + [pl.BlockSpec((PAGE_SIZE, PAGE_SIZE), lambda i, j, pt: (0, 0))],
            out_specs=pl.BlockSpec((1, SB_HEADS, LANES), lambda i, j, pt: (i, 0, 0)),
            scratch_shapes=[pltpu.VMEM((SB_HEADS, SB_DH, PAGE_SIZE), F32), pltpu.VMEM((SB_HEADS, PAGE_SIZE), F32),
                            pltpu.VMEM((SB_HEADS, SB_DH, LANES), F32), pltpu.VMEM((SB_HEADS, LANES), F32)]),
        compiler_params=_params("parallel", "arbitrary"),
        name="sb_paged",
    )(page_table.reshape(-1), qt, b_logit.reshape(SB_HEADS, 1),
      *([kt] * PAGES_PER_STEP), *([vt] * PAGES_PER_STEP), _later_key_matrix(PAGE_SIZE).astype(BF16))
    return out[:, :, :SB_DH].reshape(b, D)


def _glu(h, g, w_ref, b_ref):
    a = _bdot(_rms(h, g), w_ref[...]) + b_ref[...]
    return a[:, :D] * _sigmoid(a[:, D:])


def _conv_tail(h, c, gln_ref, bln_ref, wpw_ref, bpw_ref):
    cc = c - jnp.mean(c, axis=-1, keepdims=True)
    y = cc * lax.rsqrt(jnp.mean(cc * cc, axis=-1, keepdims=True) + EPS) * gln_ref[...] + bln_ref[...]
    y = y * _sigmoid(y)
    return h + _bdot(y, wpw_ref[...]) + bpw_ref[...]


def _conv_kernel(h_ref, g_ref, wglu_ref, bglu_ref, wdw_ref, bdw_ref, gln_ref, bln_ref, wpw_ref, bpw_ref,
                 o_ref, tail_ref, ext_ref, win_ref):
    tm = h_ref.shape[1]
    step = pl.program_id(1)

    @pl.when(step == 0)
    def _():
        ext_ref[0:CONV_HALO, :] = jnp.zeros((CONV_HALO, D), F32)

    @pl.when(step > 0)
    def _():
        ext_ref[0:CONV_HALO, :] = ext_ref[tm:tm + CONV_HALO, :]

    h = h_ref[0]
    glu = _glu(h, g_ref[...], wglu_ref, bglu_ref)
    ext_ref[CONV_HALO:CONV_HALO + tm, :] = glu
    tail_ref[0] = glu[tm - CONV_HALO:, :]
    c = jnp.broadcast_to(bdw_ref[...], (tm, D))
    first = CONV_HALO - CONV_STATE
    for shift in range(SUBLANES):
        last = max(lo for lo in range(shift, CONV_HALO + 1, SUBLANES))
        if shift == 0:
            window_ref = ext_ref
        else:
            window_ref = win_ref
            win_ref[0:last - shift + tm, :] = ext_ref[shift:last + tm, :]
        for lo in range(shift, last + 1, SUBLANES):
            j = lo - first
            if 0 <= j < CONV_WIDTH:
                c = c + window_ref[lo - shift:lo - shift + tm, :] * wdw_ref[j:j + 1, :]
    o_ref[0] = _conv_tail(h, c, gln_ref, bln_ref, wpw_ref, bpw_ref)


def _conv_prompt(h, g, w_glu, b_glu, w_dw, b_dw, g_ln, b_ln, w_pw, b_pw):
    b, l, _ = h.shape
    tm = _row_tile(l)
    const = lambda shape: pl.BlockSpec(shape, lambda i, j: (0,) * len(shape), pipeline_mode=pl.Buffered(1))
    return pl.pallas_call(
        _conv_kernel,
        out_shape=[jax.ShapeDtypeStruct((b, l, D), F32), jax.ShapeDtypeStruct((b, CONV_HALO, D), F32)],
        grid=(b, l // tm),
        in_specs=[pl.BlockSpec((1, tm, D), lambda i, j: (i, j, 0)),
                  const((1, D)), const((D, 2 * D)), const((1, 2 * D)), const((CONV_WIDTH, D)), const((1, D)),
                  const((1, D)), const((1, D)), const((D, D)), const((1, D))],
        out_specs=[pl.BlockSpec((1, tm, D), lambda i, j: (i, j, 0)),
                   pl.BlockSpec((1, CONV_HALO, D), lambda i, j: (i, 0, 0))],
        scratch_shapes=[pltpu.VMEM((CONV_HALO + tm, D), F32), pltpu.VMEM((CONV_HALO + tm, D), F32)],
        compiler_params=_params("parallel", "arbitrary"),
        name="conv",
    )(h, g, w_glu, b_glu, w_dw, b_dw, g_ln, b_ln, w_pw, b_pw)


def _conv_step_kernel(h_ref, pref_ref, g_ref, wglu_ref, bglu_ref, wdw_ref, bdw_ref, gln_ref, bln_ref,
                      wpw_ref, bpw_ref, o_ref, state_ref):
    h = h_ref[...]
    glu = _glu(h, g_ref[...], wglu_ref, bglu_ref)
    c = bdw_ref[...] + glu * wdw_ref[CONV_STATE:CONV_WIDTH, :]
    for j in range(CONV_STATE):
        c = c + pref_ref[j] * wdw_ref[j:j + 1, :]
    o_ref[...] = _conv_tail(h, c, gln_ref, bln_ref, wpw_ref, bpw_ref)
    state_ref[0:CONV_STATE - 1] = pref_ref[1:CONV_STATE]
    state_ref[CONV_STATE - 1] = glu


def _conv_step(h, prefix, g, w_glu, b_glu, w_dw, b_dw, g_ln, b_ln, w_pw, b_pw):
    b = h.shape[0]
    out, state_t = pl.pallas_call(
        _conv_step_kernel,
        out_shape=[jax.ShapeDtypeStruct((b, D), F32), jax.ShapeDtypeStruct((CONV_STATE, b, D), F32)],
        compiler_params=pltpu.CompilerParams(vmem_limit_bytes=V7X_VMEM_LIMIT),
        name="conv_step",
    )(h, jnp.swapaxes(prefix, 0, 1), g, w_glu, b_glu, w_dw, b_dw, g_ln, b_ln, w_pw, b_pw)
    return out, jnp.swapaxes(state_t, 0, 1)


def _prepare_weights(w):
    hq = ML_HEADS * ML_DQK
    row = lambda a: a.reshape(a.shape[0], 1, -1)
    ml_w_in = w['ml_w_in']
    sb_w = w['sb_w_qkv'].astype(BF16)
    tile_pair = lambda g: jnp.tile(g, (1, 2)).reshape(g.shape[0], 1, LANES)
    return dict(
        norm_mix=row(w['norm_mix']), norm_ffn=row(w['norm_ffn']), norm_ple=row(w['norm_ple']),
        w_ffn_gu=w['w_ffn_gu'].astype(BF16), w_ffn_down=w['w_ffn_down'].astype(BF16),
        w_ple=w['w_ple'].astype(BF16), w_ple_gate=w['w_ple_gate'].astype(BF16),
        ml_wq=ml_w_in[:, :, :hq].astype(BF16), ml_wk=ml_w_in[:, :, hq:2 * hq].astype(BF16),
        ml_wv=ml_w_in[:, :, 2 * hq:2 * hq + D].astype(BF16),
        ml_wo=ml_w_in[:, :, 2 * hq + D:2 * hq + 2 * D].astype(BF16),
        ml_wg=ml_w_in[:, :, 2 * hq + 2 * D:], ml_b_gate=row(w['ml_b_gate']),
        ml_g_out=w['ml_g_out'].reshape(-1, 1, D), ml_w_out=w['ml_w_out'].astype(BF16),
        pl_w=w['pl_w'].astype(BF16), pl_scale=row(w['pl_scale']),
        sb_wq=sb_w[:, :, :D], sb_wk=sb_w[:, :, D:2 * D], sb_wv=sb_w[:, :, 2 * D:],
        sb_g_q=tile_pair(w['sb_g_q']), sb_g_k=tile_pair(w['sb_g_k']), sb_b_logit=w['sb_b_logit'],
        sb_w_out=w['sb_w_out'].astype(BF16),
        cv_w_glu=w['cv_w_glu'].astype(BF16), cv_b_glu=row(w['cv_b_glu']), cv_w_dw=w['cv_w_dw'],
        cv_b_dw=row(w['cv_b_dw']), cv_g_ln=row(w['cv_g_ln']), cv_b_ln=row(w['cv_b_ln']),
        cv_w_pw=w['cv_w_pw'].astype(BF16), cv_b_pw=row(w['cv_b_pw']),
    )


def _trunk(x, p, w, state):
    b, l, _ = x.shape
    n = b * l
    h = x.reshape(n, D)
    depth = p.shape[0]
    new = {}
    for i in range(depth):
        kind, j = i % 4, i // 4
        g_mix = w['norm_mix'][i]
        mixer_proj = None
        if kind == 0:
            q, k, v, og, gates = _ml_in(h, g_mix, w['ml_wq'][j], w['ml_wk'][j], w['ml_wv'][j], w['ml_wo'][j],
                                        w['ml_wg'][j], w['ml_b_gate'][j])
            if state is None:
                y, c1, n1, m1 = _ml_chunk(q.reshape(b, l, -1), k.reshape(b, l, -1), v.reshape(b, l, D),
                                          og.reshape(b, l, D), gates.reshape(b, l, -1), w['ml_g_out'][j])
                m1 = m1[:, :, 0]
            else:
                y, c1, n1, m1 = _ml_step(q, k, v, og, gates, state['ml_c'][j], state['ml_n'][j],
                                         state['ml_m'][j], w['ml_g_out'][j])
            new.setdefault('ml_c', []).append(c1)
            new.setdefault('ml_n', []).append(n1)
            new.setdefault('ml_m', []).append(m1)
            mixer_proj = (y.reshape(n, D), w['ml_w_out'][j])
        elif kind == 1:
            if state is None:
                h3, tail = _pool_prompt(h.reshape(b, l, D), g_mix, w['pl_w'][j], w['pl_scale'][j])
                h, st = h3.reshape(n, D), tail[:, POOL_HALO - POOL_STATE:]
            else:
                h, st = _pool_step(h, state['pool'][j], g_mix, w['pl_w'][j], w['pl_scale'][j], state['start_pos'])
            new.setdefault('pool', []).append(st)
        elif kind == 2:
            sb = (h, g_mix, w['sb_wq'][j], w['sb_wk'][j], w['sb_wv'][j], w['sb_g_q'][j], w['sb_g_k'][j])
            if state is None:
                q, k, v, kt, vt = _sb_qkv(*sb, seq_len=l)
                att = _sb_attn_prompt(q.reshape(b, l, D), k.reshape(b, l, D), v.reshape(b, l, D),
                                      w['sb_b_logit'][j]).reshape(n, D)
                k_new, v_new = jnp.transpose(kt, (0, 3, 1, 2)), jnp.transpose(vt, (0, 3, 1, 2))
            else:
                q, k, v = _sb_qkv(*sb)
                att = _sb_attn_paged(q, state['sb_k'][j], state['sb_v'][j], state['page_table'], w['sb_b_logit'][j])
                k_new, v_new = k.reshape(b, l, SB_HEADS, SB_DH), v.reshape(b, l, SB_HEADS, SB_DH)
            new.setdefault('sb_k', []).append(k_new)
            new.setdefault('sb_v', []).append(v_new)
            mixer_proj = (att, w['sb_w_out'][j])
        else:
            cv = (g_mix, w['cv_w_glu'][j], w['cv_b_glu'][j], w['cv_w_dw'][j], w['cv_b_dw'][j], w['cv_g_ln'][j],
                  w['cv_b_ln'][j], w['cv_w_pw'][j], w['cv_b_pw'][j])
            if state is None:
                h3, tail = _conv_prompt(h.reshape(b, l, D), *cv)
                h, st = h3.reshape(n, D), tail[:, CONV_HALO - CONV_STATE:]
            else:
                h, st = _conv_step(h, state['conv'][j], *cv)
            new.setdefault('conv', []).append(st)
        h = _ffn_ple(h, p[i].reshape(n, PLE_DIM), w['norm_ffn'][i], w['w_ffn_gu'][i], w['w_ffn_down'][i],
                     w['norm_ple'][i], w['w_ple_gate'][i], w['w_ple'][i], mixer_proj)
    stacked = tuple(jnp.stack(new[name]) for name in ('ml_c', 'ml_n', 'ml_m', 'pool', 'sb_k', 'sb_v', 'conv'))
    return (h.reshape(b, l, D),) + stacked


def kernel(x_prompt, x_sample, state_mlstm_C, state_mlstm_n, state_mlstm_m, state_pool, cache_sb_k, cache_sb_v,
           state_conv, page_table, p_prompt, p_sample, norm_mix, norm_ffn, w_ffn_gu, w_ffn_down, w_ple, norm_ple,
           w_ple_gate, ml_w_in, ml_b_gate, ml_g_out, ml_w_out, pl_w, pl_scale, sb_w_qkv, sb_g_q, sb_g_k, sb_b_logit,
           sb_w_out, cv_w_glu, cv_b_glu, cv_w_dw, cv_b_dw, cv_g_ln, cv_b_ln, cv_w_pw, cv_b_pw):
    w = _prepare_weights(dict(
        norm_mix=norm_mix, norm_ffn=norm_ffn, w_ffn_gu=w_ffn_gu, w_ffn_down=w_ffn_down, w_ple=w_ple,
        norm_ple=norm_ple, w_ple_gate=w_ple_gate, ml_w_in=ml_w_in, ml_b_gate=ml_b_gate, ml_g_out=ml_g_out,
        ml_w_out=ml_w_out, pl_w=pl_w, pl_scale=pl_scale, sb_w_qkv=sb_w_qkv, sb_g_q=sb_g_q, sb_g_k=sb_g_k,
        sb_b_logit=sb_b_logit, sb_w_out=sb_w_out, cv_w_glu=cv_w_glu, cv_b_glu=cv_b_glu, cv_w_dw=cv_w_dw,
        cv_b_dw=cv_b_dw, cv_g_ln=cv_g_ln, cv_b_ln=cv_b_ln, cv_w_pw=cv_w_pw, cv_b_pw=cv_b_pw))
    prompt = _trunk(x_prompt, p_prompt, w, None)
    sample_state = dict(ml_c=state_mlstm_C, ml_n=state_mlstm_n, ml_m=state_mlstm_m, pool=state_pool,
                        sb_k=cache_sb_k, sb_v=cache_sb_v, page_table=page_table, conv=state_conv,
                        start_pos=page_table.shape[1] * PAGE_SIZE)
    sample = _trunk(x_sample, p_sample, w, sample_state)
    return (prompt[0], sample[0]) + prompt[1:] + sample[1:]
```

```python
import functools

import jax
import jax.numpy as jnp
from jax import lax
from jax.experimental import pallas as pl
from jax.experimental.pallas import tpu as pltpu

F32 = jnp.float32
BF16 = jnp.bfloat16
HIGHEST = lax.Precision.HIGHEST

EPS = 1e-6
LOG2E = 1.4426950408889634
F32_TINY = 1e-37
D = 1024
PLE_DIM = 256
D_FF = 2816
ML_HEADS, ML_DQK, ML_DV = 8, 64, 128
POOL_WINDOWS = (2, 4, 8, 16)
POOL_GW = D // len(POOL_WINDOWS)
POOL_STATE = 15
SB_HEADS, SB_DH = 16, 64
CONV_WIDTH = 31
CONV_STATE = CONV_WIDTH - 1
PAGE_SIZE = 128

V7X_VMEM_LIMIT = 56 * 1024 * 1024
LANES = 128
SUBLANES = 8

FF_CHUNK = 256
FFN_ROWS = 512
ML_CHUNK = 256
SB_BLOCK = 256
PAGES_PER_STEP = 8
ML_STEP_SEQS = 1
POOL_HALO = 16
CONV_HALO = 32


def _row_tile(n, target=512):
    return target if n % target == 0 else n


def _params(*sem):
    return pltpu.CompilerParams(dimension_semantics=sem, vmem_limit_bytes=V7X_VMEM_LIMIT)


def _full(shape):
    return pl.BlockSpec(shape, lambda *_: (0,) * len(shape), pipeline_mode=pl.Buffered(1))


def _rows(tm, width):
    return pl.BlockSpec((tm, width), lambda i: (i, 0))


def _rms(x, g):
    return x * lax.rsqrt(jnp.mean(x * x, axis=-1, keepdims=True) + EPS) * g


def _bdot(a, b):
    return jnp.dot(a.astype(BF16), b.astype(BF16), preferred_element_type=F32)


def _bdot_nt(a, b):
    return lax.dot_general(a.astype(BF16), b.astype(BF16), (((1,), (1,)), ((), ())),
                           preferred_element_type=F32)


def _sigmoid(x):
    return 1.0 / (1.0 + jnp.exp(-x))


def _log_sigmoid(x):
    return jnp.minimum(x, 0.0) - jnp.log(1.0 + jnp.exp(-jnp.abs(x)))


def _ffn_ple_kernel(*refs, mixer_proj):
    if mixer_proj:
        h_ref, y_ref, wout_ref, p_ref, gf_ref, wgu_ref, wd_ref, gp_ref, wgate_ref, wple_ref, o_ref = refs
        h = h_ref[...] + _bdot(y_ref[...], wout_ref[...])
    else:
        h_ref, p_ref, gf_ref, wgu_ref, wd_ref, gp_ref, wgate_ref, wple_ref, o_ref = refs
        h = h_ref[...]
    u = _rms(h, gf_ref[...]).astype(BF16)
    acc = jnp.zeros(h.shape, F32)
    for c in range(D_FF // FF_CHUNK):
        lo = c * FF_CHUNK
        gate = jnp.dot(u, wgu_ref[:, lo:lo + FF_CHUNK], preferred_element_type=F32)
        up = jnp.dot(u, wgu_ref[:, D_FF + lo:D_FF + lo + FF_CHUNK], preferred_element_type=F32)
        act = (gate * _sigmoid(gate) * up).astype(BF16)
        acc = acc + jnp.dot(act, wd_ref[lo:lo + FF_CHUNK, :], preferred_element_type=F32)
    h1 = h + acc
    u2 = _rms(h1, gp_ref[...])
    gate = _sigmoid(_bdot(u2, wgate_ref[...]))
    o_ref[...] = h1 + gate * _bdot(p_ref[...], wple_ref[...])


def _ffn_ple(h, p, g_ffn, w_gu, w_down, g_ple, w_gate, w_ple, mixer_proj=None):
    n = h.shape[0]
    tm = _row_tile(n, FFN_ROWS)
    proj_args = [] if mixer_proj is None else list(mixer_proj)
    proj_specs = [] if mixer_proj is None else [_rows(tm, D), _full((D, D))]
    return pl.pallas_call(
        functools.partial(_ffn_ple_kernel, mixer_proj=mixer_proj is not None),
        out_shape=jax.ShapeDtypeStruct((n, D), F32),
        grid=(n // tm,),
        in_specs=[_rows(tm, D)] + proj_specs + [_rows(tm, PLE_DIM), _full((1, D)), _full((D, 2 * D_FF)),
                                                _full((D_FF, D)), _full((1, D)), _full((D, D)), _full((PLE_DIM, D))],
        out_specs=_rows(tm, D),
        compiler_params=_params("parallel"),
        name="ffn_ple",
    )(h, *proj_args, p, g_ffn, w_gu, w_down, g_ple, w_gate, w_ple)


def _ml_in_kernel(h_ref, g_ref, wq_ref, wk_ref, wv_ref, wo_ref, wg_ref, bg_ref,
                  q_ref, k_ref, v_ref, og_ref, gates_ref):
    uf = _rms(h_ref[...], g_ref[...])
    u = uf.astype(BF16)
    q_ref[...] = jnp.dot(u, wq_ref[...], preferred_element_type=F32)
    k_ref[...] = jnp.dot(u, wk_ref[...], preferred_element_type=F32) * (ML_DQK ** -0.5)
    v_ref[...] = jnp.dot(u, wv_ref[...], preferred_element_type=F32)
    og_ref[...] = _sigmoid(jnp.dot(u, wo_ref[...], preferred_element_type=F32))
    gates = jnp.dot(uf, wg_ref[...], preferred_element_type=F32, precision=HIGHEST) + bg_ref[...]
    is_forget = lax.broadcasted_iota(jnp.int32, gates.shape, 1) >= ML_HEADS
    gates_ref[...] = jnp.where(is_forget, _log_sigmoid(gates), gates)


def _ml_in(h, g, wq, wk, wv, wo, wg, bg):
    n = h.shape[0]
    tm = _row_tile(n)
    hq = ML_HEADS * ML_DQK
    return pl.pallas_call(
        _ml_in_kernel,
        out_shape=[jax.ShapeDtypeStruct((n, hq), F32), jax.ShapeDtypeStruct((n, hq), F32),
                   jax.ShapeDtypeStruct((n, D), F32), jax.ShapeDtypeStruct((n, D), F32),
                   jax.ShapeDtypeStruct((n, 2 * ML_HEADS), F32)],
        grid=(n // tm,),
        in_specs=[_rows(tm, D), _full((1, D)), _full((D, hq)), _full((D, hq)), _full((D, D)), _full((D, D)),
                  _full((D, 2 * ML_HEADS)), _full((1, 2 * ML_HEADS))],
        out_specs=[_rows(tm, hq), _rows(tm, hq), _rows(tm, D), _rows(tm, D), _rows(tm, 2 * ML_HEADS)],
        compiler_params=_params("parallel"),
        name="ml_in",
    )(h, g, wq, wk, wv, wo, wg, bg)


def _head_norm_gate(hh, g, og):
    return og * (hh * lax.rsqrt(jnp.mean(hh * hh, axis=-1, keepdims=True) + EPS) * g)


def _ml_chunk_kernel(q_ref, k_ref, v_ref, og_ref, gc_ref, gr_ref, gout_ref, y_ref, c_ref, n_ref, m_ref):
    @pl.when(pl.program_id(1) == 0)
    def _():
        c_ref[...] = jnp.zeros(c_ref.shape, F32)
        n_ref[...] = jnp.zeros(n_ref.shape, F32)
        m_ref[...] = jnp.zeros(m_ref.shape, F32)

    lc = q_ref.shape[1]
    row = lax.broadcasted_iota(jnp.int32, (lc, lc), 0)
    col = lax.broadcasted_iota(jnp.int32, (lc, lc), 1)
    causal = col <= row
    gates_c = gc_ref[0]
    gates_r = gr_ref[0]
    b_c = jnp.dot(causal.astype(F32), gates_c[:, ML_HEADS:], preferred_element_type=F32, precision=HIGHEST)
    b_r = jnp.dot(gates_r[ML_HEADS:, :], (row <= col).astype(F32), preferred_element_type=F32,
                  precision=HIGHEST)
    lane = lax.broadcasted_iota(jnp.int32, (1, LANES), 1)
    for hd in range(ML_HEADS):
        pair, odd = divmod(hd, 2)
        own = (lane >= ML_DQK) if odd else (lane < ML_DQK)
        cols = slice(pair * LANES, (pair + 1) * LANES)
        qh = jnp.where(own, q_ref[0, :, cols], 0.0)
        k_pair = k_ref[0, :, cols]
        vh = v_ref[0, :, hd * ML_DV:(hd + 1) * ML_DV]
        bc = b_c[:, hd:hd + 1]
        br = b_r[hd:hd + 1, :]
        li_c = gates_c[:, hd:hd + 1]
        li_r = gates_r[hd:hd + 1, :]
        m_prev = m_ref[0, hd:hd + 1, 0:1]
        c_prev = c_ref[0, hd]
        c_pair = c_ref[0, 2 * pair:2 * pair + 2].reshape(2 * ML_DQK, ML_DV)
        n_pair = n_ref[0, pair:pair + 1, :]

        key_side = jnp.where(causal, li_r - br, -jnp.inf)
        top = jnp.maximum(m_prev, jnp.max(key_side, axis=1, keepdims=True))
        m_t = bc + top
        w_intra = jnp.exp(key_side - top)
        w_inter = jnp.exp(m_prev - top)
        s_mat = w_intra * _bdot_nt(qh, k_pair)
        num = w_inter * _bdot(qh, c_pair) + _bdot(s_mat, vh)
        den = w_inter * jnp.sum(qh * n_pair, axis=1, keepdims=True) + jnp.sum(s_mat, axis=1, keepdims=True)
        floor = jnp.maximum(jnp.abs(den), jnp.exp(-m_t))
        scale = lax.rsqrt(jnp.maximum(jnp.mean(num * num, axis=1, keepdims=True) + EPS * floor * floor, F32_TINY))
        y_ref[0, :, hd * ML_DV:(hd + 1) * ML_DV] = (
            og_ref[0, :, hd * ML_DV:(hd + 1) * ML_DV] * (num * scale * gout_ref[:, hd * ML_DV:(hd + 1) * ML_DV]))

        g_end = br[:, lc - 1:lc]
        dl_end = g_end - bc + li_c
        m_new = jnp.maximum(g_end + m_prev, jnp.max(dl_end, axis=0, keepdims=True))
        a_prev = jnp.exp(g_end + m_prev - m_new)
        kw = jnp.where(own, jnp.exp(dl_end - m_new) * k_pair, 0.0)
        update = lax.dot_general(kw.astype(BF16), vh.astype(BF16), (((0,), (0,)), ((), ())),
                                 preferred_element_type=F32)
        c_ref[0, hd] = a_prev * c_prev + update[odd * ML_DQK:(odd + 1) * ML_DQK]
        n_ref[0, pair:pair + 1, :] = jnp.where(own, a_prev * n_pair + jnp.sum(kw, axis=0, keepdims=True), n_pair)
        m_ref[0, hd:hd + 1, :] = jnp.broadcast_to(m_new, (1, LANES))


def _ml_chunk(q, k, v, og, gates, g_out):
    b, l, _ = q.shape
    lc = ML_CHUNK if l % ML_CHUNK == 0 else l
    gates_t = jnp.swapaxes(gates, 1, 2)
    hq = ML_HEADS * ML_DQK
    blk = lambda w: pl.BlockSpec((1, lc, w), lambda i, j: (i, j, 0))
    return pl.pallas_call(
        _ml_chunk_kernel,
        out_shape=[jax.ShapeDtypeStruct((b, l, D), F32),
                   jax.ShapeDtypeStruct((b, ML_HEADS, ML_DQK, ML_DV), F32),
                   jax.ShapeDtypeStruct((b, ML_HEADS // 2, LANES), F32),
                   jax.ShapeDtypeStruct((b, ML_HEADS, LANES), F32)],
        grid=(b, l // lc),
        in_specs=[blk(hq), blk(hq), blk(D), blk(D), blk(2 * ML_HEADS),
                  pl.BlockSpec((1, 2 * ML_HEADS, lc), lambda i, j: (i, 0, j)),
                  pl.BlockSpec((1, D), lambda i, j: (0, 0))],
        out_specs=[blk(D),
                   pl.BlockSpec((1, ML_HEADS, ML_DQK, ML_DV), lambda i, j: (i, 0, 0, 0)),
                   pl.BlockSpec((1, ML_HEADS // 2, LANES), lambda i, j: (i, 0, 0)),
                   pl.BlockSpec((1, ML_HEADS, LANES), lambda i, j: (i, 0, 0))],
        compiler_params=_params("parallel", "arbitrary"),
        name="ml_chunk",
    )(q, k, v, og, gates, gates_t, g_out)


def _ml_step_kernel(qt_ref, kt_ref, v_ref, og_ref, gates_ref, c_ref, nt_ref, m_ref, gout_ref,
                    y_ref, c_out, nt_out, m_out):
    lane = lax.broadcasted_iota(jnp.int32, (1, ML_HEADS), 1)
    for s in range(qt_ref.shape[0]):
        m_row = jnp.zeros((1, ML_HEADS), F32)
        for hd in range(ML_HEADS):
            qc = qt_ref[s, :, hd:hd + 1]
            kc = kt_ref[s, :, hd:hd + 1]
            nc = nt_ref[s, :, hd:hd + 1]
            vh = v_ref[s, :, hd * ML_DV:(hd + 1) * ML_DV]
            c_prev = c_ref[s, hd]
            li = gates_ref[s, :, hd:hd + 1]
            lf = gates_ref[s, :, ML_HEADS + hd:ML_HEADS + hd + 1]
            m_prev = m_ref[s, :, hd:hd + 1]
            inter = lf + m_prev
            m_t = jnp.maximum(inter, li)
            w_intra = jnp.exp(li - m_t)
            w_inter = jnp.exp(inter - m_t)
            qk = w_intra * jnp.sum(qc * kc, axis=0, keepdims=True)
            num = w_inter * jnp.sum(qc * c_prev, axis=0, keepdims=True) + qk * vh
            den = w_inter * jnp.sum(qc * nc, axis=0, keepdims=True) + qk
            hh = num / jnp.maximum(jnp.abs(den), jnp.exp(-m_t))
            y_ref[s, :, hd * ML_DV:(hd + 1) * ML_DV] = _head_norm_gate(
                hh, gout_ref[:, hd * ML_DV:(hd + 1) * ML_DV], og_ref[s, :, hd * ML_DV:(hd + 1) * ML_DV])
            c_out[s, hd] = w_inter * c_prev + w_intra * (kc * vh)
            nt_out[s, :, hd:hd + 1] = w_inter * nc + w_intra * kc
            m_row = jnp.where(lane == hd, m_t, m_row)
        m_out[s] = m_row


def _ml_step(q, k, v, og, gates, c0, n0, m0, g_out):
    b = q.shape[0]
    per = _row_tile(b, ML_STEP_SEQS)
    to_t = lambda a: jnp.swapaxes(a.reshape(b, ML_HEADS, ML_DQK), 1, 2)
    t_spec = pl.BlockSpec((per, ML_DQK, ML_HEADS), lambda i: (i, 0, 0))
    row = lambda w: pl.BlockSpec((per, 1, w), lambda i: (i, 0, 0))
    c_spec = pl.BlockSpec((per, ML_HEADS, ML_DQK, ML_DV), lambda i: (i, 0, 0, 0))
    y, c1, nt1, m1 = pl.pallas_call(
        _ml_step_kernel,
        out_shape=[jax.ShapeDtypeStruct((b, 1, D), F32),
                   jax.ShapeDtypeStruct((b, ML_HEADS, ML_DQK, ML_DV), F32),
                   jax.ShapeDtypeStruct((b, ML_DQK, ML_HEADS), F32),
                   jax.ShapeDtypeStruct((b, 1, ML_HEADS), F32)],
        grid=(b // per,),
        in_specs=[t_spec, t_spec, row(D), row(D), row(2 * ML_HEADS), c_spec, t_spec, row(ML_HEADS),
                  pl.BlockSpec((1, D), lambda i: (0, 0))],
        out_specs=[row(D), c_spec, t_spec, row(ML_HEADS)],
        compiler_params=_params("parallel"),
        name="ml_step",
    )(to_t(q), to_t(k), v.reshape(b, 1, D), og.reshape(b, 1, D), gates.reshape(b, 1, 2 * ML_HEADS),
      c0, jnp.swapaxes(n0, 1, 2), m0.reshape(b, 1, ML_HEADS), g_out)
    return y.reshape(b, D), c1, jnp.swapaxes(nt1, 1, 2), m1.reshape(b, ML_HEADS)


def _pool_project(h, pooled, w_ref, scale_ref, o_ref):
    for g in range(len(POOL_WINDOWS)):
        cols = slice(g * POOL_GW, (g + 1) * POOL_GW)
        o_ref[:, cols] = h[:, cols] + _bdot(pooled[g], w_ref[g]) * scale_ref[:, cols]


def _pool_kernel(h_ref, g_ref, w_ref, scale_ref, o_ref, tail_ref, ext_ref):
    tm = h_ref.shape[1]
    step = pl.program_id(1)

    @pl.when(step == 0)
    def _():
        ext_ref[0:POOL_HALO, :] = jnp.zeros((POOL_HALO, D), F32)

    @pl.when(step > 0)
    def _():
        ext_ref[0:POOL_HALO, :] = ext_ref[tm:tm + POOL_HALO, :]

    h = h_ref[0]
    u = _rms(h, g_ref[...])
    ext_ref[POOL_HALO:POOL_HALO + tm, :] = u
    tail_ref[0] = u[tm - POOL_HALO:, :]
    pos = step * tm + lax.broadcasted_iota(jnp.int32, (tm, 1), 0)
    pooled = []
    for g, w in enumerate(POOL_WINDOWS):
        cols = slice(g * POOL_GW, (g + 1) * POOL_GW)
        wsum = u[:, cols]
        for back in range(1, w):
            wsum = wsum + ext_ref[POOL_HALO - back:POOL_HALO - back + tm, cols]
        cnt = jnp.minimum(pos + 1, w).astype(F32)
        pooled.append(wsum / cnt - u[:, cols])
    _pool_project(h, pooled, w_ref, scale_ref, o_ref.at[0])


def _pool_prompt(h, g, w, scale):
    b, l, _ = h.shape
    tm = _row_tile(l)
    return pl.pallas_call(
        _pool_kernel,
        out_shape=[jax.ShapeDtypeStruct((b, l, D), F32), jax.ShapeDtypeStruct((b, POOL_HALO, D), F32)],
        grid=(b, l // tm),
        in_specs=[pl.BlockSpec((1, tm, D), lambda i, j: (i, j, 0)),
                  pl.BlockSpec((1, D), lambda i, j: (0, 0)),
                  pl.BlockSpec((len(POOL_WINDOWS), POOL_GW, POOL_GW), lambda i, j: (0, 0, 0)),
                  pl.BlockSpec((1, D), lambda i, j: (0, 0))],
        out_specs=[pl.BlockSpec((1, tm, D), lambda i, j: (i, j, 0)),
                   pl.BlockSpec((1, POOL_HALO, D), lambda i, j: (i, 0, 0))],
        scratch_shapes=[pltpu.VMEM((POOL_HALO + tm, D), F32)],
        compiler_params=_params("parallel", "arbitrary"),
        name="pool",
    )(h, g, w, scale)


def _pool_step_kernel(h_ref, pref_ref, g_ref, w_ref, scale_ref, o_ref, state_ref, *, start_pos):
    h = h_ref[...]
    u = _rms(h, g_ref[...])
    pooled = []
    for g, w in enumerate(POOL_WINDOWS):
        cols = slice(g * POOL_GW, (g + 1) * POOL_GW)
        wsum = u[:, cols]
        for back in range(1, w):
            wsum = wsum + pref_ref[POOL_STATE - back, :, cols]
        pooled.append(wsum / float(min(start_pos + 1, w)) - u[:, cols])
    _pool_project(h, pooled, w_ref, scale_ref, o_ref)
    state_ref[0:POOL_STATE - 1] = pref_ref[1:POOL_STATE]
    state_ref[POOL_STATE - 1] = u


def _pool_step(h, prefix, g, w, scale, start_pos):
    b = h.shape[0]
    out, state_t = pl.pallas_call(
        functools.partial(_pool_step_kernel, start_pos=start_pos),
        out_shape=[jax.ShapeDtypeStruct((b, D), F32), jax.ShapeDtypeStruct((POOL_STATE, b, D), F32)],
        compiler_params=pltpu.CompilerParams(vmem_limit_bytes=V7X_VMEM_LIMIT),
        name="pool_step",
    )(h, jnp.swapaxes(prefix, 0, 1), g, w, scale)
    return out, jnp.swapaxes(state_t, 0, 1)


def _pair_norm(x, g):
    first = lax.broadcasted_iota(jnp.int32, x.shape, 1) < SB_DH
    sq = x * x
    s_first = jnp.sum(jnp.where(first, sq, 0.0), axis=1, keepdims=True)
    s_second = jnp.sum(jnp.where(first, 0.0, sq), axis=1, keepdims=True)
    inv = jnp.where(first, lax.rsqrt(s_first * (1.0 / SB_DH) + EPS), lax.rsqrt(s_second * (1.0 / SB_DH) + EPS))
    return x * inv * g


def _sb_qkv_kernel(h_ref, g_ref, wq_ref, wk_ref, wv_ref, gq_ref, gk_ref, q_ref, k_ref, v_ref, *feature_major):
    u = _rms(h_ref[...], g_ref[...]).astype(BF16)
    v = jnp.dot(u, wv_ref[...], preferred_element_type=F32)
    v_ref[...] = v
    q = jnp.dot(u, wq_ref[...], preferred_element_type=F32)
    k = jnp.dot(u, wk_ref[...], preferred_element_type=F32)
    for p in range(D // LANES):
        cols = slice(p * LANES, (p + 1) * LANES)
        q_ref[:, cols] = _pair_norm(q[:, cols], gq_ref[...])
        k_pair = _pair_norm(k[:, cols], gk_ref[...])
        k_ref[:, cols] = k_pair
        if feature_major:
            for x, t_ref in ((k_pair, feature_major[0]), (v[:, cols], feature_major[1])):
                xt = x.T
                t_ref[0, 2 * p] = xt[:SB_DH]
                t_ref[0, 2 * p + 1] = xt[SB_DH:]


def _sb_qkv(h, g, wq, wk, wv, gq, gk, seq_len=None):
    n = h.shape[0]
    tm = _row_tile(n)
    out_shape = [jax.ShapeDtypeStruct((n, D), F32)] * 3
    out_specs = [_rows(tm, D)] * 3
    if seq_len is not None:
        per_seq = seq_len // tm
        out_shape += [jax.ShapeDtypeStruct((n // seq_len, SB_HEADS, SB_DH, seq_len), F32)] * 2
        out_specs += [pl.BlockSpec((1, SB_HEADS, SB_DH, tm), lambda i: (i // per_seq, 0, 0, i % per_seq))] * 2
    return pl.pallas_call(
        _sb_qkv_kernel,
        out_shape=out_shape,
        grid=(n // tm,),
        in_specs=[_rows(tm, D), _full((1, D)), _full((D, D)), _full((D, D)), _full((D, D)),
                  _full((1, LANES)), _full((1, LANES))],
        out_specs=out_specs,
        compiler_params=_params("parallel"),
        name="sb_qkv",
    )(h, g, wq, wk, wv, gq, gk)


def _sb_weights(z, valid):
    softplus = jnp.maximum(z, 0.0) + jnp.log(1.0 + jnp.exp(-jnp.abs(z)))
    log_rest = -softplus
    if valid is not None:
        log_rest = jnp.where(valid, log_rest, 0.0)
    return z - softplus, log_rest


def _neg_abs(x):
    bits = lax.bitcast_convert_type(x, jnp.uint32) | jnp.uint32(0x80000000)
    return lax.bitcast_convert_type(bits, F32)


SB_MASKED_LOGIT = -1e30


SB_STAGES = 5
SB_UNROLL = 4


def _sb_attn_kernel(blk_q_ref, blk_k_ref, bias_ref, q_ref, k_ref, v_ref, ntri_ref, mask_ref, o_ref,
                    q2_ref, k2_ref, v2_ref, z_ref, sp16_ref, za_ref, inner_ref, zb_ref, a_ref, carry_ref):
    n_pairs = blk_q_ref.shape[0]
    tq = z_ref.shape[2]
    pair = pl.program_id(1)
    lane = lax.broadcasted_iota(jnp.int32, (tq, LANES), 1)
    own = [lane < SB_DH, lane >= SB_DH]

    def prepare_operands(i, _):
        block = pl.ds(pl.multiple_of(i * tq, tq), tq)
        q = q_ref[0, block, :] * (SB_DH ** -0.5 * LOG2E)
        kb = k_ref[0, block, :]
        vb = v_ref[0, block, :]
        for hh in range(2):
            base = SB_DH if hh == 0 else 0
            extra = jnp.where(lane == base, bias_ref[0, 2 * pair + hh],
                              jnp.where(lane == base + 1, bias_ref[1, 2 * pair + hh], 0.0))
            q2_ref[i, hh] = jnp.where(own[hh], q, extra).astype(BF16)
            ones = jnp.where((lane == base) | (lane == base + 1), 1.0, 0.0)
            k2_ref[i, hh] = jnp.where(own[hh], kb, ones).astype(BF16)
            v2_ref[i, hh] = jnp.where(own[hh], vb, 0.0).astype(BF16)
        return 0

    lax.fori_loop(0, q_ref.shape[1] // tq, prepare_operands, 0)
    for ref in (z_ref, sp16_ref, za_ref, inner_ref, zb_ref, a_ref, carry_ref):
        ref[...] = jnp.zeros(ref.shape, ref.dtype)
    o_ref[...] = jnp.zeros(o_ref.shape, F32)

    def pair_at(t, stage):
        idx = t - stage
        live = (idx >= 0) & (idx < n_pairs)
        idx = jnp.clip(idx, 0, n_pairs - 1)
        return blk_q_ref[idx], blk_k_ref[idx], live

    def rows(block):
        return pl.ds(pl.multiple_of(block * tq, tq), tq)

    def logits(t, slot):
        qi, kj, _ = pair_at(t, 0)
        for hh in range(2):
            z_ref[slot, hh] = lax.dot_general(q2_ref[qi, hh], k2_ref[kj, hh], (((1,), (1,)), ((), ())),
                                              preferred_element_type=F32)

    def softplus(t, slot):
        qi, kj, _ = pair_at(t, 1)
        mask = mask_ref[(qi == kj).astype(jnp.int32)]
        for hh in range(2):
            z = z_ref[slot, hh] + mask
            sp = jnp.maximum(z, 0.0) + jnp.log(1.0 + jnp.exp2(_neg_abs(z))) * LOG2E
            za_ref[slot, hh] = z
            sp16_ref[slot, hh] = sp.astype(BF16)

    def key_sums(t, slot):
        for hh in range(2):
            inner_ref[slot, hh] = jnp.dot(sp16_ref[slot, hh], ntri_ref[...], preferred_element_type=F32)
            zb_ref[slot, hh] = za_ref[slot, hh]

    def weights(t, slot):
        qi, _, live = pair_at(t, 3)
        for hh in range(2):
            carry = carry_ref[qi, hh]
            inner = inner_ref[slot, hh]
            a_ref[slot, hh] = jnp.exp2(zb_ref[slot, hh] + inner + carry).astype(BF16)
            carry_ref[qi, hh] = jnp.where(live, carry + inner[:, 0:1], carry)

    def values(t, slot):
        qi, kj, live = pair_at(t, 4)
        parts = [jnp.dot(a_ref[slot, hh], v2_ref[kj, hh], preferred_element_type=F32) for hh in range(2)]
        o_ref[0, rows(qi), :] += jnp.where(live, parts[0] + parts[1], 0.0)

    def steps(p, _):
        for sub in range(SB_UNROLL):
            t = SB_UNROLL * p + sub
            even = sub % 2
            logits(t, even)
            key_sums(t, even)
            values(t, even)
            softplus(t, 1 - even)
            weights(t, 1 - even)
        return 0

    lax.fori_loop(0, pl.cdiv(n_pairs + SB_STAGES - 1, SB_UNROLL), steps, 0)


def _later_key_matrix(n, inclusive=False):
    r = jnp.arange(n)
    later = (r[:, None] >= r[None, :]) if inclusive else (r[:, None] > r[None, :])
    return later.astype(F32)


def _sb_attn_prompt(q, k, v, b_logit):
    b, l, _ = q.shape
    tq = SB_BLOCK if l % SB_BLOCK == 0 else l
    n_q = l // tq
    pairs = [(i, j) for i in range(n_q) for j in range(i, -1, -1)]
    blk_q = jnp.asarray([i for i, _ in pairs], jnp.int32)
    blk_k = jnp.asarray([j for _, j in pairs], jnp.int32)
    seq = pl.BlockSpec((1, l, LANES), lambda i, p, *_: (i, 0, p))
    ntri = -_later_key_matrix(tq, inclusive=True).astype(BF16)
    r = jnp.arange(tq)
    mask = jnp.stack([jnp.zeros((tq, tq), F32), jnp.where(r[None, :] < r[:, None], 0.0, SB_MASKED_LOGIT)])
    bias2 = b_logit * LOG2E
    bias_hi = bias2.astype(BF16).astype(F32)
    bias_split = jnp.stack([bias_hi, bias2 - bias_hi])
    slots = lambda *shape: (2, 2) + shape
    return pl.pallas_call(
        _sb_attn_kernel,
        out_shape=jax.ShapeDtypeStruct((b, l, D), F32),
        grid_spec=pltpu.PrefetchScalarGridSpec(
            num_scalar_prefetch=2,
            grid=(b, SB_HEADS // 2),
            in_specs=[pl.BlockSpec(memory_space=pltpu.SMEM), seq, seq, seq,
                      pl.BlockSpec((tq, tq), lambda i, p, *_: (0, 0)),
                      pl.BlockSpec((2, tq, tq), lambda i, p, *_: (0, 0, 0))],
            out_specs=seq,
            scratch_shapes=[pltpu.VMEM((n_q, 2, tq, LANES), BF16),
                            pltpu.VMEM((n_q, 2, tq, LANES), BF16),
                            pltpu.VMEM((n_q, 2, tq, LANES), BF16),
                            pltpu.VMEM(slots(tq, tq), F32),
                            pltpu.VMEM(slots(tq, tq), BF16),
                            pltpu.VMEM(slots(tq, tq), F32),
                            pltpu.VMEM(slots(tq, tq), F32),
                            pltpu.VMEM(slots(tq, tq), F32),
                            pltpu.VMEM(slots(tq, tq), BF16),
                            pltpu.VMEM((n_q, 2, tq, 1), F32)]),
        compiler_params=_params("parallel", "parallel"),
        name="sb_attn",
    )(blk_q, blk_k, bias_split, q, k, v, ntri, mask)


def _sb_paged_kernel(pt_ref, qt_ref, bias_ref, *refs):
    k_refs, v_refs = refs[:PAGES_PER_STEP], refs[PAGES_PER_STEP:2 * PAGES_PER_STEP]
    tri_ref, o_ref, qb_ref, a_ref, acc_ref, carry_ref = refs[2 * PAGES_PER_STEP:]
    step = pl.program_id(1)

    @pl.when(step == 0)
    def _():
        qt = qt_ref[0] * (SB_DH ** -0.5)
        for h in range(SB_HEADS):
            qb_ref[h] = jnp.broadcast_to(qt[:, h:h + 1], (SB_DH, PAGE_SIZE))
        acc_ref[...] = jnp.zeros(acc_ref.shape, F32)
        carry_ref[...] = jnp.zeros(carry_ref.shape, F32)

    head = lax.broadcasted_iota(jnp.int32, (SB_HEADS, PAGE_SIZE), 0)
    for k_ref, v_ref in zip(k_refs, v_refs):
        z = jnp.zeros((SB_HEADS, PAGE_SIZE), F32)
        for h in range(SB_HEADS):
            z_h = jnp.sum(k_ref[0, h] * qb_ref[h], axis=0, keepdims=True)
            z = jnp.where(head == h, z_h, z)
        log_beta, log_rest = _sb_weights(z + bias_ref[...], None)
        hi = log_rest.astype(BF16)
        rem = log_rest - hi.astype(F32)
        mid = rem.astype(BF16)
        lo = (rem - mid.astype(F32)).astype(BF16)
        terms = jnp.dot(jnp.concatenate([hi, mid, lo], axis=0), tri_ref[...], preferred_element_type=F32)
        inner = terms[:SB_HEADS] + terms[SB_HEADS:2 * SB_HEADS] + terms[2 * SB_HEADS:]
        carry = carry_ref[:, 0:1]
        a_ref[...] = jnp.exp(log_beta + inner + carry)
        carry_ref[...] = jnp.broadcast_to(carry + jnp.sum(log_rest, axis=1, keepdims=True), carry_ref.shape)
        for h in range(SB_HEADS):
            acc_ref[h] += v_ref[0, h] * a_ref[h:h + 1, :]

    @pl.when(step == pl.num_programs(1) - 1)
    def _():
        diag = (lax.broadcasted_iota(jnp.int32, (SB_DH, LANES), 0)
                == lax.broadcasted_iota(jnp.int32, (SB_DH, LANES), 1))
        for h in range(SB_HEADS):
            total = jnp.sum(acc_ref[h], axis=1, keepdims=True)
            o_ref[0, h:h + 1, :] = jnp.sum(jnp.where(diag, total, 0.0), axis=0, keepdims=True)


def _sb_attn_paged(q, cache_k, cache_v, page_table, b_logit):
    b = q.shape[0]
    n_pages = page_table.shape[1]
    steps = n_pages // PAGES_PER_STEP
    kt = jnp.transpose(cache_k, (0, 2, 3, 1))
    vt = jnp.transpose(cache_v, (0, 2, 3, 1))
    qt = jnp.swapaxes(q.reshape(b, SB_HEADS, SB_DH), 1, 2)

    def page_spec(t):
        return pl.BlockSpec((1, SB_HEADS, SB_DH, PAGE_SIZE),
                            lambda i, j, pt: (pt[i * n_pages + n_pages - 1 - (j * PAGES_PER_STEP + t)], 0, 0, 0))

    pages = [page_spec(t) for t in range(PAGES_PER_STEP)]
    out = pl.pallas_call(
        _sb_paged_kernel,
        out_shape=jax.ShapeDtypeStruct((b, SB_HEADS, LANES), F32),
        grid_spec=pltpu.PrefetchScalarGridSpec(
            num_scalar_prefetch=1,
            grid=(b, steps),
            in_specs=[pl.BlockSpec((1, SB_DH, SB_HEADS), lambda i, j, pt: (i, 0, 0)),
                      pl.BlockSpec((SB_HEADS, 1), lambda i, j, pt: (0, 0))] + pages + pages
                     + [pl.BlockSpec((PAGE_SIZE, PAGE_SIZE), lambda i, j, pt: (0, 0))],
            out_specs=pl.BlockSpec((1, SB_HEADS, LANES), lambda i, j, pt: (i, 0, 0)),
            scratch_shapes=[pltpu.VMEM((SB_HEADS, SB_DH, PAGE_SIZE), F32), pltpu.VMEM((SB_HEADS, PAGE_SIZE), F32),
                            pltpu.VMEM((SB_HEADS, SB_DH, LANES), F32), pltpu.VMEM((SB_HEADS, LANES), F32)]),
        compiler_params=_params("parallel", "arbitrary"),
        name="sb_paged",
    )(page_table.reshape(-1), qt, b_logit.reshape(SB_HEADS, 1),
      *([kt] * PAGES_PER_STEP), *([vt] * PAGES_PER_STEP), _later_key_matrix(PAGE_SIZE).astype(BF16))
    return out[:, :, :SB_DH].reshape(b, D)


def _glu(h, g, w_ref, b_ref):
    a = _bdot(_rms(h, g), w_ref[...]) + b_ref[...]
    return a[:, :D] * _sigmoid(a[:, D:])


def _conv_tail(h, c, gln_ref, bln_ref, wpw_ref, bpw_ref):
    cc = c - jnp.mean(c, axis=-1, keepdims=True)
    y = cc * lax.rsqrt(jnp.mean(cc * cc, axis=-1, keepdims=True) + EPS) * gln_ref[...] + bln_ref[...]
    y = y * _sigmoid(y)
    return h + _bdot(y, wpw_ref[...]) + bpw_ref[...]


def _conv_kernel(h_ref, g_ref, wglu_ref, bglu_ref, wdw_ref, bdw_ref, gln_ref, bln_ref, wpw_ref, bpw_ref,
                 o_ref, tail_ref, ext_ref, win_ref):
    tm = h_ref.shape[1]
    step = pl.program_id(1)

    @pl.when(step == 0)
    def _():
        ext_ref[0:CONV_HALO, :] = jnp.zeros((CONV_HALO, D), F32)

    @pl.when(step > 0)
    def _():
        ext_ref[0:CONV_HALO, :] = ext_ref[tm:tm + CONV_HALO, :]

    h = h_ref[0]
    glu = _glu(h, g_ref[...], wglu_ref, bglu_ref)
    ext_ref[CONV_HALO:CONV_HALO + tm, :] = glu
    tail_ref[0] = glu[tm - CONV_HALO:, :]
    c = jnp.broadcast_to(bdw_ref[...], (tm, D))
    first = CONV_HALO - CONV_STATE
    for shift in range(SUBLANES):
        last = max(lo for lo in range(shift, CONV_HALO + 1, SUBLANES))
        if shift == 0:
            window_ref = ext_ref
        else:
            window_ref = win_ref
            win_ref[0:last - shift + tm, :] = ext_ref[shift:last + tm, :]
        for lo in range(shift, last + 1, SUBLANES):
            j = lo - first
            if 0 <= j < CONV_WIDTH:
                c = c + window_ref[lo - shift:lo - shift + tm, :] * wdw_ref[j:j + 1, :]
    o_ref[0] = _conv_tail(h, c, gln_ref, bln_ref, wpw_ref, bpw_ref)


def _conv_prompt(h, g, w_glu, b_glu, w_dw, b_dw, g_ln, b_ln, w_pw, b_pw):
    b, l, _ = h.shape
    tm = _row_tile(l)
    const = lambda shape: pl.BlockSpec(shape, lambda i, j: (0,) * len(shape), pipeline_mode=pl.Buffered(1))
    return pl.pallas_call(
        _conv_kernel,
        out_shape=[jax.ShapeDtypeStruct((b, l, D), F32), jax.ShapeDtypeStruct((b, CONV_HALO, D), F32)],
        grid=(b, l // tm),
        in_specs=[pl.BlockSpec((1, tm, D), lambda i, j: (i, j, 0)),
                  const((1, D)), const((D, 2 * D)), const((1, 2 * D)), const((CONV_WIDTH, D)), const((1, D)),
                  const((1, D)), const((1, D)), const((D, D)), const((1, D))],
        out_specs=[pl.BlockSpec((1, tm, D), lambda i, j: (i, j, 0)),
                   pl.BlockSpec((1, CONV_HALO, D), lambda i, j: (i, 0, 0))],
        scratch_shapes=[pltpu.VMEM((CONV_HALO + tm, D), F32), pltpu.VMEM((CONV_HALO + tm, D), F32)],
        compiler_params=_params("parallel", "arbitrary"),
        name="conv",
    )(h, g, w_glu, b_glu, w_dw, b_dw, g_ln, b_ln, w_pw, b_pw)


def _conv_step_kernel(h_ref, pref_ref, g_ref, wglu_ref, bglu_ref, wdw_ref, bdw_ref, gln_ref, bln_ref,
                      wpw_ref, bpw_ref, o_ref, state_ref):
    h = h_ref[...]
    glu = _glu(h, g_ref[...], wglu_ref, bglu_ref)
    c = bdw_ref[...] + glu * wdw_ref[CONV_STATE:CONV_WIDTH, :]
    for j in range(CONV_STATE):
        c = c + pref_ref[j] * wdw_ref[j:j + 1, :]
    o_ref[...] = _conv_tail(h, c, gln_ref, bln_ref, wpw_ref, bpw_ref)
    state_ref[0:CONV_STATE - 1] = pref_ref[1:CONV_STATE]
    state_ref[CONV_STATE - 1] = glu


def _conv_step(h, prefix, g, w_glu, b_glu, w_dw, b_dw, g_ln, b_ln, w_pw, b_pw):
    b = h.shape[0]
    out, state_t = pl.pallas_call(
        _conv_step_kernel,
        out_shape=[jax.ShapeDtypeStruct((b, D), F32), jax.ShapeDtypeStruct((CONV_STATE, b, D), F32)],
        compiler_params=pltpu.CompilerParams(vmem_limit_bytes=V7X_VMEM_LIMIT),
        name="conv_step",
    )(h, jnp.swapaxes(prefix, 0, 1), g, w_glu, b_glu, w_dw, b_dw, g_ln, b_ln, w_pw, b_pw)
    return out, jnp.swapaxes(state_t, 0, 1)


def _prepare_weights(w):
    hq = ML_HEADS * ML_DQK
    row = lambda a: a.reshape(a.shape[0], 1, -1)
    ml_w_in = w['ml_w_in']
    sb_w = w['sb_w_qkv'].astype(BF16)
    tile_pair = lambda g: jnp.tile(g, (1, 2)).reshape(g.shape[0], 1, LANES)
    return dict(
        norm_mix=row(w['norm_mix']), norm_ffn=row(w['norm_ffn']), norm_ple=row(w['norm_ple']),
        w_ffn_gu=w['w_ffn_gu'].astype(BF16), w_ffn_down=w['w_ffn_down'].astype(BF16),
        w_ple=w['w_ple'].astype(BF16), w_ple_gate=w['w_ple_gate'].astype(BF16),
        ml_wq=ml_w_in[:, :, :hq].astype(BF16), ml_wk=ml_w_in[:, :, hq:2 * hq].astype(BF16),
        ml_wv=ml_w_in[:, :, 2 * hq:2 * hq + D].astype(BF16),
        ml_wo=ml_w_in[:, :, 2 * hq + D:2 * hq + 2 * D].astype(BF16),
        ml_wg=ml_w_in[:, :, 2 * hq + 2 * D:], ml_b_gate=row(w['ml_b_gate']),
        ml_g_out=w['ml_g_out'].reshape(-1, 1, D), ml_w_out=w['ml_w_out'].astype(BF16),
        pl_w=w['pl_w'].astype(BF16), pl_scale=row(w['pl_scale']),
        sb_wq=sb_w[:, :, :D], sb_wk=sb_w[:, :, D:2 * D], sb_wv=sb_w[:, :, 2 * D:],
        sb_g_q=tile_pair(w['sb_g_q']), sb_g_k=tile_pair(w['sb_g_k']), sb_b_logit=w['sb_b_logit'],
        sb_w_out=w['sb_w_out'].astype(BF16),
        cv_w_glu=w['cv_w_glu'].astype(BF16), cv_b_glu=row(w['cv_b_glu']), cv_w_dw=w['cv_w_dw'],
        cv_b_dw=row(w['cv_b_dw']), cv_g_ln=row(w['cv_g_ln']), cv_b_ln=row(w['cv_b_ln']),
        cv_w_pw=w['cv_w_pw'].astype(BF16), cv_b_pw=row(w['cv_b_pw']),
    )


def _trunk(x, p, w, state):
    b, l, _ = x.shape
    n = b * l
    h = x.reshape(n, D)
    depth = p.shape[0]
    new = {}
    for i in range(depth):
        kind, j = i % 4, i // 4
        g_mix = w['norm_mix'][i]
        mixer_proj = None
        if kind == 0:
            q, k, v, og, gates = _ml_in(h, g_mix, w['ml_wq'][j], w['ml_wk'][j], w['ml_wv'][j], w['ml_wo'][j],
                                        w['ml_wg'][j], w['ml_b_gate'][j])
            if state is None:
                y, c1, n1, m1 = _ml_chunk(q.reshape(b, l, -1), k.reshape(b, l, -1), v.reshape(b, l, D),
                                          og.reshape(b, l, D), gates.reshape(b, l, -1), w['ml_g_out'][j])
                n1, m1 = n1.reshape(b, ML_HEADS, ML_DQK), m1[:, :, 0]
            else:
                y, c1, n1, m1 = _ml_step(q, k, v, og, gates, state['ml_c'][j], state['ml_n'][j],
                                         state['ml_m'][j], w['ml_g_out'][j])
            new.setdefault('ml_c', []).append(c1)
            new.setdefault('ml_n', []).append(n1)
            new.setdefault('ml_m', []).append(m1)
            mixer_proj = (y.reshape(n, D), w['ml_w_out'][j])
        elif kind == 1:
            if state is None:
                h3, tail = _pool_prompt(h.reshape(b, l, D), g_mix, w['pl_w'][j], w['pl_scale'][j])
                h, st = h3.reshape(n, D), tail[:, POOL_HALO - POOL_STATE:]
            else:
                h, st = _pool_step(h, state['pool'][j], g_mix, w['pl_w'][j], w['pl_scale'][j], state['start_pos'])
            new.setdefault('pool', []).append(st)
        elif kind == 2:
            sb = (h, g_mix, w['sb_wq'][j], w['sb_wk'][j], w['sb_wv'][j], w['sb_g_q'][j], w['sb_g_k'][j])
            if state is None:
                q, k, v, kt, vt = _sb_qkv(*sb, seq_len=l)
                att = _sb_attn_prompt(q.reshape(b, l, D), k.reshape(b, l, D), v.reshape(b, l, D),
                                      w['sb_b_logit'][j]).reshape(n, D)
                k_new, v_new = jnp.transpose(kt, (0, 3, 1, 2)), jnp.transpose(vt, (0, 3, 1, 2))
            else:
                q, k, v = _sb_qkv(*sb)
                att = _sb_attn_paged(q, state['sb_k'][j], state['sb_v'][j], state['page_table'], w['sb_b_logit'][j])
                k_new, v_new = k.reshape(b, l, SB_HEADS, SB_DH), v.reshape(b, l, SB_HEADS, SB_DH)
            new.setdefault('sb_k', []).append(k_new)
            new.setdefault('sb_v', []).append(v_new)
            mixer_proj = (att, w['sb_w_out'][j])
        else:
            cv = (g_mix, w['cv_w_glu'][j], w['cv_b_glu'][j], w['cv_w_dw'][j], w['cv_b_dw'][j], w['cv_g_ln'][j],
                  w['cv_b_ln'][j], w['cv_w_pw'][j], w['cv_b_pw'][j])
            if state is None:
                h3, tail = _conv_prompt(h.reshape(b, l, D), *cv)
                h, st = h3.reshape(n, D), tail[:, CONV_HALO - CONV_STATE:]
            else:
                h, st = _conv_step(h, state['conv'][j], *cv)
            new.setdefault('conv', []).append(st)
        h = _ffn_ple(h, p[i].reshape(n, PLE_DIM), w['norm_ffn'][i], w['w_ffn_gu'][i], w['w_ffn_down'][i],
                     w['norm_ple'][i], w['w_ple_gate'][i], w['w_ple'][i], mixer_proj)
    stacked = tuple(jnp.stack(new[name]) for name in ('ml_c', 'ml_n', 'ml_m', 'pool', 'sb_k', 'sb_v', 'conv'))
    return (h.reshape(b, l, D),) + stacked


def kernel(x_prompt, x_sample, state_mlstm_C, state_mlstm_n, state_mlstm_m, state_pool, cache_sb_k, cache_sb_v,
           state_conv, page_table, p_prompt, p_sample, norm_mix, norm_ffn, w_ffn_gu, w_ffn_down, w_ple, norm_ple,
           w_ple_gate, ml_w_in, ml_b_gate, ml_g_out, ml_w_out, pl_w, pl_scale, sb_w_qkv, sb_g_q, sb_g_k, sb_b_logit,
           sb_w_out, cv_w_glu, cv_b_glu, cv_w_dw, cv_b_dw, cv_g_ln, cv_b_ln, cv_w_pw, cv_b_pw):
    w = _prepare_weights(dict(
        norm_mix=norm_mix, norm_ffn=norm_ffn, w_ffn_gu=w_ffn_gu, w_ffn_down=w_ffn_down, w_ple=w_ple,
        norm_ple=norm_ple, w_ple_gate=w_ple_gate, ml_w_in=ml_w_in, ml_b_gate=ml_b_gate, ml_g_out=ml_g_out,
        ml_w_out=ml_w_out, pl_w=pl_w, pl_scale=pl_scale, sb_w_qkv=sb_w_qkv, sb_g_q=sb_g_q, sb_g_k=sb_g_k,
        sb_b_logit=sb_b_logit, sb_w_out=sb_w_out, cv_w_glu=cv_w_glu, cv_b_glu=cv_b_glu, cv_w_dw=cv_w_dw,
        cv_b_dw=cv_b_dw, cv_g_ln=cv_g_ln, cv_b_ln=cv_b_ln, cv_w_pw=cv_w_pw, cv_b_pw=cv_b_pw))
    prompt = _trunk(x_prompt, p_prompt, w, None)
    sample_state = dict(ml_c=state_mlstm_C, ml_n=state_mlstm_n, ml_m=state_mlstm_m, pool=state_pool,
                        sb_k=cache_sb_k, sb_v=cache_sb_v, page_table=page_table, conv=state_conv,
                        start_pos=page_table.shape[1] * PAGE_SIZE)
    sample = _trunk(x_sample, p_sample, w, sample_state)
    return (prompt[0], sample[0]) + prompt[1:] + sample[1:]
```

```python
import functools

import jax
import jax.numpy as jnp
from jax import lax
from jax.experimental import pallas as pl
from jax.experimental.pallas import tpu as pltpu

F32 = jnp.float32
BF16 = jnp.bfloat16
HIGHEST = lax.Precision.HIGHEST

EPS = 1e-6
LOG2E = 1.4426950408889634
F32_TINY = 1e-37
D = 1024
PLE_DIM = 256
D_FF = 2816
ML_HEADS, ML_DQK, ML_DV = 8, 64, 128
POOL_WINDOWS = (2, 4, 8, 16)
POOL_GW = D // len(POOL_WINDOWS)
POOL_STATE = 15
SB_HEADS, SB_DH = 16, 64
CONV_WIDTH = 31
CONV_STATE = CONV_WIDTH - 1
PAGE_SIZE = 128

V7X_VMEM_LIMIT = 56 * 1024 * 1024
LANES = 128
SUBLANES = 8

FF_CHUNK = 256
FFN_ROWS = 512
ML_CHUNK = 256
SB_BLOCK = 256
PAGES_PER_STEP = 8
ML_STEP_SEQS = 1
POOL_HALO = 16
CONV_HALO = 32


def _row_tile(n, target=512):
    return target if n % target == 0 else n


def _params(*sem):
    return pltpu.CompilerParams(dimension_semantics=sem, vmem_limit_bytes=V7X_VMEM_LIMIT)


def _full(shape):
    return pl.BlockSpec(shape, lambda *_: (0,) * len(shape), pipeline_mode=pl.Buffered(1))


def _rows(tm, width):
    return pl.BlockSpec((tm, width), lambda i: (i, 0))


def _rms(x, g):
    return x * lax.rsqrt(jnp.mean(x * x, axis=-1, keepdims=True) + EPS) * g


def _bdot(a, b):
    return jnp.dot(a.astype(BF16), b.astype(BF16), preferred_element_type=F32)


def _bdot_nt(a, b):
    return lax.dot_general(a.astype(BF16), b.astype(BF16), (((1,), (1,)), ((), ())),
                           preferred_element_type=F32)


def _sigmoid(x):
    return 1.0 / (1.0 + jnp.exp(-x))


def _log_sigmoid(x):
    return jnp.minimum(x, 0.0) - jnp.log(1.0 + jnp.exp(-jnp.abs(x)))


def _ffn_ple_kernel(*refs, mixer_proj):
    if mixer_proj:
        h_ref, y_ref, wout_ref, p_ref, gf_ref, wgu_ref, wd_ref, gp_ref, wgate_ref, wple_ref, o_ref = refs
        h = h_ref[...] + _bdot(y_ref[...], wout_ref[...])
    else:
        h_ref, p_ref, gf_ref, wgu_ref, wd_ref, gp_ref, wgate_ref, wple_ref, o_ref = refs
        h = h_ref[...]
    u = _rms(h, gf_ref[...]).astype(BF16)
    acc = jnp.zeros(h.shape, F32)
    for c in range(D_FF // FF_CHUNK):
        lo = c * FF_CHUNK
        gate = jnp.dot(u, wgu_ref[:, lo:lo + FF_CHUNK], preferred_element_type=F32)
        up = jnp.dot(u, wgu_ref[:, D_FF + lo:D_FF + lo + FF_CHUNK], preferred_element_type=F32)
        act = (gate * _sigmoid(gate) * up).astype(BF16)
        acc = acc + jnp.dot(act, wd_ref[lo:lo + FF_CHUNK, :], preferred_element_type=F32)
    h1 = h + acc
    u2 = _rms(h1, gp_ref[...])
    gate = _sigmoid(_bdot(u2, wgate_ref[...]))
    o_ref[...] = h1 + gate * _bdot(p_ref[...], wple_ref[...])


def _ffn_ple(h, p, g_ffn, w_gu, w_down, g_ple, w_gate, w_ple, mixer_proj=None):
    n = h.shape[0]
    tm = _row_tile(n, FFN_ROWS)
    proj_args = [] if mixer_proj is None else list(mixer_proj)
    proj_specs = [] if mixer_proj is None else [_rows(tm, D), _full((D, D))]
    return pl.pallas_call(
        functools.partial(_ffn_ple_kernel, mixer_proj=mixer_proj is not None),
        out_shape=jax.ShapeDtypeStruct((n, D), F32),
        grid=(n // tm,),
        in_specs=[_rows(tm, D)] + proj_specs + [_rows(tm, PLE_DIM), _full((1, D)), _full((D, 2 * D_FF)),
                                                _full((D_FF, D)), _full((1, D)), _full((D, D)), _full((PLE_DIM, D))],
        out_specs=_rows(tm, D),
        compiler_params=_params("parallel"),
        name="ffn_ple",
    )(h, *proj_args, p, g_ffn, w_gu, w_down, g_ple, w_gate, w_ple)


def _ml_in_kernel(h_ref, g_ref, wq_ref, wk_ref, wv_ref, wo_ref, wg_ref, bg_ref,
                  q_ref, k_ref, v_ref, og_ref, gates_ref):
    uf = _rms(h_ref[...], g_ref[...])
    u = uf.astype(BF16)
    q_ref[...] = jnp.dot(u, wq_ref[...], preferred_element_type=F32)
    k_ref[...] = jnp.dot(u, wk_ref[...], preferred_element_type=F32) * (ML_DQK ** -0.5)
    v_ref[...] = jnp.dot(u, wv_ref[...], preferred_element_type=F32)
    og_ref[...] = _sigmoid(jnp.dot(u, wo_ref[...], preferred_element_type=F32))
    u_lo = (uf - u.astype(F32)).astype(BF16)
    wg = wg_ref[...]
    wg_hi = wg.astype(BF16)
    wg_lo = (wg - wg_hi.astype(F32)).astype(BF16)
    gates = (jnp.dot(u, wg_hi, preferred_element_type=F32) + jnp.dot(u_lo, wg_hi, preferred_element_type=F32)
             + jnp.dot(u, wg_lo, preferred_element_type=F32)) + bg_ref[...]
    is_forget = lax.broadcasted_iota(jnp.int32, gates.shape, 1) >= ML_HEADS
    gates_ref[...] = jnp.where(is_forget, _log_sigmoid(gates), gates)


def _ml_in(h, g, wq, wk, wv, wo, wg, bg):
    n = h.shape[0]
    tm = _row_tile(n)
    hq = ML_HEADS * ML_DQK
    return pl.pallas_call(
        _ml_in_kernel,
        out_shape=[jax.ShapeDtypeStruct((n, hq), F32), jax.ShapeDtypeStruct((n, hq), F32),
                   jax.ShapeDtypeStruct((n, D), F32), jax.ShapeDtypeStruct((n, D), F32),
                   jax.ShapeDtypeStruct((n, 2 * ML_HEADS), F32)],
        grid=(n // tm,),
        in_specs=[_rows(tm, D), _full((1, D)), _full((D, hq)), _full((D, hq)), _full((D, D)), _full((D, D)),
                  _full((D, 2 * ML_HEADS)), _full((1, 2 * ML_HEADS))],
        out_specs=[_rows(tm, hq), _rows(tm, hq), _rows(tm, D), _rows(tm, D), _rows(tm, 2 * ML_HEADS)],
        compiler_params=_params("parallel"),
        name="ml_in",
    )(h, g, wq, wk, wv, wo, wg, bg)


def _head_norm_gate(hh, g, og):
    return og * (hh * lax.rsqrt(jnp.mean(hh * hh, axis=-1, keepdims=True) + EPS) * g)


def _ml_chunk_kernel(q_ref, k_ref, v_ref, og_ref, gc_ref, gr_ref, gout_ref, y_ref, c_ref, n_ref, m_ref):
    @pl.when(pl.program_id(1) == 0)
    def _():
        c_ref[...] = jnp.zeros(c_ref.shape, F32)
        n_ref[...] = jnp.zeros(n_ref.shape, F32)
        m_ref[...] = jnp.zeros(m_ref.shape, F32)

    lc = q_ref.shape[1]
    row = lax.broadcasted_iota(jnp.int32, (lc, lc), 0)
    col = lax.broadcasted_iota(jnp.int32, (lc, lc), 1)
    causal = col <= row
    gates_c = gc_ref[0]
    gates_r = gr_ref[0]
    b_c = jnp.dot(causal.astype(F32), gates_c[:, ML_HEADS:], preferred_element_type=F32, precision=HIGHEST)
    b_r = jnp.dot(gates_r[ML_HEADS:, :], (row <= col).astype(F32), preferred_element_type=F32,
                  precision=HIGHEST)
    lane = lax.broadcasted_iota(jnp.int32, (1, LANES), 1)
    for hd in range(ML_HEADS):
        pair, odd = divmod(hd, 2)
        own = (lane >= ML_DQK) if odd else (lane < ML_DQK)
        cols = slice(pair * LANES, (pair + 1) * LANES)
        qh = jnp.where(own, q_ref[0, :, cols], 0.0)
        k_pair = k_ref[0, :, cols]
        vh = v_ref[0, :, hd * ML_DV:(hd + 1) * ML_DV]
        bc = b_c[:, hd:hd + 1]
        br = b_r[hd:hd + 1, :]
        li_c = gates_c[:, hd:hd + 1]
        li_r = gates_r[hd:hd + 1, :]
        m_prev = m_ref[0, hd:hd + 1, 0:1]
        c_prev = c_ref[0, hd]
        c_pair = c_ref[0, 2 * pair:2 * pair + 2].reshape(2 * ML_DQK, ML_DV)
        n_pair = n_ref[0, pair:pair + 1, :]

        key_side = jnp.where(causal, li_r - br, -jnp.inf)
        top = jnp.maximum(m_prev, jnp.max(key_side, axis=1, keepdims=True))
        m_t = bc + top
        w_intra = jnp.exp(key_side - top)
        w_inter = jnp.exp(m_prev - top)
        s_mat = w_intra * _bdot_nt(qh, k_pair)
        num = w_inter * _bdot(qh, c_pair) + _bdot(s_mat, vh)
        den = w_inter * jnp.sum(qh * n_pair, axis=1, keepdims=True) + jnp.sum(s_mat, axis=1, keepdims=True)
        floor = jnp.maximum(jnp.abs(den), jnp.exp(-m_t))
        scale = lax.rsqrt(jnp.maximum(jnp.mean(num * num, axis=1, keepdims=True) + EPS * floor * floor, F32_TINY))
        y_ref[0, :, hd * ML_DV:(hd + 1) * ML_DV] = (
            og_ref[0, :, hd * ML_DV:(hd + 1) * ML_DV] * (num * scale * gout_ref[:, hd * ML_DV:(hd + 1) * ML_DV]))

        g_end = br[:, lc - 1:lc]
        dl_end = g_end - bc + li_c
        m_new = jnp.maximum(g_end + m_prev, jnp.max(dl_end, axis=0, keepdims=True))
        a_prev = jnp.exp(g_end + m_prev - m_new)
        kw = jnp.where(own, jnp.exp(dl_end - m_new) * k_pair, 0.0)
        update = lax.dot_general(kw.astype(BF16), vh.astype(BF16), (((0,), (0,)), ((), ())),
                                 preferred_element_type=F32)
        c_ref[0, hd] = a_prev * c_prev + update[odd * ML_DQK:(odd + 1) * ML_DQK]
        n_ref[0, pair:pair + 1, :] = jnp.where(own, a_prev * n_pair + jnp.sum(kw, axis=0, keepdims=True), n_pair)
        m_ref[0, hd:hd + 1, :] = jnp.broadcast_to(m_new, (1, LANES))


def _ml_chunk(q, k, v, og, gates, g_out):
    b, l, _ = q.shape
    lc = ML_CHUNK if l % ML_CHUNK == 0 else l
    gates_t = jnp.swapaxes(gates, 1, 2)
    hq = ML_HEADS * ML_DQK
    blk = lambda w: pl.BlockSpec((1, lc, w), lambda i, j: (i, j, 0))
    return pl.pallas_call(
        _ml_chunk_kernel,
        out_shape=[jax.ShapeDtypeStruct((b, l, D), F32),
                   jax.ShapeDtypeStruct((b, ML_HEADS, ML_DQK, ML_DV), F32),
                   jax.ShapeDtypeStruct((b, ML_HEADS // 2, LANES), F32),
                   jax.ShapeDtypeStruct((b, ML_HEADS, LANES), F32)],
        grid=(b, l // lc),
        in_specs=[blk(hq), blk(hq), blk(D), blk(D), blk(2 * ML_HEADS),
                  pl.BlockSpec((1, 2 * ML_HEADS, lc), lambda i, j: (i, 0, j)),
                  pl.BlockSpec((1, D), lambda i, j: (0, 0))],
        out_specs=[blk(D),
                   pl.BlockSpec((1, ML_HEADS, ML_DQK, ML_DV), lambda i, j: (i, 0, 0, 0)),
                   pl.BlockSpec((1, ML_HEADS // 2, LANES), lambda i, j: (i, 0, 0)),
                   pl.BlockSpec((1, ML_HEADS, LANES), lambda i, j: (i, 0, 0))],
        compiler_params=_params("parallel", "arbitrary"),
        name="ml_chunk",
    )(q, k, v, og, gates, gates_t, g_out)


def _ml_step_kernel(qt_ref, kt_ref, v_ref, og_ref, gates_ref, c_ref, nt_ref, m_ref, gout_ref,
                    y_ref, c_out, nt_out, m_out):
    lane = lax.broadcasted_iota(jnp.int32, (1, ML_HEADS), 1)
    for s in range(qt_ref.shape[0]):
        m_row = jnp.zeros((1, ML_HEADS), F32)
        for hd in range(ML_HEADS):
            qc = qt_ref[s, :, hd:hd + 1]
            kc = kt_ref[s, :, hd:hd + 1]
            nc = nt_ref[s, :, hd:hd + 1]
            vh = v_ref[s, :, hd * ML_DV:(hd + 1) * ML_DV]
            c_prev = c_ref[s, hd]
            li = gates_ref[s, :, hd:hd + 1]
            lf = gates_ref[s, :, ML_HEADS + hd:ML_HEADS + hd + 1]
            m_prev = m_ref[s, :, hd:hd + 1]
            inter = lf + m_prev
            m_t = jnp.maximum(inter, li)
            w_intra = jnp.exp(li - m_t)
            w_inter = jnp.exp(inter - m_t)
            qk = w_intra * jnp.sum(qc * kc, axis=0, keepdims=True)
            num = w_inter * jnp.sum(qc * c_prev, axis=0, keepdims=True) + qk * vh
            den = w_inter * jnp.sum(qc * nc, axis=0, keepdims=True) + qk
            hh = num / jnp.maximum(jnp.abs(den), jnp.exp(-m_t))
            y_ref[s, :, hd * ML_DV:(hd + 1) * ML_DV] = _head_norm_gate(
                hh, gout_ref[:, hd * ML_DV:(hd + 1) * ML_DV], og_ref[s, :, hd * ML_DV:(hd + 1) * ML_DV])
            c_out[s, hd] = w_inter * c_prev + w_intra * (kc * vh)
            nt_out[s, :, hd:hd + 1] = w_inter * nc + w_intra * kc
            m_row = jnp.where(lane == hd, m_t, m_row)
        m_out[s] = m_row


def _ml_step(q, k, v, og, gates, c0, n0, m0, g_out):
    b = q.shape[0]
    per = _row_tile(b, ML_STEP_SEQS)
    to_t = lambda a: jnp.swapaxes(a.reshape(b, ML_HEADS, ML_DQK), 1, 2)
    t_spec = pl.BlockSpec((per, ML_DQK, ML_HEADS), lambda i: (i, 0, 0))
    row = lambda w: pl.BlockSpec((per, 1, w), lambda i: (i, 0, 0))
    c_spec = pl.BlockSpec((per, ML_HEADS, ML_DQK, ML_DV), lambda i: (i, 0, 0, 0))
    y, c1, nt1, m1 = pl.pallas_call(
        _ml_step_kernel,
        out_shape=[jax.ShapeDtypeStruct((b, 1, D), F32),
                   jax.ShapeDtypeStruct((b, ML_HEADS, ML_DQK, ML_DV), F32),
                   jax.ShapeDtypeStruct((b, ML_DQK, ML_HEADS), F32),
                   jax.ShapeDtypeStruct((b, 1, ML_HEADS), F32)],
        grid=(b // per,),
        in_specs=[t_spec, t_spec, row(D), row(D), row(2 * ML_HEADS), c_spec, t_spec, row(ML_HEADS),
                  pl.BlockSpec((1, D), lambda i: (0, 0))],
        out_specs=[row(D), c_spec, t_spec, row(ML_HEADS)],
        compiler_params=_params("parallel"),
        name="ml_step",
    )(to_t(q), to_t(k), v.reshape(b, 1, D), og.reshape(b, 1, D), gates.reshape(b, 1, 2 * ML_HEADS),
      c0, jnp.swapaxes(n0, 1, 2), m0.reshape(b, 1, ML_HEADS), g_out)
    return y.reshape(b, D), c1, jnp.swapaxes(nt1, 1, 2), m1.reshape(b, ML_HEADS)


def _pool_project(h, pooled, w_ref, scale_ref, o_ref):
    for g in range(len(POOL_WINDOWS)):
        cols = slice(g * POOL_GW, (g + 1) * POOL_GW)
        o_ref[:, cols] = h[:, cols] + _bdot(pooled[g], w_ref[g]) * scale_ref[:, cols]


def _pool_kernel(h_ref, g_ref, w_ref, scale_ref, o_ref, tail_ref, ext_ref):
    tm = h_ref.shape[1]
    step = pl.program_id(1)

    @pl.when(step == 0)
    def _():
        ext_ref[0:POOL_HALO, :] = jnp.zeros((POOL_HALO, D), F32)

    @pl.when(step > 0)
    def _():
        ext_ref[0:POOL_HALO, :] = ext_ref[tm:tm + POOL_HALO, :]

    h = h_ref[0]
    u = _rms(h, g_ref[...])
    ext_ref[POOL_HALO:POOL_HALO + tm, :] = u
    tail_ref[0] = u[tm - POOL_HALO:, :]
    pos = step * tm + lax.broadcasted_iota(jnp.int32, (tm, 1), 0)
    pooled = []
    for g, w in enumerate(POOL_WINDOWS):
        cols = slice(g * POOL_GW, (g + 1) * POOL_GW)
        wsum = u[:, cols]
        for back in range(1, w):
            wsum = wsum + ext_ref[POOL_HALO - back:POOL_HALO - back + tm, cols]
        cnt = jnp.minimum(pos + 1, w).astype(F32)
        pooled.append(wsum / cnt - u[:, cols])
    _pool_project(h, pooled, w_ref, scale_ref, o_ref.at[0])


def _pool_prompt(h, g, w, scale):
    b, l, _ = h.shape
    tm = _row_tile(l)
    return pl.pallas_call(
        _pool_kernel,
        out_shape=[jax.ShapeDtypeStruct((b, l, D), F32), jax.ShapeDtypeStruct((b, POOL_HALO, D), F32)],
        grid=(b, l // tm),
        in_specs=[pl.BlockSpec((1, tm, D), lambda i, j: (i, j, 0)),
                  pl.BlockSpec((1, D), lambda i, j: (0, 0)),
                  pl.BlockSpec((len(POOL_WINDOWS), POOL_GW, POOL_GW), lambda i, j: (0, 0, 0)),
                  pl.BlockSpec((1, D), lambda i, j: (0, 0))],
        out_specs=[pl.BlockSpec((1, tm, D), lambda i, j: (i, j, 0)),
                   pl.BlockSpec((1, POOL_HALO, D), lambda i, j: (i, 0, 0))],
        scratch_shapes=[pltpu.VMEM((POOL_HALO + tm, D), F32)],
        compiler_params=_params("parallel", "arbitrary"),
        name="pool",
    )(h, g, w, scale)


def _pool_step_kernel(h_ref, pref_ref, g_ref, w_ref, scale_ref, o_ref, state_ref, *, start_pos):
    h = h_ref[...]
    u = _rms(h, g_ref[...])
    pooled = []
    for g, w in enumerate(POOL_WINDOWS):
        cols = slice(g * POOL_GW, (g + 1) * POOL_GW)
        wsum = u[:, cols]
        for back in range(1, w):
            wsum = wsum + pref_ref[POOL_STATE - back, :, cols]
        pooled.append(wsum / float(min(start_pos + 1, w)) - u[:, cols])
    _pool_project(h, pooled, w_ref, scale_ref, o_ref)
    state_ref[0:POOL_STATE - 1] = pref_ref[1:POOL_STATE]
    state_ref[POOL_STATE - 1] = u


def _pool_step(h, prefix, g, w, scale, start_pos):
    b = h.shape[0]
    out, state_t = pl.pallas_call(
        functools.partial(_pool_step_kernel, start_pos=start_pos),
        out_shape=[jax.ShapeDtypeStruct((b, D), F32), jax.ShapeDtypeStruct((POOL_STATE, b, D), F32)],
        compiler_params=pltpu.CompilerParams(vmem_limit_bytes=V7X_VMEM_LIMIT),
        name="pool_step",
    )(h, jnp.swapaxes(prefix, 0, 1), g, w, scale)
    return out, jnp.swapaxes(state_t, 0, 1)


def _pair_norm(x, g):
    first = lax.broadcasted_iota(jnp.int32, x.shape, 1) < SB_DH
    sq = x * x
    s_first = jnp.sum(jnp.where(first, sq, 0.0), axis=1, keepdims=True)
    s_second = jnp.sum(jnp.where(first, 0.0, sq), axis=1, keepdims=True)
    inv = jnp.where(first, lax.rsqrt(s_first * (1.0 / SB_DH) + EPS), lax.rsqrt(s_second * (1.0 / SB_DH) + EPS))
    return x * inv * g


def _sb_qkv_kernel(h_ref, g_ref, wq_ref, wk_ref, wv_ref, gq_ref, gk_ref, q_ref, k_ref, v_ref, *feature_major):
    u = _rms(h_ref[...], g_ref[...]).astype(BF16)
    v = jnp.dot(u, wv_ref[...], preferred_element_type=F32)
    v_ref[...] = v
    q = jnp.dot(u, wq_ref[...], preferred_element_type=F32)
    k = jnp.dot(u, wk_ref[...], preferred_element_type=F32)
    for p in range(D // LANES):
        cols = slice(p * LANES, (p + 1) * LANES)
        q_ref[:, cols] = _pair_norm(q[:, cols], gq_ref[...])
        k_pair = _pair_norm(k[:, cols], gk_ref[...])
        k_ref[:, cols] = k_pair
        if feature_major:
            for x, t_ref in ((k_pair, feature_major[0]), (v[:, cols], feature_major[1])):
                xt = x.T
                t_ref[0, 2 * p] = xt[:SB_DH]
                t_ref[0, 2 * p + 1] = xt[SB_DH:]


def _sb_qkv(h, g, wq, wk, wv, gq, gk, seq_len=None):
    n = h.shape[0]
    tm = _row_tile(n)
    out_shape = [jax.ShapeDtypeStruct((n, D), F32)] * 3
    out_specs = [_rows(tm, D)] * 3
    if seq_len is not None:
        per_seq = seq_len // tm
        out_shape += [jax.ShapeDtypeStruct((n // seq_len, SB_HEADS, SB_DH, seq_len), F32)] * 2
        out_specs += [pl.BlockSpec((1, SB_HEADS, SB_DH, tm), lambda i: (i // per_seq, 0, 0, i % per_seq))] * 2
    return pl.pallas_call(
        _sb_qkv_kernel,
        out_shape=out_shape,
        grid=(n // tm,),
        in_specs=[_rows(tm, D), _full((1, D)), _full((D, D)), _full((D, D)), _full((D, D)),
                  _full((1, LANES)), _full((1, LANES))],
        out_specs=out_specs,
        compiler_params=_params("parallel"),
        name="sb_qkv",
    )(h, g, wq, wk, wv, gq, gk)


def _sb_weights(z, valid):
    softplus = jnp.maximum(z, 0.0) + jnp.log(1.0 + jnp.exp(-jnp.abs(z)))
    log_rest = -softplus
    if valid is not None:
        log_rest = jnp.where(valid, log_rest, 0.0)
    return z - softplus, log_rest


SB_MASKED_LOGIT = -1e30
SB_SOFTPLUS_LINEAR = 64.0


SB_STAGES = 5
SB_UNROLL = 4


def _sb_attn_kernel(blk_q_ref, blk_k_ref, bias_ref, q_ref, k_ref, v_ref, ntri_ref, mask_ref, o_ref,
                    q2_ref, k2_ref, v2_ref, z_ref, sp16_ref, za_ref, inner_ref, zb_ref, a_ref, carry_ref):
    n_pairs = blk_q_ref.shape[0]
    tq = z_ref.shape[2]
    pair = pl.program_id(1)
    lane = lax.broadcasted_iota(jnp.int32, (tq, LANES), 1)
    own = [lane < SB_DH, lane >= SB_DH]

    def prepare_operands(i, _):
        block = pl.ds(pl.multiple_of(i * tq, tq), tq)
        q = q_ref[0, block, :] * (SB_DH ** -0.5 * LOG2E)
        kb = k_ref[0, block, :]
        vb = v_ref[0, block, :]
        for hh in range(2):
            base = SB_DH if hh == 0 else 0
            extra = jnp.where(lane == base, bias_ref[0, 2 * pair + hh],
                              jnp.where(lane == base + 1, bias_ref[1, 2 * pair + hh], 0.0))
            q2_ref[i, hh] = jnp.where(own[hh], q, extra).astype(BF16)
            ones = jnp.where((lane == base) | (lane == base + 1), 1.0, 0.0)
            k2_ref[i, hh] = jnp.where(own[hh], kb, ones).astype(BF16)
            v2_ref[i, hh] = jnp.where(own[hh], vb, 0.0).astype(BF16)
        return 0

    lax.fori_loop(0, q_ref.shape[1] // tq, prepare_operands, 0)
    for ref in (z_ref, sp16_ref, za_ref, inner_ref, zb_ref, a_ref, carry_ref):
        ref[...] = jnp.zeros(ref.shape, ref.dtype)
    o_ref[...] = jnp.zeros(o_ref.shape, F32)

    def pair_at(t, stage):
        idx = t - stage
        live = (idx >= 0) & (idx < n_pairs)
        idx = jnp.clip(idx, 0, n_pairs - 1)
        return blk_q_ref[idx], blk_k_ref[idx], live

    def rows(block):
        return pl.ds(pl.multiple_of(block * tq, tq), tq)

    def logits(t, slot):
        qi, kj, _ = pair_at(t, 0)
        for hh in range(2):
            z_ref[slot, hh] = lax.dot_general(q2_ref[qi, hh], k2_ref[kj, hh], (((1,), (1,)), ((), ())),
                                              preferred_element_type=F32)

    def softplus(t, slot):
        qi, kj, _ = pair_at(t, 1)
        mask = mask_ref[(qi == kj).astype(jnp.int32)]
        for hh in range(2):
            z = z_ref[slot, hh] + mask
            sp = jnp.maximum(z, jnp.log(1.0 + jnp.exp2(jnp.minimum(z, SB_SOFTPLUS_LINEAR))) * LOG2E)
            za_ref[slot, hh] = z - sp
            sp16_ref[slot, hh] = sp.astype(BF16)

    def key_sums(t, slot):
        for hh in range(2):
            inner_ref[slot, hh] = jnp.dot(sp16_ref[slot, hh], ntri_ref[...], preferred_element_type=F32)
            zb_ref[slot, hh] = za_ref[slot, hh]

    def weights(t, slot):
        qi, _, live = pair_at(t, 3)
        for hh in range(2):
            carry = carry_ref[qi, hh]
            inner = inner_ref[slot, hh]
            a_ref[slot, hh] = jnp.exp2(zb_ref[slot, hh] + inner + carry).astype(BF16)
            block_sum = inner[:, 0:1] - sp16_ref[slot, hh, :, 0:1].astype(F32)
            carry_ref[qi, hh] = jnp.where(live, carry + block_sum, carry)

    def values(t, slot):
        qi, kj, live = pair_at(t, 4)
        parts = [jnp.dot(a_ref[slot, hh], v2_ref[kj, hh], preferred_element_type=F32) for hh in range(2)]
        o_ref[0, rows(qi), :] += jnp.where(live, parts[0] + parts[1], 0.0)

    def steps(p, _):
        for sub in range(SB_UNROLL):
            t = SB_UNROLL * p + sub
            even = sub % 2
            logits(t, even)
            key_sums(t, even)
            values(t, even)
            weights(t, 1 - even)
            softplus(t, 1 - even)
        return 0

    lax.fori_loop(0, pl.cdiv(n_pairs + SB_STAGES - 1, SB_UNROLL), steps, 0)


def _later_key_matrix(n):
    r = jnp.arange(n)
    return (r[:, None] > r[None, :]).astype(F32)


def _sb_attn_prompt(q, k, v, b_logit):
    b, l, _ = q.shape
    tq = SB_BLOCK if l % SB_BLOCK == 0 else l
    n_q = l // tq
    pairs = [(i, j) for i in range(n_q) for j in range(i, -1, -1)]
    blk_q = jnp.asarray([i for i, _ in pairs], jnp.int32)
    blk_k = jnp.asarray([j for _, j in pairs], jnp.int32)
    seq = pl.BlockSpec((1, l, LANES), lambda i, p, *_: (i, 0, p))
    ntri = -_later_key_matrix(tq).astype(BF16)
    r = jnp.arange(tq)
    mask = jnp.stack([jnp.zeros((tq, tq), F32), jnp.where(r[None, :] < r[:, None], 0.0, SB_MASKED_LOGIT)])
    bias2 = b_logit * LOG2E
    bias_hi = bias2.astype(BF16).astype(F32)
    bias_split = jnp.stack([bias_hi, bias2 - bias_hi])
    slots = lambda *shape: (2, 2) + shape
    return pl.pallas_call(
        _sb_attn_kernel,
        out_shape=jax.ShapeDtypeStruct((b, l, D), F32),
        grid_spec=pltpu.PrefetchScalarGridSpec(
            num_scalar_prefetch=2,
            grid=(b, SB_HEADS // 2),
            in_specs=[pl.BlockSpec(memory_space=pltpu.SMEM), seq, seq, seq,
                      pl.BlockSpec((tq, tq), lambda i, p, *_: (0, 0)),
                      pl.BlockSpec((2, tq, tq), lambda i, p, *_: (0, 0, 0))],
            out_specs=seq,
            scratch_shapes=[pltpu.VMEM((n_q, 2, tq, LANES), BF16),
                            pltpu.VMEM((n_q, 2, tq, LANES), BF16),
                            pltpu.VMEM((n_q, 2, tq, LANES), BF16),
                            pltpu.VMEM(slots(tq, tq), F32),
                            pltpu.VMEM(slots(tq, tq), BF16),
                            pltpu.VMEM(slots(tq, tq), F32),
                            pltpu.VMEM(slots(tq, tq), F32),
                            pltpu.VMEM(slots(tq, tq), F32),
                            pltpu.VMEM(slots(tq, tq), BF16),
                            pltpu.VMEM((n_q, 2, tq, 1), F32)]),
        compiler_params=_params("parallel", "parallel"),
        name="sb_attn",
    )(blk_q, blk_k, bias_split, q, k, v, ntri, mask)


def _sb_paged_kernel(pt_ref, qt_ref, bias_ref, *refs):
    k_refs, v_refs = refs[:PAGES_PER_STEP], refs[PAGES_PER_STEP:2 * PAGES_PER_STEP]
    tri_ref, o_ref, qb_ref, a_ref, acc_ref, carry_ref = refs[2 * PAGES_PER_STEP:]
    step = pl.program_id(1)

    @pl.when(step == 0)
    def _():
        qt = qt_ref[0] * (SB_DH ** -0.5)
        for h in range(SB_HEADS):
            qb_ref[h] = jnp.broadcast_to(qt[:, h:h + 1], (SB_DH, PAGE_SIZE))
        acc_ref[...] = jnp.zeros(acc_ref.shape, F32)
        carry_ref[...] = jnp.zeros(carry_ref.shape, F32)

    head = lax.broadcasted_iota(jnp.int32, (SB_HEADS, PAGE_SIZE), 0)
    for k_ref, v_ref in zip(k_refs, v_refs):
        z = jnp.zeros((SB_HEADS, PAGE_SIZE), F32)
        for h in range(SB_HEADS):
            z_h = jnp.sum(k_ref[0, h] * qb_ref[h], axis=0, keepdims=True)
            z = jnp.where(head == h, z_h, z)
        log_beta, log_rest = _sb_weights(z + bias_ref[...], None)
        hi = log_rest.astype(BF16)
        rem = log_rest - hi.astype(F32)
        mid = rem.astype(BF16)
        lo = (rem - mid.astype(F32)).astype(BF16)
        terms = jnp.dot(jnp.concatenate([hi, mid, lo], axis=0), tri_ref[...], preferred_element_type=F32)
        inner = terms[:SB_HEADS] + terms[SB_HEADS:2 * SB_HEADS] + terms[2 * SB_HEADS:]
        carry = carry_ref[:, 0:1]
        a_ref[...] = jnp.exp(log_beta + inner + carry)
        carry_ref[...] = jnp.broadcast_to(carry + jnp.sum(log_rest, axis=1, keepdims=True), carry_ref.shape)
        for h in range(SB_HEADS):
            acc_ref[h] += v_ref[0, h] * a_ref[h:h + 1, :]

    @pl.when(step == pl.num_programs(1) - 1)
    def _():
        diag = (lax.broadcasted_iota(jnp.int32, (SB_DH, LANES), 0)
                == lax.broadcasted_iota(jnp.int32, (SB_DH, LANES), 1))
        for h in range(SB_HEADS):
            total = jnp.sum(acc_ref[h], axis=1, keepdims=True)
            o_ref[0, h:h + 1, :] = jnp.sum(jnp.where(diag, total, 0.0), axis=0, keepdims=True)


def _sb_attn_paged(q, cache_k, cache_v, page_table, b_logit):
    b = q.shape[0]
    n_pages = page_table.shape[1]
    steps = n_pages // PAGES_PER_STEP
    kt = jnp.transpose(cache_k, (0, 2, 3, 1))
    vt = jnp.transpose(cache_v, (0, 2, 3, 1))
    qt = jnp.swapaxes(q.reshape(b, SB_HEADS, SB_DH), 1, 2)

    def page_spec(t):
        return pl.BlockSpec((1, SB_HEADS, SB_DH, PAGE_SIZE),
                            lambda i, j, pt: (pt[i * n_pages + n_pages - 1 - (j * PAGES_PER_STEP + t)], 0, 0, 0))

    pages = [page_spec(t) for t in range(PAGES_PER_STEP)]
    out = pl.pallas_call(
        _sb_paged_kernel,
        out_shape=jax.ShapeDtypeStruct((b, SB_HEADS, LANES), F32),
        grid_spec=pltpu.PrefetchScalarGridSpec(
            num_scalar_prefetch=1,
            grid=(b, steps),
            in_specs=[pl.BlockSpec((1, SB_DH, SB_HEADS), lambda i, j, pt: (i, 0, 0)),
                      pl.BlockSpec((SB_HEADS, 1), lambda i, j, pt: (0, 0))] + pages + pages
                     + [pl.BlockSpec((PAGE_SIZE, PAGE_SIZE), lambda i, j, pt: (0, 0))],
            out_specs=pl.BlockSpec((1, SB_HEADS, LANES), lambda i, j, pt: (i, 0, 0)),
            scratch_shapes=[pltpu.VMEM((SB_HEADS, SB_DH, PAGE_SIZE), F32), pltpu.VMEM((SB_HEADS, PAGE_SIZE), F32),
                            pltpu.VMEM((SB_HEADS, SB_DH, LANES), F32), pltpu.VMEM((SB_HEADS, LANES), F32)]),
        compiler_params=_params("parallel", "arbitrary"),
        name="sb_paged",
    )(page_table.reshape(-1), qt, b_logit.reshape(SB_HEADS, 1),
      *([kt] * PAGES_PER_STEP), *([vt] * PAGES_PER_STEP), _later_key_matrix(PAGE_SIZE).astype(BF16))
    return out[:, :, :SB_DH].reshape(b, D)


def _glu(h, g, w_ref, b_ref):
    a = _bdot(_rms(h, g), w_ref[...]) + b_ref[...]
    return a[:, :D] * _sigmoid(a[:, D:])


def _conv_tail(h, c, gln_ref, bln_ref, wpw_ref, bpw_ref):
    cc = c - jnp.mean(c, axis=-1, keepdims=True)
    y = cc * lax.rsqrt(jnp.mean(cc * cc, axis=-1, keepdims=True) + EPS) * gln_ref[...] + bln_ref[...]
    y = y * _sigmoid(y)
    return h + _bdot(y, wpw_ref[...]) + bpw_ref[...]


def _conv_kernel(h_ref, g_ref, wglu_ref, bglu_ref, wdw_ref, bdw_ref, gln_ref, bln_ref, wpw_ref, bpw_ref,
                 o_ref, tail_ref, ext_ref, win_ref):
    tm = h_ref.shape[1]
    step = pl.program_id(1)

    @pl.when(step == 0)
    def _():
        ext_ref[0:CONV_HALO, :] = jnp.zeros((CONV_HALO, D), F32)

    @pl.when(step > 0)
    def _():
        ext_ref[0:CONV_HALO, :] = ext_ref[tm:tm + CONV_HALO, :]

    h = h_ref[0]
    glu = _glu(h, g_ref[...], wglu_ref, bglu_ref)
    ext_ref[CONV_HALO:CONV_HALO + tm, :] = glu
    tail_ref[0] = glu[tm - CONV_HALO:, :]
    c = jnp.broadcast_to(bdw_ref[...], (tm, D))
    first = CONV_HALO - CONV_STATE
    for shift in range(SUBLANES):
        last = max(lo for lo in range(shift, CONV_HALO + 1, SUBLANES))
        if shift == 0:
            window_ref = ext_ref
        else:
            window_ref = win_ref
            win_ref[0:last - shift + tm, :] = ext_ref[shift:last + tm, :]
        for lo in range(shift, last + 1, SUBLANES):
            j = lo - first
            if 0 <= j < CONV_WIDTH:
                c = c + window_ref[lo - shift:lo - shift + tm, :] * wdw_ref[j:j + 1, :]
    o_ref[0] = _conv_tail(h, c, gln_ref, bln_ref, wpw_ref, bpw_ref)


def _conv_prompt(h, g, w_glu, b_glu, w_dw, b_dw, g_ln, b_ln, w_pw, b_pw):
    b, l, _ = h.shape
    tm = _row_tile(l)
    const = lambda shape: pl.BlockSpec(shape, lambda i, j: (0,) * len(shape), pipeline_mode=pl.Buffered(1))
    return pl.pallas_call(
        _conv_kernel,
        out_shape=[jax.ShapeDtypeStruct((b, l, D), F32), jax.ShapeDtypeStruct((b, CONV_HALO, D), F32)],
        grid=(b, l // tm),
        in_specs=[pl.BlockSpec((1, tm, D), lambda i, j: (i, j, 0)),
                  const((1, D)), const((D, 2 * D)), const((1, 2 * D)), const((CONV_WIDTH, D)), const((1, D)),
                  const((1, D)), const((1, D)), const((D, D)), const((1, D))],
        out_specs=[pl.BlockSpec((1, tm, D), lambda i, j: (i, j, 0)),
                   pl.BlockSpec((1, CONV_HALO, D), lambda i, j: (i, 0, 0))],
        scratch_shapes=[pltpu.VMEM((CONV_HALO + tm, D), F32), pltpu.VMEM((CONV_HALO + tm, D), F32)],
        compiler_params=_params("parallel", "arbitrary"),
        name="conv",
    )(h, g, w_glu, b_glu, w_dw, b_dw, g_ln, b_ln, w_pw, b_pw)


def _conv_step_kernel(h_ref, pref_ref, g_ref, wglu_ref, bglu_ref, wdw_ref, bdw_ref, gln_ref, bln_ref,
                      wpw_ref, bpw_ref, o_ref, state_ref):
    h = h_ref[...]
    glu = _glu(h, g_ref[...], wglu_ref, bglu_ref)
    c = bdw_ref[...] + glu * wdw_ref[CONV_STATE:CONV_WIDTH, :]
    for j in range(CONV_STATE):
        c = c + pref_ref[j] * wdw_ref[j:j + 1, :]
    o_ref[...] = _conv_tail(h, c, gln_ref, bln_ref, wpw_ref, bpw_ref)
    state_ref[0:CONV_STATE - 1] = pref_ref[1:CONV_STATE]
    state_ref[CONV_STATE - 1] = glu


def _conv_step(h, prefix, g, w_glu, b_glu, w_dw, b_dw, g_ln, b_ln, w_pw, b_pw):
    b = h.shape[0]
    out, state_t = pl.pallas_call(
        _conv_step_kernel,
        out_shape=[jax.ShapeDtypeStruct((b, D), F32), jax.ShapeDtypeStruct((CONV_STATE, b, D), F32)],
        compiler_params=pltpu.CompilerParams(vmem_limit_bytes=V7X_VMEM_LIMIT),
        name="conv_step",
    )(h, jnp.swapaxes(prefix, 0, 1), g, w_glu, b_glu, w_dw, b_dw, g_ln, b_ln, w_pw, b_pw)
    return out, jnp.swapaxes(state_t, 0, 1)


def _prepare_weights(w):
    hq = ML_HEADS * ML_DQK
    row = lambda a: a.reshape(a.shape[0], 1, -1)
    ml_w_in = w['ml_w_in']
    sb_w = w['sb_w_qkv'].astype(BF16)
    tile_pair = lambda g: jnp.tile(g, (1, 2)).reshape(g.shape[0], 1, LANES)
    return dict(
        norm_mix=row(w['norm_mix']), norm_ffn=row(w['norm_ffn']), norm_ple=row(w['norm_ple']),
        w_ffn_gu=w['w_ffn_gu'].astype(BF16), w_ffn_down=w['w_ffn_down'].astype(BF16),
        w_ple=w['w_ple'].astype(BF16), w_ple_gate=w['w_ple_gate'].astype(BF16),
        ml_wq=ml_w_in[:, :, :hq].astype(BF16), ml_wk=ml_w_in[:, :, hq:2 * hq].astype(BF16),
        ml_wv=ml_w_in[:, :, 2 * hq:2 * hq + D].astype(BF16),
        ml_wo=ml_w_in[:, :, 2 * hq + D:2 * hq + 2 * D].astype(BF16),
        ml_wg=ml_w_in[:, :, 2 * hq + 2 * D:], ml_b_gate=row(w['ml_b_gate']),
        ml_g_out=w['ml_g_out'].reshape(-1, 1, D), ml_w_out=w['ml_w_out'].astype(BF16),
        pl_w=w['pl_w'].astype(BF16), pl_scale=row(w['pl_scale']),
        sb_wq=sb_w[:, :, :D], sb_wk=sb_w[:, :, D:2 * D], sb_wv=sb_w[:, :, 2 * D:],
        sb_g_q=tile_pair(w['sb_g_q']), sb_g_k=tile_pair(w['sb_g_k']), sb_b_logit=w['sb_b_logit'],
        sb_w_out=w['sb_w_out'].astype(BF16),
        cv_w_glu=w['cv_w_glu'].astype(BF16), cv_b_glu=row(w['cv_b_glu']), cv_w_dw=w['cv_w_dw'],
        cv_b_dw=row(w['cv_b_dw']), cv_g_ln=row(w['cv_g_ln']), cv_b_ln=row(w['cv_b_ln']),
        cv_w_pw=w['cv_w_pw'].astype(BF16), cv_b_pw=row(w['cv_b_pw']),
    )


def _trunk(x, p, w, state):
    b, l, _ = x.shape
    n = b * l
    h = x.reshape(n, D)
    depth = p.shape[0]
    new = {}
    for i in range(depth):
        kind, j = i % 4, i // 4
        g_mix = w['norm_mix'][i]
        mixer_proj = None
        if kind == 0:
            q, k, v, og, gates = _ml_in(h, g_mix, w['ml_wq'][j], w['ml_wk'][j], w['ml_wv'][j], w['ml_wo'][j],
                                        w['ml_wg'][j], w['ml_b_gate'][j])
            if state is None:
                y, c1, n1, m1 = _ml_chunk(q.reshape(b, l, -1), k.reshape(b, l, -1), v.reshape(b, l, D),
                                          og.reshape(b, l, D), gates.reshape(b, l, -1), w['ml_g_out'][j])
                n1, m1 = n1.reshape(b, ML_HEADS, ML_DQK), m1[:, :, 0]
            else:
                y, c1, n1, m1 = _ml_step(q, k, v, og, gates, state['ml_c'][j], state['ml_n'][j],
                                         state['ml_m'][j], w['ml_g_out'][j])
            new.setdefault('ml_c', []).append(c1)
            new.setdefault('ml_n', []).append(n1)
            new.setdefault('ml_m', []).append(m1)
            mixer_proj = (y.reshape(n, D), w['ml_w_out'][j])
        elif kind == 1:
            if state is None:
                h3, tail = _pool_prompt(h.reshape(b, l, D), g_mix, w['pl_w'][j], w['pl_scale'][j])
                h, st = h3.reshape(n, D), tail[:, POOL_HALO - POOL_STATE:]
            else:
                h, st = _pool_step(h, state['pool'][j], g_mix, w['pl_w'][j], w['pl_scale'][j], state['start_pos'])
            new.setdefault('pool', []).append(st)
        elif kind == 2:
            sb = (h, g_mix, w['sb_wq'][j], w['sb_wk'][j], w['sb_wv'][j], w['sb_g_q'][j], w['sb_g_k'][j])
            if state is None:
                q, k, v, kt, vt = _sb_qkv(*sb, seq_len=l)
                att = _sb_attn_prompt(q.reshape(b, l, D), k.reshape(b, l, D), v.reshape(b, l, D),
                                      w['sb_b_logit'][j]).reshape(n, D)
                k_new, v_new = jnp.transpose(kt, (0, 3, 1, 2)), jnp.transpose(vt, (0, 3, 1, 2))
            else:
                q, k, v = _sb_qkv(*sb)
                att = _sb_attn_paged(q, state['sb_k'][j], state['sb_v'][j], state['page_table'], w['sb_b_logit'][j])
                k_new, v_new = k.reshape(b, l, SB_HEADS, SB_DH), v.reshape(b, l, SB_HEADS, SB_DH)
            new.setdefault('sb_k', []).append(k_new)
            new.setdefault('sb_v', []).append(v_new)
            mixer_proj = (att, w['sb_w_out'][j])
        else:
            cv = (g_mix, w['cv_w_glu'][j], w['cv_b_glu'][j], w['cv_w_dw'][j], w['cv_b_dw'][j], w['cv_g_ln'][j],
                  w['cv_b_ln'][j], w['cv_w_pw'][j], w['cv_b_pw'][j])
            if state is None:
                h3, tail = _conv_prompt(h.reshape(b, l, D), *cv)
                h, st = h3.reshape(n, D), tail[:, CONV_HALO - CONV_STATE:]
            else:
                h, st = _conv_step(h, state['conv'][j], *cv)
            new.setdefault('conv', []).append(st)
        h = _ffn_ple(h, p[i].reshape(n, PLE_DIM), w['norm_ffn'][i], w['w_ffn_gu'][i], w['w_ffn_down'][i],
                     w['norm_ple'][i], w['w_ple_gate'][i], w['w_ple'][i], mixer_proj)
    stacked = tuple(jnp.stack(new[name]) for name in ('ml_c', 'ml_n', 'ml_m', 'pool', 'sb_k', 'sb_v', 'conv'))
    return (h.reshape(b, l, D),) + stacked


def kernel(x_prompt, x_sample, state_mlstm_C, state_mlstm_n, state_mlstm_m, state_pool, cache_sb_k, cache_sb_v,
           state_conv, page_table, p_prompt, p_sample, norm_mix, norm_ffn, w_ffn_gu, w_ffn_down, w_ple, norm_ple,
           w_ple_gate, ml_w_in, ml_b_gate, ml_g_out, ml_w_out, pl_w, pl_scale, sb_w_qkv, sb_g_q, sb_g_k, sb_b_logit,
           sb_w_out, cv_w_glu, cv_b_glu, cv_w_dw, cv_b_dw, cv_g_ln, cv_b_ln, cv_w_pw, cv_b_pw):
    w = _prepare_weights(dict(
        norm_mix=norm_mix, norm_ffn=norm_ffn, w_ffn_gu=w_ffn_gu, w_ffn_down=w_ffn_down, w_ple=w_ple,
        norm_ple=norm_ple, w_ple_gate=w_ple_gate, ml_w_in=ml_w_in, ml_b_gate=ml_b_gate, ml_g_out=ml_g_out,
        ml_w_out=ml_w_out, pl_w=pl_w, pl_scale=pl_scale, sb_w_qkv=sb_w_qkv, sb_g_q=sb_g_q, sb_g_k=sb_g_k,
        sb_b_logit=sb_b_logit, sb_w_out=sb_w_out, cv_w_glu=cv_w_glu, cv_b_glu=cv_b_glu, cv_w_dw=cv_w_dw,
        cv_b_dw=cv_b_dw, cv_g_ln=cv_g_ln, cv_b_ln=cv_b_ln, cv_w_pw=cv_w_pw, cv_b_pw=cv_b_pw))
    prompt = _trunk(x_prompt, p_prompt, w, None)
    sample_state = dict(ml_c=state_mlstm_C, ml_n=state_mlstm_n, ml_m=state_mlstm_m, pool=state_pool,
                        sb_k=cache_sb_k, sb_v=cache_sb_v, page_table=page_table, conv=state_conv,
                        start_pos=page_table.shape[1] * PAGE_SIZE)
    sample = _trunk(x_sample, p_sample, w, sample_state)
    return (prompt[0], sample[0]) + prompt[1:] + sample[1:]
```
